```python
import jax, jax.numpy as jnp
from jax import lax
import numpy as np

D_MODEL = 1024
BATCH = 2
SEQ = 8192
DEPTH = 4
DEC_BATCH = 128
DEC_SEQ = 8
PAST_LEN = 8192
PAGE_SIZE = 128

N_MIXERS = 2
N_MLA = (DEPTH + 1) // 2
N_DSA = DEPTH // 2
HEAD_DIM = 64
N_HEADS = 12
MEM_HEADS = 4
N_MEM = 256
D_MIX = (N_HEADS + MEM_HEADS) * HEAD_DIM
Q_LORA = 384
KV_LORA = 256
QK_NOPE = 64
QK_ROPE = 32
V_DIM = 64
MLA_SCALE = (QK_NOPE + QK_ROPE) ** -0.5
MLA_IN = Q_LORA + KV_LORA + QK_ROPE + MEM_HEADS * HEAD_DIM
KV_HEADS = 4
IDX_HEADS = 8
IDX_DIM = 64
TOPK_MAX = 256
ROT_DIM = HEAD_DIM // 4
IDX_ROT_DIM = IDX_DIM // 4
DSA_WIDTHS = [N_HEADS * HEAD_DIM, KV_HEADS * HEAD_DIM, KV_HEADS * HEAD_DIM, IDX_HEADS * IDX_DIM, IDX_DIM, IDX_HEADS]
DSA_SPLITS = [sum(DSA_WIDTHS[:n + 1]) for n in range(len(DSA_WIDTHS))]
DSA_IN = DSA_SPLITS[-1] + MEM_HEADS * HEAD_DIM
ROPE_THETA = 500000.0
N_EXPERTS = 16
N_GROUPS = 4
EXPERTS_PER_GROUP = N_EXPERTS // N_GROUPS
GROUP_SCORE_TOPK = 2
TOP_K = 2
D_EXPERT = 256
DEEPNORM_ALPHA = (2 * DEPTH) ** 0.25
DEEPNORM_BETA = (8 * DEPTH) ** -0.25
QBLOCK = 128
LN_EPS = 1e-5
RMS_EPS = 1e-6
NEG = -1e30

kernel_name = 'hybrid_mla_dsa_memory_groupmoe_step'


def layer_norm(x, g, b):
    xf = x.astype(jnp.float32)
    mu = xf.mean(-1, keepdims=True)
    var = jnp.square(xf - mu).mean(-1, keepdims=True)
    return ((xf - mu) * lax.rsqrt(var + LN_EPS) * g.astype(jnp.float32) + b.astype(jnp.float32)).astype(x.dtype)


def rms_norm(x, g):
    xf = x.astype(jnp.float32)
    return (xf * lax.rsqrt(jnp.mean(xf * xf, -1, keepdims=True) + RMS_EPS) * g.astype(jnp.float32)).astype(x.dtype)


def rope(x, pos, rot_dim):
    half = rot_dim // 2
    inv = ROPE_THETA ** (-jnp.arange(half, dtype=jnp.float32) / half)
    ang = pos.astype(jnp.float32)[:, None] * inv
    cos = jnp.cos(ang)[None, :, None, :]
    sin = jnp.sin(ang)[None, :, None, :]
    xf = x.astype(jnp.float32)
    x1, x2 = xf[..., :half], xf[..., half:rot_dim]
    out = jnp.concatenate([x1 * cos - x2 * sin, x2 * cos + x1 * sin, xf[..., rot_dim:]], axis=-1)
    return out.astype(x.dtype)


def masked_softmax(s, mask):
    return jax.nn.softmax(jnp.where(mask, s.astype(jnp.float32), NEG), axis=-1)


def gather_rows(a, idx):
    return jax.vmap(lambda ab, ib: ab[ib])(a, idx)


def mem_attend(q, mk, mv):
    s = jnp.einsum('bthd,bmhd->bhtm', q, mk).astype(jnp.float32) * HEAD_DIM ** -0.5
    p = jax.nn.softmax(s, axis=-1).astype(q.dtype)
    o = jnp.einsum('bhtm,bmhd->bthd', p, mv)
    return o.reshape(q.shape[0], q.shape[1], MEM_HEADS * HEAD_DIM)


def causal_block_attention(q, k, v, scale):
    B, S, H, Dk = q.shape
    nb = S // QBLOCK
    qb = q.reshape(B, nb, QBLOCK, H, Dk).swapaxes(0, 1)
    kpos = jnp.arange(S)

    def one(args):
        i, qi = args
        qpos = i * QBLOCK + jnp.arange(QBLOCK)
        s = jnp.einsum('bqhd,bshd->bhqs', qi, k) * scale
        p = masked_softmax(s, kpos[None, :] <= qpos[:, None]).astype(v.dtype)
        return jnp.einsum('bhqs,bshd->bqhd', p, v)

    o = lax.map(one, (jnp.arange(nb), qb))
    return o.swapaxes(0, 1).reshape(B, S, H, v.shape[-1])


def mla_project(x, pos, w_in, q_norm, kv_norm, w_uq):
    B, T, _ = x.shape
    c_q, c_kv, k_rope, q_mem = jnp.split(x @ w_in, [Q_LORA, Q_LORA + KV_LORA, Q_LORA + KV_LORA + QK_ROPE], axis=-1)
    q = jnp.einsum('btc,chd->bthd', rms_norm(c_q, q_norm), w_uq)
    q_nope = q[..., :QK_NOPE]
    q_rope = rope(q[..., QK_NOPE:], pos, QK_ROPE)
    ckv = rms_norm(c_kv, kv_norm)
    krope = rope(k_rope[:, :, None, :], pos, QK_ROPE)[:, :, 0]
    return q_nope, q_rope, ckv, krope, q_mem.reshape(B, T, MEM_HEADS, HEAD_DIM)


def mla_prompt(q_nope, q_rope, ckv, krope, w_uk, w_uv):
    B, S = ckv.shape[:2]
    k_nope = jnp.einsum('bsc,chn->bshn', ckv, w_uk)
    v = jnp.einsum('bsc,chv->bshv', ckv, w_uv)
    q = jnp.concatenate([q_nope, q_rope], axis=-1)
    k = jnp.concatenate([k_nope, jnp.broadcast_to(krope[:, :, None, :], (B, S, N_HEADS, QK_ROPE))], axis=-1)
    return causal_block_attention(q, k, v, MLA_SCALE).reshape(B, S, N_HEADS * V_DIM)


def mla_sample(q_nope, q_rope, ckv_new, krope_new, pool_ckv, pool_krope, j, page_table, w_uk, w_uv):
    DB, T = ckv_new.shape[:2]
    past = page_table.shape[1] * PAGE_SIZE
    ckv = jnp.concatenate([pool_ckv[j, page_table].reshape(DB, past, KV_LORA), ckv_new], axis=1)
    kr = jnp.concatenate([pool_krope[j, page_table].reshape(DB, past, QK_ROPE), krope_new], axis=1)
    q_lat = jnp.einsum('bthn,chn->bthc', q_nope, w_uk)
    s = (jnp.einsum('bthc,bsc->bhts', q_lat, ckv) + jnp.einsum('bthr,bsr->bhts', q_rope, kr)) * MLA_SCALE
    causal = jnp.arange(past + T)[None, :] <= (past + jnp.arange(T))[:, None]
    p = masked_softmax(s, causal).astype(ckv.dtype)
    o_lat = jnp.einsum('bhts,bsc->bthc', p, ckv)
    return jnp.einsum('bthc,chv->bthv', o_lat, w_uv).reshape(DB, T, N_HEADS * V_DIM)


def dsa_project(x, pos, w_in, kidx_g, kidx_b):
    B, T, _ = x.shape
    q, k, v, q_idx, k_idx, w_idx, q_mem = jnp.split(x @ w_in, DSA_SPLITS, axis=-1)
    q = rope(q.reshape(B, T, N_HEADS, HEAD_DIM), pos, ROT_DIM)
    k = rope(k.reshape(B, T, KV_HEADS, HEAD_DIM), pos, ROT_DIM)
    v = v.reshape(B, T, KV_HEADS, HEAD_DIM)
    q_idx = rope(q_idx.reshape(B, T, IDX_HEADS, IDX_DIM), pos, IDX_ROT_DIM)
    k_idx = rope(layer_norm(k_idx, kidx_g, kidx_b)[:, :, None, :], pos, IDX_ROT_DIM)[:, :, 0]
    return q, k, v, q_idx, w_idx, k_idx, q_mem.reshape(B, T, MEM_HEADS, HEAD_DIM)


def indexer_scores(q_idx, w_idx, k_idx):
    dots = jnp.einsum('bthd,bsd->bhts', q_idx, k_idx).astype(jnp.float32)
    return jnp.einsum('bhts,bth->bts', jax.nn.relu(dots), w_idx.astype(jnp.float32))


def sparse_gqa_attend(q, ksel, vsel, valid):
    B, T = q.shape[:2]
    qg = q.reshape(B, T, KV_HEADS, N_HEADS // KV_HEADS, HEAD_DIM)
    s = jnp.einsum('btkgd,btjkd->btkgj', qg, ksel) * HEAD_DIM ** -0.5
    p = masked_softmax(s, valid[:, :, None, None, :]).astype(q.dtype)
    o = jnp.einsum('btkgj,btjkd->btkgd', p, vsel)
    return o.reshape(B, T, N_HEADS * HEAD_DIM)


def dsa_prompt(q, k, v, q_idx, w_idx, k_idx):
    B, S = q.shape[:2]
    topk = min(TOPK_MAX, S // 4)
    nb = S // QBLOCK
    kpos = jnp.arange(S)

    def blocks(a):
        return a.reshape((B, nb, QBLOCK) + a.shape[2:]).swapaxes(0, 1)

    def one(args):
        i, qi, qii, wi = args
        qpos = i * QBLOCK + jnp.arange(QBLOCK)
        causal = kpos[None, :] <= qpos[:, None]
        sc = jnp.where(causal[None], indexer_scores(qii, wi, k_idx), NEG)
        _, sel = lax.top_k(sc, topk)
        valid = sel <= qpos[None, :, None]
        return sparse_gqa_attend(qi, gather_rows(k, sel), gather_rows(v, sel), valid)

    o = lax.map(one, (jnp.arange(nb), blocks(q), blocks(q_idx), blocks(w_idx)))
    return o.swapaxes(0, 1).reshape(B, S, N_HEADS * HEAD_DIM)


def dsa_sample(q, k_new, v_new, q_idx, w_idx, kidx_new, pool_k, pool_v, pool_kidx, j, page_table):
    DB, T = q.shape[:2]
    past = page_table.shape[1] * PAGE_SIZE
    L = past + T
    topk = min(TOPK_MAX, L // 4)
    kidx_all = jnp.concatenate([pool_kidx[j, page_table].reshape(DB, past, IDX_DIM), kidx_new], axis=1)
    qpos = past + jnp.arange(T)
    causal = jnp.arange(L)[None, :] <= qpos[:, None]
    sc = jnp.where(causal[None], indexer_scores(q_idx, w_idx, kidx_all), NEG)
    _, sel = lax.top_k(sc, topk)
    valid = sel <= qpos[None, :, None]
    in_past = (sel < past)[..., None, None]
    sel_past = jnp.minimum(sel, past - 1)
    page = jax.vmap(lambda pt, s: pt[s // PAGE_SIZE])(page_table, sel_past)
    slot = sel_past % PAGE_SIZE
    sel_new = jnp.clip(sel - past, 0, T - 1)
    ksel = jnp.where(in_past, pool_k[j, page, slot], gather_rows(k_new, sel_new))
    vsel = jnp.where(in_past, pool_v[j, page, slot], gather_rows(v_new, sel_new))
    return sparse_gqa_attend(q, ksel, vsel, valid)


def moe(x, w_router, router_bias, w_gate, w_up, w_down):
    shp = x.shape
    xt = x.reshape(-1, shp[-1])
    s = jax.nn.sigmoid((xt @ w_router).astype(jnp.float32))
    sb = (s + router_bias.astype(jnp.float32)).reshape(-1, N_GROUPS, EXPERTS_PER_GROUP)
    gscore = lax.top_k(sb, GROUP_SCORE_TOPK)[0].sum(-1)
    gsel = jnp.argmax(gscore, axis=-1)
    in_group = (gsel[:, None] == jnp.arange(N_GROUPS))[:, :, None]
    _, eidx = lax.top_k(jnp.where(in_group, sb, NEG).reshape(-1, N_EXPERTS), TOP_K)
    w = jnp.take_along_axis(s, eidx, axis=-1)
    w = w / w.sum(-1, keepdims=True)
    gate = (jax.nn.one_hot(eidx, N_EXPERTS, dtype=jnp.float32) * w[..., None]).sum(1).astype(x.dtype)
    h = jax.nn.silu(jnp.einsum('nd,edf->nef', xt, w_gate)) * jnp.einsum('nd,edf->nef', xt, w_up)
    y = jnp.einsum('nef,efd->nd', h * gate[:, :, None], w_down)
    return y.reshape(shp)


def post_layer(x, mixed, i, w_out, ln1_g, ln1_b, ln2_g, ln2_b, w_router, router_bias, w_gate, w_up, w_down):
    x = layer_norm(DEEPNORM_ALPHA * x + mixed @ w_out[i], ln1_g[i], ln1_b[i])
    f = moe(x, w_router, router_bias, w_gate[i], w_up[i], w_down[i])
    return layer_norm(DEEPNORM_ALPHA * x + f, ln2_g[i], ln2_b[i])


def setup_inputs(seed: int = 0) -> dict:
    key = jax.random.key(seed)
    ks = iter(jax.random.split(key, 48))

    def nrm(shape, scale=1.0):
        return jax.random.normal(next(ks), shape, jnp.float32) * scale

    n_pages = PAST_LEN // PAGE_SIZE
    n_used = DEC_BATCH * n_pages
    n_pool = n_used + n_used // 4
    perm = jax.random.permutation(next(ks), n_pool)
    page_table = perm[:n_used].reshape(DEC_BATCH, n_pages).astype(jnp.int32)
    return {
        'x_prompt': nrm((BATCH, SEQ, D_MODEL)),
        'x_sample': nrm((DEC_BATCH, DEC_SEQ, D_MODEL)),
        'cache_mla_ckv': nrm((N_MLA, n_pool, PAGE_SIZE, KV_LORA)),
        'cache_mla_krope': nrm((N_MLA, n_pool, PAGE_SIZE, QK_ROPE)),
        'cache_dsa_k': nrm((N_DSA, n_pool, PAGE_SIZE, KV_HEADS, HEAD_DIM)),
        'cache_dsa_v': nrm((N_DSA, n_pool, PAGE_SIZE, KV_HEADS, HEAD_DIM)),
        'cache_dsa_kidx': nrm((N_DSA, n_pool, PAGE_SIZE, IDX_DIM)),
        'cache_mem_k': nrm((DEPTH, DEC_BATCH, N_MEM, MEM_HEADS, HEAD_DIM)),
        'cache_mem_v': nrm((DEPTH, DEC_BATCH, N_MEM, MEM_HEADS, HEAD_DIM)),
        'page_table': page_table,
        'mem_prompt': nrm((BATCH, N_MEM, D_MODEL)),
        'w_in_mla': nrm((N_MLA, D_MODEL, MLA_IN), D_MODEL ** -0.5),
        'mla_q_norm': 1.0 + nrm((N_MLA, Q_LORA), 0.02),
        'mla_kv_norm': 1.0 + nrm((N_MLA, KV_LORA), 0.02),
        'w_uq': nrm((N_MLA, Q_LORA, N_HEADS, QK_NOPE + QK_ROPE), Q_LORA ** -0.5),
        'w_uk': nrm((N_MLA, KV_LORA, N_HEADS, QK_NOPE), KV_LORA ** -0.5),
        'w_uv': nrm((N_MLA, KV_LORA, N_HEADS, V_DIM), KV_LORA ** -0.5),
        'w_in_dsa': nrm((N_DSA, D_MODEL, DSA_IN), D_MODEL ** -0.5),
        'idx_k_norm_g': 1.0 + nrm((N_DSA, IDX_DIM), 0.02),
        'idx_k_norm_b': nrm((N_DSA, IDX_DIM), 0.02),
        'w_mem_kv': nrm((DEPTH, D_MODEL, 2 * MEM_HEADS * HEAD_DIM), D_MODEL ** -0.5),
        'w_out': nrm((DEPTH, D_MIX, D_MODEL), DEEPNORM_BETA * D_MIX ** -0.5),
        'ln1_g': 1.0 + nrm((DEPTH, D_MODEL), 0.02),
        'ln1_b': nrm((DEPTH, D_MODEL), 0.02),
        'ln2_g': 1.0 + nrm((DEPTH, D_MODEL), 0.02),
        'ln2_b': nrm((DEPTH, D_MODEL), 0.02),
        'w_router': nrm((D_MODEL, N_EXPERTS), D_MODEL ** -0.5),
        'router_bias': nrm((N_EXPERTS,), 0.01),
        'w_gate': nrm((DEPTH, N_EXPERTS, D_MODEL, D_EXPERT), D_MODEL ** -0.5),
        'w_up': nrm((DEPTH, N_EXPERTS, D_MODEL, D_EXPERT), D_MODEL ** -0.5),
        'w_down': nrm((DEPTH, N_EXPERTS, D_EXPERT, D_MODEL), DEEPNORM_BETA * D_EXPERT ** -0.5),
    }


def reference(x_prompt, x_sample, cache_mla_ckv, cache_mla_krope, cache_dsa_k, cache_dsa_v, cache_dsa_kidx,
              cache_mem_k, cache_mem_v, page_table, mem_prompt, w_in_mla, mla_q_norm, mla_kv_norm, w_uq, w_uk, w_uv,
              w_in_dsa, idx_k_norm_g, idx_k_norm_b, w_mem_kv, w_out, ln1_g, ln1_b, ln2_g, ln2_b,
              w_router, router_bias, w_gate, w_up, w_down):
    B, S, _ = x_prompt.shape
    T = x_sample.shape[1]
    n_mem = mem_prompt.shape[1]
    past = page_table.shape[1] * PAGE_SIZE
    pos_p = jnp.arange(S, dtype=jnp.int32)
    pos_s = past + jnp.arange(T, dtype=jnp.int32)
    xp, xs = x_prompt, x_sample
    ckv_p, kr_p, ckv_s, kr_s = [], [], [], []
    k_p, v_p, ki_p, k_s, v_s, ki_s = [], [], [], [], [], []
    mk_list, mv_list = [], []
    moe_w = (w_router, router_bias, w_gate, w_up, w_down)
    for i in range(DEPTH):
        j = i // N_MIXERS
        mkv = (mem_prompt @ w_mem_kv[i]).reshape(B, n_mem, 2, MEM_HEADS, HEAD_DIM)
        mk_p, mv_p = mkv[:, :, 0], mkv[:, :, 1]
        mk_list.append(mk_p)
        mv_list.append(mv_p)
        if i % N_MIXERS == 0:
            qn, qr, ckv, kr, qm_p = mla_project(xp, pos_p, w_in_mla[j], mla_q_norm[j], mla_kv_norm[j], w_uq[j])
            mix_p = mla_prompt(qn, qr, ckv, kr, w_uk[j], w_uv[j])
            ckv_p.append(ckv)
            kr_p.append(kr)
            qn, qr, ckv, kr, qm_s = mla_project(xs, pos_s, w_in_mla[j], mla_q_norm[j], mla_kv_norm[j], w_uq[j])
            mix_s = mla_sample(qn, qr, ckv, kr, cache_mla_ckv, cache_mla_krope, j, page_table, w_uk[j], w_uv[j])
            ckv_s.append(ckv)
            kr_s.append(kr)
        else:
            q, k, v, qi, wi, ki, qm_p = dsa_project(xp, pos_p, w_in_dsa[j], idx_k_norm_g[j], idx_k_norm_b[j])
            mix_p = dsa_prompt(q, k, v, qi, wi, ki)
            k_p.append(k)
            v_p.append(v)
            ki_p.append(ki)
            q, k, v, qi, wi, ki, qm_s = dsa_project(xs, pos_s, w_in_dsa[j], idx_k_norm_g[j], idx_k_norm_b[j])
            mix_s = dsa_sample(q, k, v, qi, wi, ki, cache_dsa_k, cache_dsa_v, cache_dsa_kidx, j, page_table)
            k_s.append(k)
            v_s.append(v)
            ki_s.append(ki)
        mixed_p = jnp.concatenate([mix_p, mem_attend(qm_p, mk_p, mv_p)], axis=-1)
        mixed_s = jnp.concatenate([mix_s, mem_attend(qm_s, cache_mem_k[i], cache_mem_v[i])], axis=-1)
        xp = post_layer(xp, mixed_p, i, w_out, ln1_g, ln1_b, ln2_g, ln2_b, *moe_w)
        xs = post_layer(xs, mixed_s, i, w_out, ln1_g, ln1_b, ln2_g, ln2_b, *moe_w)
    return (xp, xs,
            jnp.stack(ckv_p), jnp.stack(kr_p), jnp.stack(k_p), jnp.stack(v_p), jnp.stack(ki_p),
            jnp.stack(mk_list), jnp.stack(mv_list),
            jnp.stack(ckv_s), jnp.stack(kr_s), jnp.stack(k_s), jnp.stack(v_s), jnp.stack(ki_s))
```

```python
import functools

import jax
import jax.numpy as jnp
from jax import lax
from jax.experimental import pallas as pl
from jax.experimental.pallas import tpu as pltpu

F32 = jnp.float32
BF16 = jnp.bfloat16
I32 = jnp.int32

D_MODEL = 1024
DEPTH = 4
PAGE = 128
HEAD_DIM = 64
N_HEADS = 12
MEM_HEADS = 4
Q_LORA = 384
KV_LORA = 256
QK_NOPE = 64
QK_ROPE = 32
MLA_SCALE = (QK_NOPE + QK_ROPE) ** -0.5
HEAD_SCALE = HEAD_DIM ** -0.5
KV_HEADS = 4
IDX_HEADS = 8
IDX_DIM = 64
TOPK_MAX = 256
ROT_DIM = HEAD_DIM // 4
ROPE_THETA = 500000.0
N_EXPERTS = 16
N_GROUPS = 4
D_EXPERT = 256
ALPHA = (2 * DEPTH) ** 0.25
LN_EPS = 1e-5
RMS_EPS = 1e-6
NEG = -1e30
LANE = 128
MIN_I32 = -2 ** 31
DSA_PERM = (0, 3, 1, 4, 2, 5, 6, 9, 7, 10, 8, 11)

NT = (((1,), (1,)), ((), ()))


def _cp(sem, vmem_mb=48):
    return pltpu.CompilerParams(dimension_semantics=sem, vmem_limit_bytes=vmem_mb * 1024 * 1024)


def _dot(a, b):
    return jnp.dot(a, b, preferred_element_type=F32)


def _dot_nt(a, b):
    return lax.dot_general(a, b, NT, preferred_element_type=F32)


def _rms(x, g):
    return x * lax.rsqrt(jnp.mean(x * x, axis=-1, keepdims=True) + RMS_EPS) * g


def _ln(x, g, b):
    mu = jnp.mean(x, axis=-1, keepdims=True)
    xc = x - mu
    var = jnp.mean(xc * xc, axis=-1, keepdims=True)
    return xc * lax.rsqrt(var + LN_EPS) * g + b


def _softmax_step(s, m_prev, l_prev):
    m_new = jnp.maximum(m_prev, jnp.max(s, axis=1, keepdims=True))
    a = jnp.exp(m_prev - m_new)
    p = jnp.exp(s - m_new)
    return m_new, a, p, a * l_prev + jnp.sum(p, axis=1, keepdims=True)


def _rope_tables(pos, rot_dim, period, offset, width):
    half = rot_dim // 2
    inv = ROPE_THETA ** (-jnp.arange(half, dtype=jnp.float32) / half)
    ang = pos.astype(jnp.float32)[:, None] * inv
    c, s = jnp.cos(ang), jnp.sin(ang)
    t = pos.shape[0]
    cg = jnp.ones((t, period), F32).at[:, offset:offset + rot_dim].set(jnp.concatenate([c, c], 1))
    sg = jnp.zeros((t, period), F32).at[:, offset:offset + rot_dim].set(jnp.concatenate([s, s], 1))
    return jnp.tile(cg, (1, width // period)), jnp.tile(sg, (1, width // period))


def _rot_cols(w, half):
    return jnp.concatenate([-w[..., half:2 * half], w[..., :half]], axis=-1)


def _rot_heads(w, heads, dim, rot):
    w3 = w.reshape(w.shape[0], heads, dim)
    r = jnp.concatenate([_rot_cols(w3[..., :rot], rot // 2), jnp.zeros_like(w3[..., rot:])], axis=-1)
    return r.reshape(w.shape[0], heads * dim)


def _mla_front(x_ref, w1_ref, w2_ref, qn_ref, kvn_ref, cos_ref, sin_ref):
    xb = x_ref[...].astype(BF16)
    y = _dot(xb, w1_ref[...])
    cqn = _rms(y[:, :Q_LORA], qn_ref[...]).astype(BF16)
    ckv = _rms(y[:, Q_LORA:Q_LORA + KV_LORA], kvn_ref[...])
    qmem = y[:, 640:896]
    cos = cos_ref[...]
    sin = sin_ref[...]
    krp = y[:, 896:1024] * cos + y[:, 1024:1152] * sin
    q2 = _dot(cqn, w2_ref[...])
    return q2, ckv, qmem, krp, cos, sin


def _mla_proj_prompt_kernel(x_ref, w1_ref, w2_ref, wk_ref, wv_ref, qn_ref, kvn_ref, cos_ref, sin_ref,
                            q_ref, k_ref, v_ref, ckv_ref, kr_ref, qm_ref):
    q2, ckv, qmem, krp, cos, sin = _mla_front(x_ref, w1_ref, w2_ref, qn_ref, kvn_ref, cos_ref, sin_ref)
    hw = N_HEADS * LANE
    for h in range(N_HEADS):
        qh = q2[:, h * LANE:(h + 1) * LANE] * cos + q2[:, hw + h * LANE:hw + (h + 1) * LANE] * sin
        q_ref[h] = (qh * MLA_SCALE).astype(BF16)
    ckv_ref[...] = ckv
    kr_ref[...] = krp
    qm_ref[...] = (qmem * HEAD_SCALE).astype(BF16)
    cb = ckv.astype(BF16)
    kn = _dot(cb, wk_ref[...])
    for h in range(N_HEADS):
        k_ref[h] = (kn[:, h * LANE:(h + 1) * LANE] + krp).astype(BF16)
    v_ref[...] = _dot(cb, wv_ref[...]).astype(BF16)


def _mla_proj_sample_kernel(x_ref, w1_ref, w2_ref, wukt_ref, qn_ref, kvn_ref, cos_ref, sin_ref,
                            cosr_ref, sinr_ref, qlat_ref, qr_ref, ckv_ref, kr_ref, qm_ref):
    q2, ckv, qmem, krp, _, _ = _mla_front(x_ref, w1_ref, w2_ref, qn_ref, kvn_ref, cos_ref, sin_ref)
    hw = N_HEADS * LANE
    rw = N_HEADS * QK_ROPE
    for h in range(N_HEADS):
        qh = (q2[:, h * LANE:(h + 1) * LANE] * MLA_SCALE).astype(BF16)
        qlat_ref[:, h * KV_LORA:(h + 1) * KV_LORA] = _dot(qh, wukt_ref[h]).astype(BF16)
    qr = q2[:, hw:hw + rw] * cosr_ref[...] + q2[:, hw + rw:hw + 2 * rw] * sinr_ref[...]
    qr_ref[...] = (qr * MLA_SCALE).astype(BF16)
    ckv_ref[...] = ckv
    kr_ref[...] = krp
    qm_ref[...] = (qmem * HEAD_SCALE).astype(BF16)


def _mla_w1(w_in):
    w_cq, w_ckv = w_in[:, :Q_LORA], w_in[:, Q_LORA:Q_LORA + KV_LORA]
    w_kr = w_in[:, Q_LORA + KV_LORA:Q_LORA + KV_LORA + QK_ROPE]
    w_qm = w_in[:, Q_LORA + KV_LORA + QK_ROPE:]
    z = lambda n: jnp.zeros((D_MODEL, n), F32)
    kr_pad = jnp.concatenate([z(QK_NOPE), w_kr, z(LANE - QK_NOPE - QK_ROPE)], 1)
    kr_rot = jnp.concatenate([z(QK_NOPE), _rot_cols(w_kr, QK_ROPE // 2), z(LANE - QK_NOPE - QK_ROPE)], 1)
    return jnp.concatenate([w_cq, w_ckv, w_qm, kr_pad, kr_rot], 1).astype(BF16)


def _mla_proj_prompt(x, w_in, q_norm, kv_norm, w_uq, w_uk, w_uv, cos, sin, tm):
    n = x.shape[0]
    w1 = _mla_w1(w_in)
    zq = jnp.zeros((Q_LORA, N_HEADS, LANE - QK_NOPE - QK_ROPE), F32)
    uq_pad = jnp.concatenate([w_uq, zq], -1).reshape(Q_LORA, N_HEADS * LANE)
    uq_rot = jnp.concatenate([jnp.zeros((Q_LORA, N_HEADS, QK_NOPE), F32),
                              _rot_cols(w_uq[..., QK_NOPE:], QK_ROPE // 2), zq], -1).reshape(Q_LORA, N_HEADS * LANE)
    w2 = jnp.concatenate([uq_pad, uq_rot], 1).astype(BF16)
    wk = jnp.concatenate([w_uk, jnp.zeros((KV_LORA, N_HEADS, LANE - QK_NOPE), F32)], -1)
    wk = wk.reshape(KV_LORA, N_HEADS * LANE).astype(BF16)
    wv = w_uv.reshape(KV_LORA, N_HEADS * HEAD_DIM).astype(BF16)
    nt = cos.shape[0] // tm
    full = lambda a: pl.BlockSpec(a.shape, lambda i: (0,) * a.ndim)
    rows = lambda w: pl.BlockSpec((tm, w), lambda i: (i, 0))
    heads = pl.BlockSpec((N_HEADS, tm, LANE), lambda i: (0, i, 0))
    tab = pl.BlockSpec((tm, LANE), lambda i: (i % nt, 0))
    qn, kvn = q_norm.reshape(1, -1), kv_norm.reshape(1, -1)
    return pl.pallas_call(
        _mla_proj_prompt_kernel,
        grid=(n // tm,),
        in_specs=[rows(D_MODEL), full(w1), full(w2), full(wk), full(wv), full(qn), full(kvn), tab, tab],
        out_specs=[heads, heads, rows(N_HEADS * HEAD_DIM), rows(KV_LORA), rows(LANE), rows(MEM_HEADS * HEAD_DIM)],
        out_shape=[jax.ShapeDtypeStruct((N_HEADS, n, LANE), BF16), jax.ShapeDtypeStruct((N_HEADS, n, LANE), BF16),
                   jax.ShapeDtypeStruct((n, N_HEADS * HEAD_DIM), BF16), jax.ShapeDtypeStruct((n, KV_LORA), F32),
                   jax.ShapeDtypeStruct((n, LANE), F32), jax.ShapeDtypeStruct((n, MEM_HEADS * HEAD_DIM), BF16)],
        compiler_params=_cp(("parallel",)),
        name="mla_proj_prompt",
    )(x, w1, w2, wk, wv, qn, kvn, cos, sin)


def _mla_proj_sample(x, w_in, q_norm, kv_norm, w_uq, w_uk, cos, sin, cosr, sinr, tm):
    n = x.shape[0]
    w1 = _mla_w1(w_in)
    zq = jnp.zeros((Q_LORA, N_HEADS, LANE - QK_NOPE - QK_ROPE), F32)
    uq_pad = jnp.concatenate([w_uq, zq], -1).reshape(Q_LORA, N_HEADS * LANE)
    w_qr = w_uq[..., QK_NOPE:]
    w2 = jnp.concatenate([uq_pad, w_qr.reshape(Q_LORA, -1),
                          _rot_cols(w_qr, QK_ROPE // 2).reshape(Q_LORA, -1)], 1).astype(BF16)
    wukt = jnp.transpose(w_uk, (1, 2, 0))
    wukt = jnp.concatenate([wukt, jnp.zeros((N_HEADS, LANE - QK_NOPE, KV_LORA), F32)], 1).astype(BF16)
    full = lambda a: pl.BlockSpec(a.shape, lambda i: (0,) * a.ndim)
    rows = lambda w: pl.BlockSpec((tm, w), lambda i: (i, 0))
    qn, kvn = q_norm.reshape(1, -1), kv_norm.reshape(1, -1)
    return pl.pallas_call(
        _mla_proj_sample_kernel,
        grid=(n // tm,),
        in_specs=[rows(D_MODEL), full(w1), full(w2), full(wukt), full(qn), full(kvn), rows(LANE), rows(LANE),
                  rows(N_HEADS * QK_ROPE), rows(N_HEADS * QK_ROPE)],
        out_specs=[rows(N_HEADS * KV_LORA), rows(N_HEADS * QK_ROPE), rows(KV_LORA), rows(LANE),
                   rows(MEM_HEADS * HEAD_DIM)],
        out_shape=[jax.ShapeDtypeStruct((n, N_HEADS * KV_LORA), BF16),
                   jax.ShapeDtypeStruct((n, N_HEADS * QK_ROPE), BF16), jax.ShapeDtypeStruct((n, KV_LORA), F32),
                   jax.ShapeDtypeStruct((n, LANE), F32), jax.ShapeDtypeStruct((n, MEM_HEADS * HEAD_DIM), BF16)],
        compiler_params=_cp(("parallel",)),
        name="mla_proj_sample",
    )(x, w1, w2, wukt, qn, kvn, cos, sin, cosr, sinr)


def _mla_flash_kernel(q_ref, k_ref, v_ref, o_ref, m_ref, l_ref, acc_ref, *, tq, tk):
    qi = pl.program_id(2)
    ki = pl.program_id(3)
    last = ((qi + 1) * tq - 1) // tk

    @pl.when(ki == 0)
    def _():
        m_ref[...] = jnp.full(m_ref.shape, NEG, F32)
        l_ref[...] = jnp.zeros(l_ref.shape, F32)
        acc_ref[...] = jnp.zeros(acc_ref.shape, F32)

    @pl.when(ki <= last)
    def _():
        row = qi * tq + lax.broadcasted_iota(I32, (tq, tk), 0)
        col = ki * tk + lax.broadcasted_iota(I32, (tq, tk), 1)
        causal = col <= row
        vb = v_ref[...]
        for hh in range(2):
            s = jnp.where(causal, _dot_nt(q_ref[hh], k_ref[hh]), NEG)
            m_new, a, p, l_new = _softmax_step(s, m_ref[hh], l_ref[hh])
            acc_ref[hh] = a * acc_ref[hh] + _dot(p.astype(BF16), vb)
            m_ref[hh] = m_new
            l_ref[hh] = l_new

    @pl.when(ki == last)
    def _():
        lane = lax.broadcasted_iota(I32, (tq, LANE), 1)
        o_ref[...] = jnp.where(lane < HEAD_DIM, acc_ref[0] / l_ref[0], acc_ref[1] / l_ref[1]).astype(o_ref.dtype)


def _mla_flash(q, k, v, batch, tq, tk):
    n = q.shape[1]
    s = n // batch
    nq, nk = s // tq, s // tk
    kmap = lambda b, hp, qi, ki: (hp, b * nk + jnp.minimum(ki, ((qi + 1) * tq - 1) // tk), 0)
    vmap_ = lambda b, hp, qi, ki: (b * nk + jnp.minimum(ki, ((qi + 1) * tq - 1) // tk), hp)
    return pl.pallas_call(
        functools.partial(_mla_flash_kernel, tq=tq, tk=tk),
        grid=(batch, N_HEADS // 2, nq, nk),
        in_specs=[pl.BlockSpec((2, tq, LANE), lambda b, hp, qi, ki: (hp, b * nq + qi, 0)),
                  pl.BlockSpec((2, tk, LANE), kmap),
                  pl.BlockSpec((tk, LANE), vmap_)],
        out_specs=pl.BlockSpec((tq, LANE), lambda b, hp, qi, ki: (b * nq + qi, hp)),
        out_shape=jax.ShapeDtypeStruct((n, N_HEADS * HEAD_DIM), BF16),
        scratch_shapes=[pltpu.VMEM((2, tq, 1), F32), pltpu.VMEM((2, tq, 1), F32), pltpu.VMEM((2, tq, LANE), F32)],
        compiler_params=_cp(("parallel", "parallel", "parallel", "arbitrary")),
        name="mla_flash",
    )(q, k, v)


def _mla_decode_kernel(pt_ref, qlat_ref, qr_ref, cn_ref, kn_ref, *rest, pp, t_new):
    ck_refs, kr_refs = rest[:pp], rest[pp:2 * pp]
    o_ref, m_ref, l_ref, acc_ref = rest[2 * pp:]
    p_id = pl.program_id(1)
    rows = qlat_ref.shape[1]

    @pl.when(p_id == 0)
    def _():
        m_ref[...] = jnp.full(m_ref.shape, NEG, F32)
        l_ref[...] = jnp.zeros(l_ref.shape, F32)
        acc_ref[...] = jnp.zeros(acc_ref.shape, F32)

    qlat = qlat_ref[0]
    qr = qr_ref[0]
    cks = [ck_refs[i][...].astype(BF16) for i in range(pp)]
    s = jnp.concatenate([_dot_nt(qlat, cks[i]) + _dot_nt(qr, kr_refs[i][...].astype(BF16)) for i in range(pp)], 1)
    m_new, a, p, l_new = _softmax_step(s, m_ref[...], l_ref[...])
    pb = p.astype(BF16)
    pv = _dot(pb[:, :PAGE], cks[0])
    for i in range(1, pp):
        pv = pv + _dot(pb[:, i * PAGE:(i + 1) * PAGE], cks[i])
    acc_ref[...] = a * acc_ref[...] + pv
    m_ref[...] = m_new
    l_ref[...] = l_new

    @pl.when(p_id == pl.num_programs(1) - 1)
    def _():
        cn = cn_ref[0].astype(BF16)
        kn = kn_ref[0].astype(BF16)
        npad = cn.shape[0]
        t = lax.broadcasted_iota(I32, (rows, npad), 0) % t_new
        u = lax.broadcasted_iota(I32, (rows, npad), 1)
        s2 = jnp.where(u <= t, _dot_nt(qlat, cn) + _dot_nt(qr, kn), NEG)
        m2, a2, p2, l2 = _softmax_step(s2, m_ref[...], l_ref[...])
        acc = a2 * acc_ref[...] + _dot(p2.astype(BF16), cn)
        o_ref[0] = acc / l2


def _mla_decode(page_table, qlat, qr, ckv_new, kr_new, pool_ckv, pool_kr, j, pp, t_new):
    db, rows, _ = qlat.shape
    npg = page_table.shape[1]
    pt = page_table.reshape(-1)
    npad = ckv_new.shape[1]

    def pool_spec(width, i):
        return pl.BlockSpec((None, None, PAGE, width), lambda b, p, pt_: (j, pt_[b * npg + p * pp + i], 0, 0))

    per_b = lambda r, w: pl.BlockSpec((1, r, w), lambda b, p, pt_: (b, 0, 0))
    grid_spec = pltpu.PrefetchScalarGridSpec(
        num_scalar_prefetch=1,
        grid=(db, npg // pp),
        in_specs=[per_b(rows, KV_LORA), per_b(rows, QK_ROPE), per_b(npad, KV_LORA), per_b(npad, QK_ROPE)]
        + [pool_spec(KV_LORA, i) for i in range(pp)] + [pool_spec(QK_ROPE, i) for i in range(pp)],
        out_specs=per_b(rows, KV_LORA),
        scratch_shapes=[pltpu.VMEM((rows, 1), F32), pltpu.VMEM((rows, 1), F32), pltpu.VMEM((rows, KV_LORA), F32)],
    )
    return pl.pallas_call(
        functools.partial(_mla_decode_kernel, pp=pp, t_new=t_new),
        grid_spec=grid_spec,
        out_shape=jax.ShapeDtypeStruct((db, rows, KV_LORA), F32),
        compiler_params=_cp(("parallel", "arbitrary")),
        name="mla_decode",
    )(pt, qlat, qr, ckv_new, kr_new, *([pool_ckv] * pp), *([pool_kr] * pp))


def _head_matmul_kernel(x_ref, w_ref, o_ref):
    o_ref[...] = _dot(x_ref[...].astype(BF16), w_ref[...]).astype(o_ref.dtype)


def _head_matmul(x, w):
    h, n, k = x.shape
    m = w.shape[2]
    return pl.pallas_call(
        _head_matmul_kernel,
        grid=(h,),
        in_specs=[pl.BlockSpec((None, n, k), lambda i: (i, 0, 0)), pl.BlockSpec((None, k, m), lambda i: (i, 0, 0))],
        out_specs=pl.BlockSpec((None, n, m), lambda i: (i, 0, 0)),
        out_shape=jax.ShapeDtypeStruct((h, n, m), BF16),
        compiler_params=_cp(("parallel",)),
        name="head_matmul",
    )(x, w)


def _dsa_proj_kernel(x_ref, wq_ref, wk_ref, wv_ref, wqi_ref, wm_ref, wki_ref, lng_ref, lnb_ref, sgn_ref,
                     gperm_ref, brot_ref, cos_ref, sin_ref,
                     q_ref, k_ref, kf_ref, vf_ref, vb_ref, qi_ref, kw_ref, kib_ref, qm_ref):
    tm = x_ref.shape[0]
    xb = x_ref[...].astype(BF16)
    cos = cos_ref[...]
    sin = sin_ref[...]
    lane = lax.broadcasted_iota(I32, (tm, LANE), 1)
    low = lane < HEAD_DIM

    def roped(w_ref, nblk):
        y = _dot(xb, w_ref[...])
        return [y[:, b * LANE:(b + 1) * LANE] * cos + y[:, (nblk + b) * LANE:(nblk + b + 1) * LANE] * sin
                for b in range(nblk)]

    qb = roped(wq_ref, N_HEADS // 2)
    for h in range(N_HEADS):
        g, r = h // 3, h % 3
        blk = qb[(g // 2) * 3 + r] * HEAD_SCALE
        q_ref[h] = jnp.where(low if g % 2 == 0 else jnp.logical_not(low), blk, 0.0).astype(BF16)
    kb = roped(wk_ref, KV_HEADS // 2)
    for b in range(KV_HEADS // 2):
        kf_ref[:, b * LANE:(b + 1) * LANE] = kb[b]
        k_ref[b] = kb[b].astype(BF16)
    v = _dot(xb, wv_ref[...])
    vf_ref[...] = v
    vb_ref[...] = v.astype(BF16)
    qib = roped(wqi_ref, IDX_HEADS // 2)
    for h in range(IDX_HEADS):
        qi_ref[h] = jnp.where(low if h % 2 == 0 else jnp.logical_not(low), qib[h // 2], 0.0).astype(BF16)
    qm_ref[...] = (_dot(xb, wm_ref[...]) * HEAD_SCALE).astype(BF16)
    yk = _dot(xb, wki_ref[...])
    c, c_rot, wpad = yk[:, :LANE], yk[:, LANE:2 * LANE], yk[:, 2 * LANE:3 * LANE]
    mu = jnp.sum(jnp.where(low, c, 0.0), axis=1, keepdims=True) * (1.0 / IDX_DIM)
    xc = c - mu
    var = jnp.sum(jnp.where(low, xc * xc, 0.0), axis=1, keepdims=True) * (1.0 / IDX_DIM)
    rstd = lax.rsqrt(var + LN_EPS)
    ki = xc * rstd * lng_ref[...] + lnb_ref[...]
    ki_rot = (c_rot - sgn_ref[...] * mu) * rstd * gperm_ref[...] + brot_ref[...]
    kir = ki * cos + ki_rot * sin
    kib_ref[...] = kir.astype(BF16)
    kw_ref[...] = jnp.where(low, kir, wpad)


def _dsa_proj(x, w_in, ln_g, ln_b, cos, sin, tm):
    n = x.shape[0]
    o = 0
    cols = []
    for wdt in (N_HEADS * HEAD_DIM, KV_HEADS * HEAD_DIM, KV_HEADS * HEAD_DIM, IDX_HEADS * IDX_DIM, IDX_DIM, IDX_HEADS,
                MEM_HEADS * HEAD_DIM):
        cols.append(w_in[:, o:o + wdt])
        o += wdt
    w_q, w_k, w_v, w_qi, w_ki, w_w, w_qm = cols
    perm = jnp.asarray(DSA_PERM)
    w_qp = w_q.reshape(D_MODEL, N_HEADS, HEAD_DIM)[:, perm].reshape(D_MODEL, -1)
    wq = jnp.concatenate([w_qp, _rot_heads(w_qp, N_HEADS, HEAD_DIM, ROT_DIM)], 1).astype(BF16)
    wk = jnp.concatenate([w_k, _rot_heads(w_k, KV_HEADS, HEAD_DIM, ROT_DIM)], 1).astype(BF16)
    wqi = jnp.concatenate([w_qi, _rot_heads(w_qi, IDX_HEADS, IDX_DIM, ROT_DIM)], 1).astype(BF16)
    w_ki_rot = _rot_heads(w_ki, 1, IDX_DIM, ROT_DIM)
    w_wpad = jnp.concatenate([jnp.zeros((D_MODEL, HEAD_DIM), F32), w_w,
                              jnp.zeros((D_MODEL, LANE - HEAD_DIM - IDX_HEADS), F32)], 1)
    wki = jnp.concatenate([w_ki, w_ki, w_ki_rot, w_ki_rot, w_wpad], 1).astype(BF16)
    wv, wm = w_v.astype(BF16), w_qm.astype(BF16)
    half = ROT_DIM // 2
    zr = jnp.zeros((IDX_DIM - ROT_DIM,), F32)
    dup = lambda a: jnp.concatenate([a, a]).reshape(1, LANE)
    sgn = dup(jnp.concatenate([-jnp.ones((half,), F32), jnp.ones((half,), F32), zr]))
    gperm = dup(jnp.concatenate([ln_g[half:ROT_DIM], ln_g[:half], zr]))
    brot = dup(jnp.concatenate([-ln_b[half:ROT_DIM], ln_b[:half], zr]))
    lng, lnb = dup(ln_g), dup(ln_b)
    nt = cos.shape[0] // tm
    full = lambda a: pl.BlockSpec(a.shape, lambda i: (0,) * a.ndim)
    rows = lambda w: pl.BlockSpec((tm, w), lambda i: (i, 0))
    heads = lambda h: pl.BlockSpec((h, tm, LANE), lambda i: (0, i, 0))
    tab = pl.BlockSpec((tm, LANE), lambda i: (i % nt, 0))
    sds = jax.ShapeDtypeStruct
    return pl.pallas_call(
        _dsa_proj_kernel,
        grid=(n // tm,),
        in_specs=[rows(D_MODEL), full(wq), full(wk), full(wv), full(wqi), full(wm), full(wki), full(lng), full(lnb),
                  full(sgn), full(gperm), full(brot), tab, tab],
        out_specs=[heads(N_HEADS), heads(KV_HEADS // 2), rows(2 * LANE), rows(2 * LANE), rows(2 * LANE),
                   heads(IDX_HEADS), rows(LANE), rows(LANE), rows(2 * LANE)],
        out_shape=[sds((N_HEADS, n, LANE), BF16), sds((KV_HEADS // 2, n, LANE), BF16), sds((n, 2 * LANE), F32),
                   sds((n, 2 * LANE), F32), sds((n, 2 * LANE), BF16), sds((IDX_HEADS, n, LANE), BF16),
                   sds((n, LANE), F32), sds((n, LANE), BF16), sds((n, 2 * LANE), BF16)],
        compiler_params=_cp(("parallel",), 56),
        name="dsa_proj",
    )(x, wq, wk, wv, wqi, wm, wki, lng, lnb, sgn, gperm, brot, cos, sin)


def _indexer_prompt_kernel(qi_ref, ki_ref, kw_ref, sc_ref, *, tq, tk):
    qi = pl.program_id(1)
    ki = pl.program_id(2)
    last = ((qi + 1) * tq - 1) // tk

    @pl.when(ki <= last)
    def _():
        kb = ki_ref[...]
        w = kw_ref[...]
        acc = None
        for h in range(IDX_HEADS):
            d = jnp.maximum(_dot_nt(qi_ref[h], kb), 0.0) * w[:, HEAD_DIM + h:HEAD_DIM + h + 1]
            acc = d if acc is None else acc + d
        row = qi * tq + lax.broadcasted_iota(I32, (tq, tk), 0)
        col = ki * tk + lax.broadcasted_iota(I32, (tq, tk), 1)
        sc_ref[...] = jnp.where(col <= row, acc, NEG)

    @pl.when(ki > last)
    def _():
        sc_ref[...] = jnp.full(sc_ref.shape, NEG, F32)


def _indexer_prompt(qidx, kib, kw, batch, tq, tk):
    n = kib.shape[0]
    s = n // batch
    nq, nk = s // tq, s // tk
    kclamp = lambda qi, ki: jnp.minimum(ki, ((qi + 1) * tq - 1) // tk)
    return pl.pallas_call(
        functools.partial(_indexer_prompt_kernel, tq=tq, tk=tk),
        grid=(batch, nq, nk),
        in_specs=[pl.BlockSpec((IDX_HEADS, tq, LANE), lambda b, qi, ki: (0, b * nq + qi, 0)),
                  pl.BlockSpec((tk, LANE), lambda b, qi, ki: (b * nk + kclamp(qi, ki), 0)),
                  pl.BlockSpec((tq, LANE), lambda b, qi, ki: (b * nq + qi, 0))],
        out_specs=pl.BlockSpec((tq, tk), lambda b, qi, ki: (b * nq + qi, ki)),
        out_shape=jax.ShapeDtypeStruct((n, s), F32),
        compiler_params=_cp(("parallel", "parallel", "arbitrary")),
        name="indexer_prompt",
    )(qidx, kib, kw)


def _indexer_sample_kernel(pt_ref, q_ref, w_ref, kn_ref, *rest, pp, t_new):
    k_refs = rest[:pp]
    sc_ref, scn_ref = rest[pp:]
    q = q_ref[0]
    w = w_ref[0]

    def scores(kb):
        d = jnp.maximum(_dot_nt(q, kb), 0.0)
        acc = None
        for h in range(IDX_HEADS):
            t = d[h * t_new:(h + 1) * t_new] * w[:, h:h + 1]
            acc = t if acc is None else acc + t
        return acc

    for i in range(pp):
        sc_ref[0, :, i * PAGE:(i + 1) * PAGE] = scores(k_refs[i][...].astype(BF16))

    @pl.when(pl.program_id(1) == pl.num_programs(1) - 1)
    def _():
        sn = scores(kn_ref[0].astype(BF16))
        npad = sn.shape[1]
        t = lax.broadcasted_iota(I32, (t_new, npad), 0)
        u = lax.broadcasted_iota(I32, (t_new, npad), 1)
        scn_ref[0] = jnp.full((t_new, LANE), NEG, F32)
        scn_ref[0, :, :npad] = jnp.where(u <= t, sn, NEG)


def _indexer_sample(page_table, q, w, kidx_new, pool_kidx, j, pp, t_new):
    db = q.shape[0]
    npg = page_table.shape[1]
    pt = page_table.reshape(-1)
    npad = kidx_new.shape[1]
    per_b = lambda r, c: pl.BlockSpec((1, r, c), lambda b, p, pt_: (b, 0, 0))
    pool = lambda i: pl.BlockSpec((None, None, PAGE, IDX_DIM), lambda b, p, pt_: (j, pt_[b * npg + p * pp + i], 0, 0))
    grid_spec = pltpu.PrefetchScalarGridSpec(
        num_scalar_prefetch=1,
        grid=(db, npg // pp),
        in_specs=[per_b(IDX_HEADS * t_new, IDX_DIM), per_b(t_new, IDX_HEADS), per_b(npad, IDX_DIM)]
        + [pool(i) for i in range(pp)],
        out_specs=[pl.BlockSpec((1, t_new, pp * PAGE), lambda b, p, pt_: (b, 0, p)), per_b(t_new, LANE)],
    )
    return pl.pallas_call(
        functools.partial(_indexer_sample_kernel, pp=pp, t_new=t_new),
        grid_spec=grid_spec,
        out_shape=[jax.ShapeDtypeStruct((db, t_new, npg * PAGE), F32), jax.ShapeDtypeStruct((db, t_new, LANE), F32)],
        compiler_params=_cp(("parallel", "arbitrary")),
        name="indexer_sample",
    )(pt, q, w, kidx_new, *([pool_kidx] * pp))


def _select_kernel(sc_ref, thr_ref, cut_ref, key_ref, *, topk, cw, tq, nq_causal, idx_bits):
    length = sc_ref.shape[1]
    if nq_causal:
        qi = pl.program_id(0) % nq_causal
        nch = (jnp.maximum((qi + 1) * tq, topk) + cw - 1) // cw
    else:
        nch = length // cw

    def make_keys(c, carry):
        off = pl.multiple_of(c * cw, cw)
        s = sc_ref[:, pl.ds(off, cw)]
        b = lax.bitcast_convert_type(s, I32)
        k = b ^ ((b >> 31) & 0x7FFFFFFF)
        key_ref[:, pl.ds(off, cw)] = jnp.where(s == 0.0, 0, k)
        return carry

    lax.fori_loop(0, nch, make_keys, 0)
    lane = lax.broadcasted_iota(I32, (tq, LANE), 1)

    def count(pred):
        def body(c, acc):
            off = pl.multiple_of(c * cw, cw)
            for u in range(cw // LANE):
                kk = key_ref[:, pl.ds(off + u * LANE, LANE)]
                acc = acc + jnp.where(pred(kk, off + u * LANE + lane), 1.0, 0.0)
            return acc
        acc = lax.fori_loop(0, nch, body, jnp.zeros((tq, LANE), F32))
        return jnp.sum(acc, axis=1, keepdims=True)

    def value_bit(it, ans):
        cand = ans | jnp.left_shift(jnp.int32(1), 31 - it)
        cs = cand ^ MIN_I32
        return jnp.where(count(lambda kk, idx: kk >= cs) >= topk, cand, ans)

    thrk = lax.fori_loop(0, 32, value_bit, jnp.zeros((tq, 1), I32)) ^ MIN_I32
    n_gt = count(lambda kk, idx: kk > thrk)
    n_ge = count(lambda kk, idx: kk >= thrk)
    need = topk - n_gt

    def tie_bit(it, m):
        cand = m | jnp.left_shift(jnp.int32(1), idx_bits - 1 - it)
        cnt = count(lambda kk, idx: jnp.where(kk == thrk, idx, cand) < cand)
        return jnp.where(cnt < need, cand, m)

    any_tie = jnp.max(jnp.abs(n_ge - topk)) > 0.0
    cut = lax.cond(any_tie, lambda: lax.fori_loop(0, idx_bits, tie_bit, jnp.zeros((tq, 1), I32)),
                   lambda: jnp.full((tq, 1), 2 ** 30, I32))
    tb = thrk ^ ((thrk >> 31) & 0x7FFFFFFF)
    thr_ref[...] = lax.bitcast_convert_type(jnp.broadcast_to(tb, (tq, LANE)), F32)
    cut_ref[...] = jnp.broadcast_to(cut, (tq, LANE))


def _select(sc, topk, cw, tq, nq_causal):
    n, length = sc.shape
    idx_bits = max(1, (length - 1).bit_length())
    return pl.pallas_call(
        functools.partial(_select_kernel, topk=topk, cw=cw, tq=tq, nq_causal=nq_causal, idx_bits=idx_bits),
        grid=(n // tq,),
        in_specs=[pl.BlockSpec((tq, length), lambda i: (i, 0))],
        out_specs=[pl.BlockSpec((tq, LANE), lambda i: (i, 0)), pl.BlockSpec((tq, LANE), lambda i: (i, 0))],
        out_shape=[jax.ShapeDtypeStruct((n, LANE), F32), jax.ShapeDtypeStruct((n, LANE), I32)],
        scratch_shapes=[pltpu.VMEM((tq, length), I32)],
        compiler_params=_cp(("parallel",)),
        name="topk_select",
    )(sc)


def _dsa_attn_kernel(q_ref, k_ref, v_ref, sc_ref, thr_ref, cut_ref, o_ref, m_ref, l_ref, acc_ref, *, tq, tk):
    qi = pl.program_id(1)
    ki = pl.program_id(2)
    last = ((qi + 1) * tq - 1) // tk

    @pl.when(ki == 0)
    def _():
        m_ref[...] = jnp.full(m_ref.shape, NEG, F32)
        l_ref[...] = jnp.zeros(l_ref.shape, F32)
        acc_ref[...] = jnp.zeros(acc_ref.shape, F32)

    @pl.when(ki <= last)
    def _():
        row = qi * tq + lax.broadcasted_iota(I32, (tq, tk), 0)
        col = ki * tk + lax.broadcasted_iota(I32, (tq, tk), 1)
        sc = sc_ref[...]
        thr = thr_ref[:, :1]
        cut = cut_ref[:, :1]
        sel = jnp.where(sc > thr, 1.0, jnp.where(sc == thr, jnp.where(col <= cut, 1.0, 0.0), 0.0))
        keep = jnp.where(col <= row, sel, 0.0)
        keep3 = jnp.concatenate([keep, keep, keep], axis=0)
        for g in range(KV_HEADS):
            q3 = q_ref[3 * g:3 * g + 3].reshape(3 * tq, LANE)
            s = _dot_nt(q3, k_ref[g // 2])
            s = jnp.where(keep3 > 0.0, s, NEG)
            m_new, a, p, l_new = _softmax_step(s, m_ref[g], l_ref[g])
            acc_ref[g] = a * acc_ref[g] + _dot(p.astype(BF16), v_ref[:, (g // 2) * LANE:(g // 2 + 1) * LANE])
            m_ref[g] = m_new
            l_ref[g] = l_new

    @pl.when(ki == last)
    def _():
        lane = lax.broadcasted_iota(I32, (tq, LANE), 1)
        for gp in range(KV_HEADS // 2):
            oe = acc_ref[2 * gp] / l_ref[2 * gp]
            oo = acc_ref[2 * gp + 1] / l_ref[2 * gp + 1]
            for r in range(3):
                blk = gp * 3 + r
                o_ref[:, blk * LANE:(blk + 1) * LANE] = jnp.where(
                    lane < HEAD_DIM, oe[r * tq:(r + 1) * tq], oo[r * tq:(r + 1) * tq]).astype(o_ref.dtype)


def _dsa_attn(q, k, v, sc, thr, cut, batch, tq, tk):
    n = v.shape[0]
    s = n // batch
    nq, nk = s // tq, s // tk
    kc = lambda qi, ki: jnp.minimum(ki, ((qi + 1) * tq - 1) // tk)
    qrow = lambda w: pl.BlockSpec((tq, w), lambda b, qi, ki: (b * nq + qi, 0))
    return pl.pallas_call(
        functools.partial(_dsa_attn_kernel, tq=tq, tk=tk),
        grid=(batch, nq, nk),
        in_specs=[pl.BlockSpec((N_HEADS, tq, LANE), lambda b, qi, ki: (0, b * nq + qi, 0)),
                  pl.BlockSpec((KV_HEADS // 2, tk, LANE), lambda b, qi, ki: (0, b * nk + kc(qi, ki), 0)),
                  pl.BlockSpec((tk, 2 * LANE), lambda b, qi, ki: (b * nk + kc(qi, ki), 0)),
                  pl.BlockSpec((tq, tk), lambda b, qi, ki: (b * nq + qi, kc(qi, ki))),
                  qrow(LANE), qrow(LANE)],
        out_specs=qrow(N_HEADS * HEAD_DIM),
        out_shape=jax.ShapeDtypeStruct((n, N_HEADS * HEAD_DIM), BF16),
        scratch_shapes=[pltpu.VMEM((KV_HEADS, 3 * tq, 1), F32), pltpu.VMEM((KV_HEADS, 3 * tq, 1), F32),
                        pltpu.VMEM((KV_HEADS, 3 * tq, LANE), F32)],
        compiler_params=_cp(("parallel", "parallel", "arbitrary")),
        name="dsa_attn",
    )(q, k, v, sc, thr, cut)


def _dsa_decode_kernel(pt_ref, q_ref, sc_ref, scn_ref, thr_ref, cut_ref, kn_ref, vn_ref, *rest, pp, t_new, past):
    k_refs, v_refs = rest[:pp], rest[pp:2 * pp]
    o_ref, m_ref, l_ref, acc_ref = rest[2 * pp:]
    p_id = pl.program_id(1)
    gr = q_ref.shape[1] // KV_HEADS
    reps = gr // t_new

    @pl.when(p_id == 0)
    def _():
        m_ref[...] = jnp.full(m_ref.shape, NEG, F32)
        l_ref[...] = jnp.zeros(l_ref.shape, F32)
        acc_ref[...] = jnp.zeros(acc_ref.shape, F32)

    thr = thr_ref[0][:, :1]
    cut = cut_ref[0][:, :1]
    q = q_ref[0]

    def keep_bias(sc, col, extra):
        sel = jnp.where(sc > thr, 1.0, jnp.where(sc == thr, jnp.where(col <= cut, 1.0, 0.0), 0.0))
        if extra is not None:
            sel = jnp.where(extra, sel, 0.0)
        return jnp.concatenate([sel] * reps, axis=0)

    def update(g, s, keep, pv_fn):
        s = jnp.where(keep > 0.0, s, NEG)
        m_new, a, p, l_new = _softmax_step(s, m_ref[g], l_ref[g])
        acc_ref[g] = a * acc_ref[g] + pv_fn(p.astype(BF16))
        m_ref[g] = m_new
        l_ref[g] = l_new

    width = pp * PAGE
    col = p_id * width + lax.broadcasted_iota(I32, (t_new, width), 1)
    keep = keep_bias(sc_ref[0], col, None)
    kbs = [[k_refs[i][:, b * LANE:(b + 1) * LANE].astype(BF16) for i in range(pp)] for b in range(2)]
    vbs = [[v_refs[i][:, b * LANE:(b + 1) * LANE].astype(BF16) for i in range(pp)] for b in range(2)]
    for g in range(KV_HEADS):
        qg = q[g * gr:(g + 1) * gr]
        s = jnp.concatenate([_dot_nt(qg, kbs[g // 2][i]) for i in range(pp)], axis=1)

        def pv_fn(pb, g=g):
            pv = _dot(pb[:, :PAGE], vbs[g // 2][0])
            for i in range(1, pp):
                pv = pv + _dot(pb[:, i * PAGE:(i + 1) * PAGE], vbs[g // 2][i])
            return pv

        update(g, s, keep, pv_fn)

    @pl.when(p_id == pl.num_programs(1) - 1)
    def _():
        npad = kn_ref.shape[1]
        t = lax.broadcasted_iota(I32, (t_new, npad), 0)
        u = lax.broadcasted_iota(I32, (t_new, npad), 1)
        keep_n = keep_bias(scn_ref[0][:, :npad], past + u, u <= t)
        for g in range(KV_HEADS):
            b = g // 2
            knb = kn_ref[0][:, b * LANE:(b + 1) * LANE].astype(BF16)
            vnb = vn_ref[0][:, b * LANE:(b + 1) * LANE].astype(BF16)
            update(g, _dot_nt(q[g * gr:(g + 1) * gr], knb), keep_n, lambda pb, vnb=vnb: _dot(pb, vnb))
            o_ref[0, g * gr:(g + 1) * gr, :] = acc_ref[g] / l_ref[g]


def _dsa_decode(page_table, q, sc, scn, thr, cut, k_new, v_new, pool_k, pool_v, j, pp, t_new):
    db, qrows, _ = q.shape
    npg = page_table.shape[1]
    pt = page_table.reshape(-1)
    npad = k_new.shape[1]
    gr = qrows // KV_HEADS
    per_b = lambda r, c: pl.BlockSpec((1, r, c), lambda b, p, pt_: (b, 0, 0))
    pool = lambda i: pl.BlockSpec((None, None, PAGE, 2 * LANE), lambda b, p, pt_: (j, pt_[b * npg + p * pp + i], 0, 0))
    grid_spec = pltpu.PrefetchScalarGridSpec(
        num_scalar_prefetch=1,
        grid=(db, npg // pp),
        in_specs=[per_b(qrows, LANE), pl.BlockSpec((1, t_new, pp * PAGE), lambda b, p, pt_: (b, 0, p)),
                  per_b(t_new, LANE), per_b(t_new, LANE), per_b(t_new, LANE), per_b(npad, 2 * LANE),
                  per_b(npad, 2 * LANE)] + [pool(i) for i in range(pp)] + [pool(i) for i in range(pp)],
        out_specs=per_b(qrows, LANE),
        scratch_shapes=[pltpu.VMEM((KV_HEADS, gr, 1), F32), pltpu.VMEM((KV_HEADS, gr, 1), F32),
                        pltpu.VMEM((KV_HEADS, gr, LANE), F32)],
    )
    return pl.pallas_call(
        functools.partial(_dsa_decode_kernel, pp=pp, t_new=t_new, past=npg * PAGE),
        grid_spec=grid_spec,
        out_shape=jax.ShapeDtypeStruct((db, qrows, LANE), F32),
        compiler_params=_cp(("parallel", "arbitrary")),
        name="dsa_decode",
    )(pt, q, sc, scn, thr, cut, k_new, v_new, *([pool_k] * pp), *([pool_v] * pp))


def _mem_kv_kernel(m_ref, w_ref, kf_ref, vf_ref, kb_ref, vb_ref):
    y = _dot(m_ref[...].astype(BF16), w_ref[...])
    w = MEM_HEADS * HEAD_DIM
    kf_ref[...] = y[:, :w]
    vf_ref[...] = y[:, w:]
    kb_ref[...] = y[:, :w].astype(BF16)
    vb_ref[...] = y[:, w:].astype(BF16)


def _mem_kv(mem2d, w):
    n = mem2d.shape[0]
    wd = MEM_HEADS * HEAD_DIM
    wb = w.astype(BF16)
    spec = pl.BlockSpec((n, wd), lambda i: (0, 0))
    return pl.pallas_call(
        _mem_kv_kernel,
        grid=(1,),
        in_specs=[pl.BlockSpec(mem2d.shape, lambda i: (0, 0)), pl.BlockSpec(wb.shape, lambda i: (0, 0))],
        out_specs=[spec] * 4,
        out_shape=[jax.ShapeDtypeStruct((n, wd), F32)] * 2 + [jax.ShapeDtypeStruct((n, wd), BF16)] * 2,
        compiler_params=_cp(("arbitrary",)),
        name="mem_kv",
    )(mem2d, wb)


def _mem_attend_kernel(q_ref, mk_ref, mv_ref, o_ref, *, bb):
    tt = q_ref.shape[1]
    lane = lax.broadcasted_iota(I32, (tt, LANE), 1)
    low = lane < HEAD_DIM
    for b in range(bb):
        for blk in range(MEM_HEADS // 2):
            sl = slice(blk * LANE, (blk + 1) * LANE)
            qb = q_ref[b, :, sl]
            mkb = mk_ref[b, :, sl].astype(BF16)
            mvb = mv_ref[b, :, sl].astype(BF16)
            res = []
            for half in range(2):
                qh = jnp.where(low if half == 0 else jnp.logical_not(low), qb, jnp.zeros_like(qb))
                s = _dot_nt(qh, mkb)
                p = jnp.exp(s - jnp.max(s, axis=1, keepdims=True))
                p = p / jnp.sum(p, axis=1, keepdims=True)
                res.append(_dot(p.astype(BF16), mvb))
            o_ref[b, :, sl] = jnp.where(low, res[0], res[1]).astype(o_ref.dtype)


def _mem_attend(q, mk, mv, bb, tt):
    bm, t, wd = q.shape
    nm = mk.shape[1]
    return pl.pallas_call(
        functools.partial(_mem_attend_kernel, bb=bb),
        grid=(bm // bb, t // tt),
        in_specs=[pl.BlockSpec((bb, tt, wd), lambda b, i: (b, i, 0)), pl.BlockSpec((bb, nm, wd), lambda b, i: (b, 0, 0)),
                  pl.BlockSpec((bb, nm, wd), lambda b, i: (b, 0, 0))],
        out_specs=pl.BlockSpec((bb, tt, wd), lambda b, i: (b, i, 0)),
        out_shape=jax.ShapeDtypeStruct((bm, t, wd), BF16),
        compiler_params=_cp(("parallel", "parallel")),
        name="mem_attend",
    )(q, mk, mv)


def _split_bf16(x):
    hi = x.astype(BF16)
    return hi, (x - hi.astype(F32)).astype(BF16)


def _outproj_router_kernel(x_ref, mix_ref, mem_ref, wo1_ref, wo2_ref, g_ref, b_ref, wrh_ref, wrl_ref, rb_ref,
                           x1_ref, gate_ref):
    y = ALPHA * x_ref[...] + _dot(mix_ref[...].astype(BF16), wo1_ref[...]) + _dot(mem_ref[...], wo2_ref[...])
    x1 = _ln(y, g_ref[...], b_ref[...])
    x1_ref[...] = x1
    xh, xl = _split_bf16(x1)
    logit = _dot_nt(wrh_ref[...], xh) + _dot_nt(wrh_ref[...], xl) + _dot_nt(wrl_ref[...], xh)
    s = 1.0 / (1.0 + jnp.exp(-logit))
    sb = s + rb_ref[...]
    epg = N_EXPERTS // N_GROUPS
    rows = [sb[e:e + 1] for e in range(N_EXPERTS)]
    gscore = []
    for g in range(N_GROUPS):
        v = rows[g * epg:(g + 1) * epg]
        best = None
        for a in range(epg):
            for b in range(a + 1, epg):
                pr = v[a] + v[b]
                best = pr if best is None else jnp.maximum(best, pr)
        gscore.append(best)
    gmax = functools.reduce(jnp.maximum, gscore)
    taken = jnp.zeros_like(gmax)
    gsel = []
    for g in range(N_GROUPS):
        pick = jnp.where(gscore[g] == gmax, 1.0, 0.0) * (1.0 - taken)
        taken = taken + pick
        gsel.append(pick)
    val = [jnp.where(gsel[e // epg] > 0.0, rows[e], NEG) for e in range(N_EXPERTS)]
    chosen = [jnp.zeros_like(gmax) for _ in range(N_EXPERTS)]
    for _ in range(2):
        vmax = functools.reduce(jnp.maximum, val)
        taken = jnp.zeros_like(gmax)
        for e in range(N_EXPERTS):
            pick = jnp.where(val[e] == vmax, 1.0, 0.0) * (1.0 - taken)
            taken = taken + pick
            chosen[e] = chosen[e] + pick
            val[e] = jnp.where(pick > 0.0, 2.0 * NEG, val[e])
    wsel = [jnp.where(chosen[e] > 0.0, s[e:e + 1], 0.0) for e in range(N_EXPERTS)]
    wsum = functools.reduce(lambda a, b: a + b, wsel)
    gate_ref[...] = jnp.concatenate([w / wsum for w in wsel], axis=0)


def _outproj_router(x, mix, mem, wo1, wo2, g, b, wrh, wrl, rb, tm):
    n = x.shape[0]
    full = lambda a: pl.BlockSpec(a.shape, lambda i: (0,) * a.ndim)
    rows = lambda w: pl.BlockSpec((tm, w), lambda i: (i, 0))
    return pl.pallas_call(
        _outproj_router_kernel,
        grid=(n // tm,),
        in_specs=[rows(D_MODEL), rows(mix.shape[1]), rows(mem.shape[1]), full(wo1), full(wo2), full(g), full(b),
                  full(wrh), full(wrl), full(rb)],
        out_specs=[rows(D_MODEL), pl.BlockSpec((N_EXPERTS, tm), lambda i: (0, i))],
        out_shape=[jax.ShapeDtypeStruct((n, D_MODEL), F32), jax.ShapeDtypeStruct((N_EXPERTS, n), F32)],
        compiler_params=_cp(("parallel",)),
        name="outproj_router",
    )(x, mix, mem, wo1, wo2, g, b, wrh, wrl, rb)


def _moe_kernel(x_ref, gate_ref, wgu_ref, wd_ref, g_ref, b_ref, o_ref, acc_ref, xb_ref):
    e = pl.program_id(1)

    @pl.when(e == 0)
    def _():
        acc_ref[...] = jnp.zeros(acc_ref.shape, F32)
        xb_ref[...] = x_ref[...].astype(BF16)

    gate = gate_ref[...]
    lane = lax.broadcasted_iota(I32, gate.shape, 1)
    gcol = jnp.sum(jnp.where(lane == e, gate, 0.0), axis=1, keepdims=True)
    hgu = _dot(xb_ref[...], wgu_ref[...])
    hg, hu = hgu[:, :D_EXPERT], hgu[:, D_EXPERT:]
    h = hg * (1.0 / (1.0 + jnp.exp(-hg))) * hu * gcol
    acc_ref[...] += _dot(h.astype(BF16), wd_ref[...])

    @pl.when(e == pl.num_programs(1) - 1)
    def _():
        o_ref[...] = _ln(ALPHA * x_ref[...] + acc_ref[...], g_ref[...], b_ref[...])


def _moe(x, gate, wgu, wd, g, b, tm):
    n = x.shape[0]
    full = lambda a: pl.BlockSpec(a.shape, lambda i, e: (0,) * a.ndim)
    return pl.pallas_call(
        _moe_kernel,
        grid=(n // tm, N_EXPERTS),
        in_specs=[pl.BlockSpec((tm, D_MODEL), lambda i, e: (i, 0)), pl.BlockSpec((tm, N_EXPERTS), lambda i, e: (i, 0)),
                  pl.BlockSpec((None, D_MODEL, 2 * D_EXPERT), lambda i, e: (e, 0, 0)),
                  pl.BlockSpec((None, D_EXPERT, D_MODEL), lambda i, e: (e, 0, 0)), full(g), full(b)],
        out_specs=pl.BlockSpec((tm, D_MODEL), lambda i, e: (i, 0)),
        out_shape=jax.ShapeDtypeStruct((n, D_MODEL), F32),
        scratch_shapes=[pltpu.VMEM((tm, D_MODEL), F32), pltpu.VMEM((tm, D_MODEL), BF16)],
        compiler_params=_cp(("parallel", "arbitrary")),
        name="moe",
    )(x, gate, wgu, wd, g, b)


def _tile(n, pref):
    t = min(n, pref)
    while n % t:
        t //= 2
    return t


def kernel(x_prompt, x_sample, cache_mla_ckv, cache_mla_krope, cache_dsa_k, cache_dsa_v, cache_dsa_kidx, cache_mem_k, cache_mem_v, page_table, mem_prompt, w_in_mla, mla_q_norm, mla_kv_norm, w_uq, w_uk, w_uv, w_in_dsa, idx_k_norm_g, idx_k_norm_b, w_mem_kv, w_out, ln1_g, ln1_b, ln2_g, ln2_b, w_router, router_bias, w_gate, w_up, w_down):
    B, S, _ = x_prompt.shape
    DB, T, _ = x_sample.shape
    n_mem = mem_prompt.shape[1]
    npg = page_table.shape[1]
    past = npg * PAGE
    n_pool = cache_mla_ckv.shape[1]
    NP, NS = B * S, DB * T
    TPAD = 16
    pp = 8 if npg % 8 == 0 else npg
    tm_p, tm_s = _tile(NP, 512), _tile(NS, 512)
    tq_f = _tile(S, 512)
    tq_d, tk_d = _tile(S, 256), _tile(S, 512)
    tq_sel = 128

    pos_p = jnp.arange(S, dtype=jnp.int32)
    pos_s = jnp.tile(past + jnp.arange(T, dtype=jnp.int32), DB)
    cos_mp, sin_mp = _rope_tables(pos_p, QK_ROPE, LANE, QK_NOPE, LANE)
    cos_ms, sin_ms = _rope_tables(pos_s, QK_ROPE, LANE, QK_NOPE, LANE)
    cos_mr, sin_mr = _rope_tables(pos_s, QK_ROPE, QK_ROPE, 0, N_HEADS * QK_ROPE)
    cos_dp, sin_dp = _rope_tables(pos_p, ROT_DIM, HEAD_DIM, 0, LANE)
    cos_ds, sin_ds = _rope_tables(pos_s, ROT_DIM, HEAD_DIM, 0, LANE)

    wrt = w_router.T
    wrh = wrt.astype(BF16)
    wrl = (wrt - wrh.astype(F32)).astype(BF16)
    rb = router_bias.reshape(N_EXPERTS, 1).astype(F32)
    perm = jnp.asarray(DSA_PERM)
    pool_k = cache_dsa_k.reshape(cache_dsa_k.shape[0], n_pool, PAGE, KV_HEADS * HEAD_DIM)
    pool_v = cache_dsa_v.reshape(cache_dsa_v.shape[0], n_pool, PAGE, KV_HEADS * HEAD_DIM)
    mem2d = mem_prompt.reshape(B * n_mem, D_MODEL)

    def pad_new(a):
        a = a.reshape(DB, T, a.shape[-1])
        return jnp.concatenate([a, jnp.zeros((DB, TPAD - T, a.shape[-1]), a.dtype)], axis=1)

    def post(x, mix, mem, i, wo_rows, tm):
        wo = w_out[i]
        wo1 = wo[:N_HEADS * HEAD_DIM]
        if wo_rows is not None:
            wo1 = wo1.reshape(N_HEADS, HEAD_DIM, D_MODEL)[wo_rows].reshape(N_HEADS * HEAD_DIM, D_MODEL)
        x1, gate_t = _outproj_router(x, mix, mem, wo1.astype(BF16), wo[N_HEADS * HEAD_DIM:].astype(BF16),
                                     ln1_g[i].reshape(1, -1), ln1_b[i].reshape(1, -1), wrh, wrl, rb, tm)
        wgu = jnp.concatenate([w_gate[i], w_up[i]], axis=-1).astype(BF16)
        return _moe(x1, gate_t.T, wgu, w_down[i].astype(BF16), ln2_g[i].reshape(1, -1), ln2_b[i].reshape(1, -1),
                    _tile(x.shape[0], 1024))

    xp = x_prompt.reshape(NP, D_MODEL)
    xs = x_sample.reshape(NS, D_MODEL)
    ckv_p, kr_p, ckv_s, kr_s = [], [], [], []
    k_p, v_p, ki_p, k_s, v_s, ki_s = [], [], [], [], [], []
    mk_list, mv_list = [], []
    for i in range(DEPTH):
        j = i // 2
        mkf, mvf, mkb, mvb = _mem_kv(mem2d, w_mem_kv[i])
        mk_list.append(mkf.reshape(B, n_mem, MEM_HEADS, HEAD_DIM))
        mv_list.append(mvf.reshape(B, n_mem, MEM_HEADS, HEAD_DIM))
        if i % 2 == 0:
            q, k, v, ckv, krpad, qm_p = _mla_proj_prompt(xp, w_in_mla[j], mla_q_norm[j], mla_kv_norm[j], w_uq[j],
                                                         w_uk[j], w_uv[j], cos_mp, sin_mp, tm_p)
            mix_p = _mla_flash(q, k, v, B, tq_f, tq_f)
            ckv_p.append(ckv.reshape(B, S, KV_LORA))
            kr_p.append(krpad[:, QK_NOPE:QK_NOPE + QK_ROPE].reshape(B, S, QK_ROPE))
            qlat, qr, ckv, krpad, qm_s = _mla_proj_sample(xs, w_in_mla[j], mla_q_norm[j], mla_kv_norm[j], w_uq[j],
                                                          w_uk[j], cos_ms, sin_ms, cos_mr, sin_mr, tm_s)
            kr = krpad[:, QK_NOPE:QK_NOPE + QK_ROPE]
            ckv_s.append(ckv.reshape(DB, T, KV_LORA))
            kr_s.append(kr.reshape(DB, T, QK_ROPE))
            qlat = qlat.reshape(DB, T, N_HEADS, KV_LORA).transpose(0, 2, 1, 3).reshape(DB, N_HEADS * T, KV_LORA)
            qr = qr.reshape(DB, T, N_HEADS, QK_ROPE).transpose(0, 2, 1, 3).reshape(DB, N_HEADS * T, QK_ROPE)
            o_lat = _mla_decode(page_table, qlat, qr, pad_new(ckv), pad_new(kr), cache_mla_ckv, cache_mla_krope,
                                j, pp, T)
            o_lat = o_lat.reshape(DB, N_HEADS, T, KV_LORA).transpose(1, 0, 2, 3).reshape(N_HEADS, NS, KV_LORA)
            o = _head_matmul(o_lat, jnp.transpose(w_uv[j], (1, 0, 2)).astype(BF16))
            mix_s = o.transpose(1, 0, 2).reshape(NS, N_HEADS * HEAD_DIM)
            wo_rows = None
        else:
            q, kb, kf, vf, vb, qidx, kw, kib, qm_p = _dsa_proj(xp, w_in_dsa[j], idx_k_norm_g[j], idx_k_norm_b[j],
                                                               cos_dp, sin_dp, tm_p)
            k_p.append(kf.reshape(B, S, KV_HEADS, HEAD_DIM))
            v_p.append(vf.reshape(B, S, KV_HEADS, HEAD_DIM))
            ki_p.append(kw[:, :IDX_DIM].reshape(B, S, IDX_DIM))
            sc = _indexer_prompt(qidx, kib, kw, B, tq_f, tq_f)
            topk_p = min(TOPK_MAX, S // 4)
            thr, cut = _select(sc, topk_p, _tile(S, 512), tq_sel, S // tq_sel)
            mix_p = _dsa_attn(q, kb, vb, sc, thr, cut, B, tq_d, tk_d)

            q, kb, kf, vf, vb, qidx, kw, kib, qm_s = _dsa_proj(xs, w_in_dsa[j], idx_k_norm_g[j], idx_k_norm_b[j],
                                                               cos_ds, sin_ds, tm_s)
            k_s.append(kf.reshape(DB, T, KV_HEADS, HEAD_DIM))
            v_s.append(vf.reshape(DB, T, KV_HEADS, HEAD_DIM))
            kidx_new = kw[:, :IDX_DIM]
            ki_s.append(kidx_new.reshape(DB, T, IDX_DIM))
            qc = jnp.stack([qidx[h, :, (h % 2) * HEAD_DIM:(h % 2 + 1) * HEAD_DIM] for h in range(IDX_HEADS)])
            qc = qc.reshape(IDX_HEADS, DB, T, IDX_DIM).transpose(1, 0, 2, 3).reshape(DB, IDX_HEADS * T, IDX_DIM)
            w_s = kw[:, HEAD_DIM:HEAD_DIM + IDX_HEADS].reshape(DB, T, IDX_HEADS)
            sc_past, sc_new = _indexer_sample(page_table, qc, w_s, pad_new(kidx_new), cache_dsa_kidx, j, pp, T)
            length = past + T
            topk_s = min(TOPK_MAX, length // 4)
            cw_s = 5 * LANE if (past + LANE) % (5 * LANE) == 0 else LANE
            sc_all = jnp.concatenate([sc_past, sc_new], axis=-1).reshape(NS, past + LANE)
            thr, cut = _select(sc_all, topk_s, cw_s, _tile(NS, tq_sel), 0)
            qg = q.reshape(KV_HEADS, 3, DB, T, LANE).transpose(2, 0, 1, 3, 4)
            qg = jnp.concatenate([qg, jnp.zeros((DB, KV_HEADS, 1, T, LANE), BF16)], axis=2)
            qg = qg.reshape(DB, KV_HEADS * 4 * T, LANE)
            o = _dsa_decode(page_table, qg, sc_past, sc_new, thr.reshape(DB, T, LANE), cut.reshape(DB, T, LANE),
                            pad_new(kf), pad_new(vf), pool_k, pool_v, j, pp, T)
            o = o.reshape(DB, KV_HEADS, 4, T, LANE)[:, :, :3]
            o = jnp.stack([o[:, g, :, :, (g % 2) * HEAD_DIM:(g % 2 + 1) * HEAD_DIM] for g in range(KV_HEADS)], 1)
            o = o.reshape(DB, N_HEADS, T, HEAD_DIM)[:, perm]
            mix_s = o.transpose(0, 2, 1, 3).reshape(NS, N_HEADS * HEAD_DIM)
            wo_rows = perm
        memo_p = _mem_attend(qm_p.reshape(B, S, -1), mkb.reshape(B, n_mem, -1), mvb.reshape(B, n_mem, -1), 1,
                             _tile(S, 512))
        qm_s3 = qm_s.reshape(DB, T, -1)
        qm_s3 = jnp.concatenate([qm_s3, jnp.zeros((DB, TPAD - T, qm_s3.shape[-1]), BF16)], axis=1)
        memo_s = _mem_attend(qm_s3, cache_mem_k[i].reshape(DB, n_mem, -1), cache_mem_v[i].reshape(DB, n_mem, -1),
                             _tile(DB, 8), TPAD)[:, :T]
        xp = post(xp, mix_p, memo_p.reshape(NP, -1), i, wo_rows, tm_p)
        xs = post(xs, mix_s, memo_s.reshape(NS, -1), i, wo_rows, tm_s)
    return (xp.reshape(B, S, D_MODEL), xs.reshape(DB, T, D_MODEL),
            jnp.stack(ckv_p), jnp.stack(kr_p), jnp.stack(k_p), jnp.stack(v_p), jnp.stack(ki_p),
            jnp.stack(mk_list), jnp.stack(mv_list),
            jnp.stack(ckv_s), jnp.stack(kr_s), jnp.stack(k_s), jnp.stack(v_s), jnp.stack(ki_s))
```

```python
import functools

import jax
import jax.numpy as jnp
from jax import lax
from jax.experimental import pallas as pl
from jax.experimental.pallas import tpu as pltpu

F32 = jnp.float32
BF16 = jnp.bfloat16
I32 = jnp.int32

D_MODEL = 1024
DEPTH = 4
PAGE = 128
HEAD_DIM = 64
N_HEADS = 12
MEM_HEADS = 4
Q_LORA = 384
KV_LORA = 256
QK_NOPE = 64
QK_ROPE = 32
MLA_SCALE = (QK_NOPE + QK_ROPE) ** -0.5
HEAD_SCALE = HEAD_DIM ** -0.5
LOG2E = 1.4426950408889634
MLA_QSCALE = MLA_SCALE * LOG2E
DSA_QSCALE = HEAD_SCALE * LOG2E
KV_HEADS = 4
IDX_HEADS = 8
IDX_DIM = 64
TOPK_MAX = 256
ROT_DIM = HEAD_DIM // 4
ROPE_THETA = 500000.0
N_EXPERTS = 16
N_GROUPS = 4
D_EXPERT = 256
ALPHA = (2 * DEPTH) ** 0.25
LN_EPS = 1e-5
RMS_EPS = 1e-6
NEG = -1e30
LANE = 128
SUBLANE = 8
MIN_I32 = -2 ** 31
N_ACC = 8
DSA_PERM = (0, 3, 1, 4, 2, 5, 6, 9, 7, 10, 8, 11)

NT = (((1,), (1,)), ((), ()))


def _cp(sem, vmem_mb=48):
    return pltpu.CompilerParams(dimension_semantics=sem, vmem_limit_bytes=vmem_mb * 1024 * 1024)


def _dot(a, b):
    return jnp.dot(a, b, preferred_element_type=F32)


def _dot_nt(a, b):
    return lax.dot_general(a, b, NT, preferred_element_type=F32)


def _rms(x, g):
    return x * lax.rsqrt(jnp.mean(x * x, axis=-1, keepdims=True) + RMS_EPS) * g


def _ln(x, g, b):
    mu = jnp.mean(x, axis=-1, keepdims=True)
    xc = x - mu
    var = jnp.mean(xc * xc, axis=-1, keepdims=True)
    return xc * lax.rsqrt(var + LN_EPS) * g + b


def _softmax_step(s, m_prev, l_prev, axis=1):
    m_new = jnp.maximum(m_prev, jnp.max(s, axis=axis, keepdims=True))
    a = jnp.exp2(m_prev - m_new)
    p = jnp.exp2(s - m_new)
    return m_new, a, p, a * l_prev + jnp.sum(p, axis=axis, keepdims=True)


def _merge_streams(ms, ls, accs):
    m = functools.reduce(jnp.maximum, ms)
    ws = [jnp.exp2(mi - m) for mi in ms]
    l = functools.reduce(lambda a, b: a + b, [w * li for w, li in zip(ws, ls)])
    acc = functools.reduce(lambda a, b: a + b, [w * ai for w, ai in zip(ws, accs)])
    return m, l, acc


def _score_keys(s):
    b = lax.bitcast_convert_type(s, I32)
    return jnp.where(s == 0.0, 0, b ^ ((b >> 31) & 0x7FFFFFFF))


def _key_to_score(k, shape):
    return lax.bitcast_convert_type(jnp.broadcast_to(k ^ ((k >> 31) & 0x7FFFFFFF), shape), F32)


def _rope_tables(pos, rot_dim, period, offset, width):
    half = rot_dim // 2
    inv = ROPE_THETA ** (-jnp.arange(half, dtype=jnp.float32) / half)
    ang = pos.astype(jnp.float32)[:, None] * inv
    c, s = jnp.cos(ang), jnp.sin(ang)
    t = pos.shape[0]
    cg = jnp.ones((t, period), F32).at[:, offset:offset + rot_dim].set(jnp.concatenate([c, c], 1))
    sg = jnp.zeros((t, period), F32).at[:, offset:offset + rot_dim].set(jnp.concatenate([s, s], 1))
    return jnp.tile(cg, (1, width // period)), jnp.tile(sg, (1, width // period))


def _rot_cols(w, half):
    return jnp.concatenate([-w[..., half:2 * half], w[..., :half]], axis=-1)


def _rot_heads(w, heads, dim, rot):
    w3 = w.reshape(w.shape[0], heads, dim)
    r = jnp.concatenate([_rot_cols(w3[..., :rot], rot // 2), jnp.zeros_like(w3[..., rot:])], axis=-1)
    return r.reshape(w.shape[0], heads * dim)


def _mla_front(x_ref, w1_ref, w2_ref, qn_ref, kvn_ref, cos_ref, sin_ref):
    xb = x_ref[...].astype(BF16)
    y = _dot(xb, w1_ref[...])
    cqn = _rms(y[:, :Q_LORA], qn_ref[...]).astype(BF16)
    ckv = _rms(y[:, Q_LORA:Q_LORA + KV_LORA], kvn_ref[...])
    qmem = y[:, 640:896]
    cos = cos_ref[...]
    sin = sin_ref[...]
    krp = y[:, 896:1024] * cos + y[:, 1024:1152] * sin
    q2 = _dot(cqn, w2_ref[...])
    return q2, ckv, qmem, krp, cos, sin


def _mla_proj_prompt_kernel(x_ref, w1_ref, w2_ref, wk_ref, wvt_ref, qn_ref, kvn_ref, cos_ref, sin_ref,
                            q_ref, k_ref, vt_ref, ckv_ref, kr_ref, qm_ref):
    q2, ckv, qmem, krp, cos, sin = _mla_front(x_ref, w1_ref, w2_ref, qn_ref, kvn_ref, cos_ref, sin_ref)
    hw = N_HEADS * LANE
    for h in range(N_HEADS):
        qh = q2[:, h * LANE:(h + 1) * LANE] * cos + q2[:, hw + h * LANE:hw + (h + 1) * LANE] * sin
        q_ref[h] = (qh * MLA_QSCALE).astype(BF16)
    ckv_ref[...] = ckv
    kr_ref[...] = krp
    qm_ref[...] = (qmem * HEAD_SCALE).astype(BF16)
    cb = ckv.astype(BF16)
    kn = _dot(cb, wk_ref[...])
    for h in range(N_HEADS):
        k_ref[h] = (kn[:, h * LANE:(h + 1) * LANE] + krp).astype(BF16)
    vt_ref[...] = _dot_nt(wvt_ref[...], cb).astype(BF16)


def _mla_proj_sample_kernel(x_ref, w1_ref, w2_ref, wukt_ref, qn_ref, kvn_ref, cos_ref, sin_ref,
                            cosr_ref, sinr_ref, qlat_ref, qr_ref, ckv_ref, kr_ref, qm_ref):
    q2, ckv, qmem, krp, _, _ = _mla_front(x_ref, w1_ref, w2_ref, qn_ref, kvn_ref, cos_ref, sin_ref)
    hw = N_HEADS * LANE
    rw = N_HEADS * QK_ROPE
    for h in range(N_HEADS):
        qh = (q2[:, h * LANE:(h + 1) * LANE] * MLA_QSCALE).astype(BF16)
        qlat_ref[:, h * KV_LORA:(h + 1) * KV_LORA] = _dot(qh, wukt_ref[h]).astype(BF16)
    qr = q2[:, hw:hw + rw] * cosr_ref[...] + q2[:, hw + rw:hw + 2 * rw] * sinr_ref[...]
    qr_ref[...] = (qr * MLA_QSCALE).astype(BF16)
    ckv_ref[...] = ckv
    kr_ref[...] = krp
    qm_ref[...] = (qmem * HEAD_SCALE).astype(BF16)


def _mla_w1(w_in):
    w_cq, w_ckv = w_in[:, :Q_LORA], w_in[:, Q_LORA:Q_LORA + KV_LORA]
    w_kr = w_in[:, Q_LORA + KV_LORA:Q_LORA + KV_LORA + QK_ROPE]
    w_qm = w_in[:, Q_LORA + KV_LORA + QK_ROPE:]
    z = lambda n: jnp.zeros((D_MODEL, n), F32)
    kr_pad = jnp.concatenate([z(QK_NOPE), w_kr, z(LANE - QK_NOPE - QK_ROPE)], 1)
    kr_rot = jnp.concatenate([z(QK_NOPE), _rot_cols(w_kr, QK_ROPE // 2), z(LANE - QK_NOPE - QK_ROPE)], 1)
    return jnp.concatenate([w_cq, w_ckv, w_qm, kr_pad, kr_rot], 1).astype(BF16)


def _mla_proj_prompt(x, w_in, q_norm, kv_norm, w_uq, w_uk, w_uv, cos, sin, tm):
    n = x.shape[0]
    w1 = _mla_w1(w_in)
    zq = jnp.zeros((Q_LORA, N_HEADS, LANE - QK_NOPE - QK_ROPE), F32)
    uq_pad = jnp.concatenate([w_uq, zq], -1).reshape(Q_LORA, N_HEADS * LANE)
    uq_rot = jnp.concatenate([jnp.zeros((Q_LORA, N_HEADS, QK_NOPE), F32),
                              _rot_cols(w_uq[..., QK_NOPE:], QK_ROPE // 2), zq], -1).reshape(Q_LORA, N_HEADS * LANE)
    w2 = jnp.concatenate([uq_pad, uq_rot], 1).astype(BF16)
    wk = jnp.concatenate([w_uk, jnp.zeros((KV_LORA, N_HEADS, LANE - QK_NOPE), F32)], -1)
    wk = wk.reshape(KV_LORA, N_HEADS * LANE).astype(BF16)
    wvt = w_uv.reshape(KV_LORA, N_HEADS * HEAD_DIM).T.astype(BF16)
    nt = cos.shape[0] // tm
    full = lambda a: pl.BlockSpec(a.shape, lambda i: (0,) * a.ndim)
    rows = lambda w: pl.BlockSpec((tm, w), lambda i: (i, 0))
    heads = pl.BlockSpec((N_HEADS, tm, LANE), lambda i: (0, i, 0))
    tab = pl.BlockSpec((tm, LANE), lambda i: (i % nt, 0))
    qn, kvn = q_norm.reshape(1, -1), kv_norm.reshape(1, -1)
    return pl.pallas_call(
        _mla_proj_prompt_kernel,
        grid=(n // tm,),
        in_specs=[rows(D_MODEL), full(w1), full(w2), full(wk), full(wvt), full(qn), full(kvn), tab, tab],
        out_specs=[heads, heads, pl.BlockSpec((N_HEADS * HEAD_DIM, tm), lambda i: (0, i)), rows(KV_LORA), rows(LANE),
                   rows(MEM_HEADS * HEAD_DIM)],
        out_shape=[jax.ShapeDtypeStruct((N_HEADS, n, LANE), BF16), jax.ShapeDtypeStruct((N_HEADS, n, LANE), BF16),
                   jax.ShapeDtypeStruct((N_HEADS * HEAD_DIM, n), BF16), jax.ShapeDtypeStruct((n, KV_LORA), F32),
                   jax.ShapeDtypeStruct((n, LANE), F32), jax.ShapeDtypeStruct((n, MEM_HEADS * HEAD_DIM), BF16)],
        compiler_params=_cp(("parallel",)),
        name="mla_proj_prompt",
    )(x, w1, w2, wk, wvt, qn, kvn, cos, sin)


def _mla_proj_sample(x, w_in, q_norm, kv_norm, w_uq, w_uk, cos, sin, cosr, sinr, tm):
    n = x.shape[0]
    w1 = _mla_w1(w_in)
    zq = jnp.zeros((Q_LORA, N_HEADS, LANE - QK_NOPE - QK_ROPE), F32)
    uq_pad = jnp.concatenate([w_uq, zq], -1).reshape(Q_LORA, N_HEADS * LANE)
    w_qr = w_uq[..., QK_NOPE:]
    w2 = jnp.concatenate([uq_pad, w_qr.reshape(Q_LORA, -1),
                          _rot_cols(w_qr, QK_ROPE // 2).reshape(Q_LORA, -1)], 1).astype(BF16)
    wukt = jnp.transpose(w_uk, (1, 2, 0))
    wukt = jnp.concatenate([wukt, jnp.zeros((N_HEADS, LANE - QK_NOPE, KV_LORA), F32)], 1).astype(BF16)
    full = lambda a: pl.BlockSpec(a.shape, lambda i: (0,) * a.ndim)
    rows = lambda w: pl.BlockSpec((tm, w), lambda i: (i, 0))
    qn, kvn = q_norm.reshape(1, -1), kv_norm.reshape(1, -1)
    return pl.pallas_call(
        _mla_proj_sample_kernel,
        grid=(n // tm,),
        in_specs=[rows(D_MODEL), full(w1), full(w2), full(wukt), full(qn), full(kvn), rows(LANE), rows(LANE),
                  rows(N_HEADS * QK_ROPE), rows(N_HEADS * QK_ROPE)],
        out_specs=[rows(N_HEADS * KV_LORA), rows(N_HEADS * QK_ROPE), rows(KV_LORA), rows(LANE),
                   rows(MEM_HEADS * HEAD_DIM)],
        out_shape=[jax.ShapeDtypeStruct((n, N_HEADS * KV_LORA), BF16),
                   jax.ShapeDtypeStruct((n, N_HEADS * QK_ROPE), BF16), jax.ShapeDtypeStruct((n, KV_LORA), F32),
                   jax.ShapeDtypeStruct((n, LANE), F32), jax.ShapeDtypeStruct((n, MEM_HEADS * HEAD_DIM), BF16)],
        compiler_params=_cp(("parallel",)),
        name="mla_proj_sample",
    )(x, w1, w2, wukt, qn, kvn, cos, sin, cosr, sinr)


def _causal_steps(nq, tq, tk):
    qs, ks = [], []
    for qi in range(nq):
        for ki in range(((qi + 1) * tq - 1) // tk + 1):
            qs.append(qi)
            ks.append(ki)
    return jnp.asarray(qs, I32), jnp.asarray(ks, I32)


def _mla_flash_kernel(qt_ref, kt_ref, q_ref, k_ref, vt_ref, o_ref, m_ref, l_ref, acc_ref, *, tq, tk):
    qi = qt_ref[pl.program_id(2)]
    ki = kt_ref[pl.program_id(2)]
    last = ((qi + 1) * tq - 1) // tk

    @pl.when(ki == 0)
    def _():
        m_ref[...] = jnp.full(m_ref.shape, NEG, F32)
        l_ref[...] = jnp.zeros(l_ref.shape, F32)
        acc_ref[...] = jnp.zeros(acc_ref.shape, F32)

    def step(masked):
        ss = [_dot_nt(k_ref[hh], q_ref[hh]) for hh in range(2)]
        if masked:
            key = ki * tk + lax.broadcasted_iota(I32, (tk, tq), 0)
            qry = qi * tq + lax.broadcasted_iota(I32, (tk, tq), 1)
            causal = key <= qry
        for hh in range(2):
            s = jnp.where(causal, ss[hh], NEG) if masked else ss[hh]
            m_new, a, p, l_new = _softmax_step(s, m_ref[hh], l_ref[hh], axis=0)
            vt = vt_ref[hh * HEAD_DIM:(hh + 1) * HEAD_DIM, :]
            acc_ref[hh] = a * acc_ref[hh] + _dot(vt, p.astype(BF16))
            m_ref[hh] = m_new
            l_ref[hh] = l_new

    first_masked = (qi * tq + 1) // tk

    @pl.when(ki < first_masked)
    def _():
        step(False)

    @pl.when(ki >= first_masked)
    def _():
        step(True)

    @pl.when(ki == last)
    def _():
        ot = jnp.concatenate([acc_ref[0] / l_ref[0], acc_ref[1] / l_ref[1]], axis=0)
        o_ref[...] = ot.T.astype(o_ref.dtype)


def _mla_flash(q, k, vt, batch, tq, tk):
    n = q.shape[1]
    s = n // batch
    nq, nk = s // tq, s // tk
    qtab, ktab = _causal_steps(nq, tq, tk)
    grid_spec = pltpu.PrefetchScalarGridSpec(
        num_scalar_prefetch=2,
        grid=(batch, N_HEADS // 2, qtab.shape[0]),
        in_specs=[pl.BlockSpec((2, tq, LANE), lambda b, hp, t, qt, kt: (hp, b * nq + qt[t], 0)),
                  pl.BlockSpec((2, tk, LANE), lambda b, hp, t, qt, kt: (hp, b * nk + kt[t], 0)),
                  pl.BlockSpec((2 * HEAD_DIM, tk), lambda b, hp, t, qt, kt: (hp, b * nk + kt[t]))],
        out_specs=pl.BlockSpec((tq, LANE), lambda b, hp, t, qt, kt: (b * nq + qt[t], hp)),
        scratch_shapes=[pltpu.VMEM((2, 1, tq), F32), pltpu.VMEM((2, 1, tq), F32), pltpu.VMEM((2, HEAD_DIM, tq), F32)],
    )
    return pl.pallas_call(
        functools.partial(_mla_flash_kernel, tq=tq, tk=tk),
        grid_spec=grid_spec,
        out_shape=jax.ShapeDtypeStruct((n, N_HEADS * HEAD_DIM), BF16),
        compiler_params=_cp(("parallel", "parallel", "arbitrary")),
        name="mla_flash",
    )(qtab, ktab, q, k, vt)


def _mla_decode_kernel(pt_ref, qlat_ref, qr_ref, cn_ref, kn_ref, *rest, pp, ns, t_new):
    ck_refs, krt_refs = rest[:pp], rest[pp:2 * pp]
    o_ref, m_ref, l_ref, acc_ref = rest[2 * pp:]
    p_id = pl.program_id(1)
    rows = qlat_ref.shape[1]
    per = pp // ns

    @pl.when(p_id == 0)
    def _():
        m_ref[...] = jnp.full(m_ref.shape, NEG, F32)
        l_ref[...] = jnp.zeros(l_ref.shape, F32)
        acc_ref[...] = jnp.zeros(acc_ref.shape, F32)

    qlat = qlat_ref[0]
    qr = qr_ref[0]
    for st in range(ns):
        pages = range(st * per, (st + 1) * per)
        cks = [ck_refs[i][...].astype(BF16) for i in pages]
        s = jnp.concatenate([_dot_nt(qlat, ck) + _dot(qr, krt_refs[i][...].astype(BF16))
                             for ck, i in zip(cks, pages)], axis=1)
        m_new, a, p, l_new = _softmax_step(s, m_ref[st], l_ref[st])
        pb = p.astype(BF16)
        pv = _dot(pb[:, :PAGE], cks[0])
        for i in range(1, per):
            pv = pv + _dot(pb[:, i * PAGE:(i + 1) * PAGE], cks[i])
        acc_ref[st] = a * acc_ref[st] + pv
        m_ref[st] = m_new
        l_ref[st] = l_new

    @pl.when(p_id == pl.num_programs(1) - 1)
    def _():
        m, l, acc = _merge_streams([m_ref[st] for st in range(ns)], [l_ref[st] for st in range(ns)],
                                   [acc_ref[st] for st in range(ns)])
        cn = cn_ref[0].astype(BF16)
        kn = kn_ref[0].astype(BF16)
        npad = cn.shape[0]
        t = lax.broadcasted_iota(I32, (rows, npad), 0) % t_new
        u = lax.broadcasted_iota(I32, (rows, npad), 1)
        s2 = jnp.where(u <= t, _dot_nt(qlat, cn) + _dot_nt(qr, kn), NEG)
        m2, a2, p2, l2 = _softmax_step(s2, m, l)
        o_ref[0] = (a2 * acc + _dot(p2.astype(BF16), cn)) / l2


def _mla_decode(page_table, qlat, qr, ckv_new, kr_new, pool_ckv, pool_krt, j, pp, ns, t_new):
    db, rows, _ = qlat.shape
    npg = page_table.shape[1]
    pt = page_table.reshape(-1)
    npad = ckv_new.shape[1]
    page = lambda i: (lambda b, p, pt_: (j, pt_[b * npg + p * pp + i], 0, 0))
    per_b = lambda r, w: pl.BlockSpec((1, r, w), lambda b, p, pt_: (b, 0, 0))
    grid_spec = pltpu.PrefetchScalarGridSpec(
        num_scalar_prefetch=1,
        grid=(db, npg // pp),
        in_specs=[per_b(rows, KV_LORA), per_b(rows, QK_ROPE), per_b(npad, KV_LORA), per_b(npad, QK_ROPE)]
        + [pl.BlockSpec((None, None, PAGE, KV_LORA), page(i)) for i in range(pp)]
        + [pl.BlockSpec((None, None, QK_ROPE, PAGE), page(i)) for i in range(pp)],
        out_specs=per_b(rows, KV_LORA),
        scratch_shapes=[pltpu.VMEM((ns, rows, 1), F32), pltpu.VMEM((ns, rows, 1), F32),
                        pltpu.VMEM((ns, rows, KV_LORA), F32)],
    )
    return pl.pallas_call(
        functools.partial(_mla_decode_kernel, pp=pp, ns=ns, t_new=t_new),
        grid_spec=grid_spec,
        out_shape=jax.ShapeDtypeStruct((db, rows, KV_LORA), F32),
        compiler_params=_cp(("parallel", "arbitrary")),
        name="mla_decode",
    )(pt, qlat, qr, ckv_new, kr_new, *([pool_ckv] * pp), *([pool_krt] * pp))


def _head_matmul_kernel(x_ref, w_ref, o_ref):
    o_ref[...] = _dot(x_ref[...].astype(BF16), w_ref[...]).astype(o_ref.dtype)


def _head_matmul(x, w):
    h, n, k = x.shape
    m = w.shape[2]
    return pl.pallas_call(
        _head_matmul_kernel,
        grid=(h,),
        in_specs=[pl.BlockSpec((None, n, k), lambda i: (i, 0, 0)), pl.BlockSpec((None, k, m), lambda i: (i, 0, 0))],
        out_specs=pl.BlockSpec((None, n, m), lambda i: (i, 0, 0)),
        out_shape=jax.ShapeDtypeStruct((h, n, m), BF16),
        compiler_params=_cp(("parallel",)),
        name="head_matmul",
    )(x, w)


def _dsa_proj_kernel(x_ref, wq_ref, wk_ref, wv_ref, wvt_ref, wwt_ref, wqi_ref, wm_ref, wki_ref, lng_ref, lnb_ref,
                     sgn_ref, gperm_ref, brot_ref, cos_ref, sin_ref,
                     q_ref, k_ref, kf_ref, vf_ref, vt_ref, wt_ref, qi_ref, kw_ref, kib_ref, qm_ref):
    tm = x_ref.shape[0]
    xb = x_ref[...].astype(BF16)
    cos = cos_ref[...]
    sin = sin_ref[...]
    lane = lax.broadcasted_iota(I32, (tm, LANE), 1)
    low = lane < HEAD_DIM

    def roped(w_ref, nblk):
        y = _dot(xb, w_ref[...])
        return [y[:, b * LANE:(b + 1) * LANE] * cos + y[:, (nblk + b) * LANE:(nblk + b + 1) * LANE] * sin
                for b in range(nblk)]

    qb = roped(wq_ref, N_HEADS // 2)
    for h in range(N_HEADS):
        g, r = h // 3, h % 3
        blk = qb[(g // 2) * 3 + r] * DSA_QSCALE
        q_ref[h] = jnp.where(low if g % 2 == 0 else jnp.logical_not(low), blk, 0.0).astype(BF16)
    kb = roped(wk_ref, KV_HEADS // 2)
    for b in range(KV_HEADS // 2):
        kf_ref[:, b * LANE:(b + 1) * LANE] = kb[b]
        k_ref[b] = kb[b].astype(BF16)
    vf_ref[...] = _dot(xb, wv_ref[...])
    vt_ref[...] = _dot_nt(wvt_ref[...], xb).astype(BF16)
    wt_ref[...] = _dot_nt(wwt_ref[...], xb)
    qib = roped(wqi_ref, IDX_HEADS // 2)
    for h in range(IDX_HEADS):
        qi_ref[h] = jnp.where(low if h % 2 == 0 else jnp.logical_not(low), qib[h // 2], 0.0).astype(BF16)
    qm_ref[...] = (_dot(xb, wm_ref[...]) * HEAD_SCALE).astype(BF16)
    yk = _dot(xb, wki_ref[...])
    c, c_rot, wpad = yk[:, :LANE], yk[:, LANE:2 * LANE], yk[:, 2 * LANE:3 * LANE]
    mu = jnp.sum(jnp.where(low, c, 0.0), axis=1, keepdims=True) * (1.0 / IDX_DIM)
    xc = c - mu
    var = jnp.sum(jnp.where(low, xc * xc, 0.0), axis=1, keepdims=True) * (1.0 / IDX_DIM)
    rstd = lax.rsqrt(var + LN_EPS)
    ki = xc * rstd * lng_ref[...] + lnb_ref[...]
    ki_rot = (c_rot - sgn_ref[...] * mu) * rstd * gperm_ref[...] + brot_ref[...]
    kir = ki * cos + ki_rot * sin
    kib_ref[...] = kir.astype(BF16)
    kw_ref[...] = jnp.where(low, kir, wpad)


def _dsa_proj(x, w_in, ln_g, ln_b, cos, sin, tm):
    n = x.shape[0]
    o = 0
    cols = []
    for wdt in (N_HEADS * HEAD_DIM, KV_HEADS * HEAD_DIM, KV_HEADS * HEAD_DIM, IDX_HEADS * IDX_DIM, IDX_DIM, IDX_HEADS,
                MEM_HEADS * HEAD_DIM):
        cols.append(w_in[:, o:o + wdt])
        o += wdt
    w_q, w_k, w_v, w_qi, w_ki, w_w, w_qm = cols
    perm = jnp.asarray(DSA_PERM)
    w_qp = w_q.reshape(D_MODEL, N_HEADS, HEAD_DIM)[:, perm].reshape(D_MODEL, -1)
    wq = jnp.concatenate([w_qp, _rot_heads(w_qp, N_HEADS, HEAD_DIM, ROT_DIM)], 1).astype(BF16)
    wk = jnp.concatenate([w_k, _rot_heads(w_k, KV_HEADS, HEAD_DIM, ROT_DIM)], 1).astype(BF16)
    wqi = jnp.concatenate([w_qi, _rot_heads(w_qi, IDX_HEADS, IDX_DIM, ROT_DIM)], 1).astype(BF16)
    w_ki_rot = _rot_heads(w_ki, 1, IDX_DIM, ROT_DIM)
    w_wpad = jnp.concatenate([jnp.zeros((D_MODEL, HEAD_DIM), F32), w_w,
                              jnp.zeros((D_MODEL, LANE - HEAD_DIM - IDX_HEADS), F32)], 1)
    wki = jnp.concatenate([w_ki, w_ki, w_ki_rot, w_ki_rot, w_wpad], 1).astype(BF16)
    wv, wm = w_v.astype(BF16), w_qm.astype(BF16)
    wvt = w_v.T.astype(BF16)
    wwt = jnp.concatenate([w_w.T, jnp.zeros((2 * SUBLANE - IDX_HEADS, D_MODEL), F32)], 0).astype(BF16)
    half = ROT_DIM // 2
    zr = jnp.zeros((IDX_DIM - ROT_DIM,), F32)
    dup = lambda a: jnp.concatenate([a, a]).reshape(1, LANE)
    sgn = dup(jnp.concatenate([-jnp.ones((half,), F32), jnp.ones((half,), F32), zr]))
    gperm = dup(jnp.concatenate([ln_g[half:ROT_DIM], ln_g[:half], zr]))
    brot = dup(jnp.concatenate([-ln_b[half:ROT_DIM], ln_b[:half], zr]))
    lng, lnb = dup(ln_g), dup(ln_b)
    nt = cos.shape[0] // tm
    full = lambda a: pl.BlockSpec(a.shape, lambda i: (0,) * a.ndim)
    rows = lambda w: pl.BlockSpec((tm, w), lambda i: (i, 0))
    heads = lambda h: pl.BlockSpec((h, tm, LANE), lambda i: (0, i, 0))
    colsp = lambda r: pl.BlockSpec((r, tm), lambda i: (0, i))
    tab = pl.BlockSpec((tm, LANE), lambda i: (i % nt, 0))
    sds = jax.ShapeDtypeStruct
    return pl.pallas_call(
        _dsa_proj_kernel,
        grid=(n // tm,),
        in_specs=[rows(D_MODEL), full(wq), full(wk), full(wv), full(wvt), full(wwt), full(wqi), full(wm), full(wki),
                  full(lng), full(lnb), full(sgn), full(gperm), full(brot), tab, tab],
        out_specs=[heads(N_HEADS), heads(KV_HEADS // 2), rows(2 * LANE), rows(2 * LANE), colsp(2 * LANE),
                   colsp(2 * SUBLANE), heads(IDX_HEADS), rows(LANE), rows(LANE), rows(2 * LANE)],
        out_shape=[sds((N_HEADS, n, LANE), BF16), sds((KV_HEADS // 2, n, LANE), BF16), sds((n, 2 * LANE), F32),
                   sds((n, 2 * LANE), F32), sds((2 * LANE, n), BF16), sds((2 * SUBLANE, n), F32),
                   sds((IDX_HEADS, n, LANE), BF16), sds((n, LANE), F32), sds((n, LANE), BF16),
                   sds((n, 2 * LANE), BF16)],
        compiler_params=_cp(("parallel",), 56),
        name="dsa_proj",
    )(x, wq, wk, wv, wvt, wwt, wqi, wm, wki, lng, lnb, sgn, gperm, brot, cos, sin)


def _indexer_prompt_kernel(qt_ref, kt_ref, qi_ref, ki_ref, wt_ref, sc_ref, *, tq, tk):
    qi = qt_ref[pl.program_id(1)]
    ki = kt_ref[pl.program_id(1)]
    kb = ki_ref[...]
    w = wt_ref[...]
    acc = None
    for h in range(IDX_HEADS):
        d = jnp.maximum(_dot_nt(kb, qi_ref[h]), 0.0) * w[h:h + 1]
        acc = d if acc is None else acc + d
    key = ki * tk + lax.broadcasted_iota(I32, (tk, tq), 0)
    qry = qi * tq + lax.broadcasted_iota(I32, (tk, tq), 1)
    sc_ref[...] = jnp.where(key <= qry, acc, NEG)


def _indexer_prompt(qidx, kib, wt, batch, tq, tk):
    n = kib.shape[0]
    s = n // batch
    nq, nk = s // tq, s // tk
    qtab, ktab = _causal_steps(nq, tq, tk)
    grid_spec = pltpu.PrefetchScalarGridSpec(
        num_scalar_prefetch=2,
        grid=(batch, qtab.shape[0]),
        in_specs=[pl.BlockSpec((IDX_HEADS, tq, LANE), lambda b, t, qt, kt: (0, b * nq + qt[t], 0)),
                  pl.BlockSpec((tk, LANE), lambda b, t, qt, kt: (b * nk + kt[t], 0)),
                  pl.BlockSpec((2 * SUBLANE, tq), lambda b, t, qt, kt: (0, b * nq + qt[t]))],
        out_specs=pl.BlockSpec((tk, tq), lambda b, t, qt, kt: (b * nk + kt[t], qt[t])),
    )
    return pl.pallas_call(
        functools.partial(_indexer_prompt_kernel, tq=tq, tk=tk),
        grid_spec=grid_spec,
        out_shape=jax.ShapeDtypeStruct((n, s), F32),
        compiler_params=_cp(("parallel", "arbitrary")),
        name="indexer_prompt",
    )(qtab, ktab, qidx, kib, wt)


def _indexer_sample_kernel(pt_ref, q_ref, w_ref, kn_ref, *rest, pp, t_new):
    kt_refs = rest[:pp]
    sc_ref, scn_ref = rest[pp:]
    q = q_ref[0]
    w = w_ref[0]

    def weighted(d):
        acc = None
        for h in range(IDX_HEADS):
            t = d[h * t_new:(h + 1) * t_new] * w[:, h:h + 1]
            acc = t if acc is None else acc + t
        return acc

    for i in range(pp):
        sc_ref[0, :, i * PAGE:(i + 1) * PAGE] = weighted(jnp.maximum(_dot(q, kt_refs[i][...].astype(BF16)), 0.0))

    @pl.when(pl.program_id(1) == pl.num_programs(1) - 1)
    def _():
        sn = weighted(jnp.maximum(_dot_nt(q, kn_ref[0].astype(BF16)), 0.0))
        npad = sn.shape[1]
        t = lax.broadcasted_iota(I32, (t_new, npad), 0)
        u = lax.broadcasted_iota(I32, (t_new, npad), 1)
        scn_ref[0] = jnp.full((t_new, LANE), NEG, F32)
        scn_ref[0, :, :npad] = jnp.where(u <= t, sn, NEG)


def _indexer_sample(page_table, q, w, kidx_new, pool_kidxt, j, pp, t_new):
    db = q.shape[0]
    npg = page_table.shape[1]
    pt = page_table.reshape(-1)
    npad = kidx_new.shape[1]
    per_b = lambda r, c: pl.BlockSpec((1, r, c), lambda b, p, pt_: (b, 0, 0))
    pool = lambda i: pl.BlockSpec((None, None, IDX_DIM, PAGE), lambda b, p, pt_: (j, pt_[b * npg + p * pp + i], 0, 0))
    grid_spec = pltpu.PrefetchScalarGridSpec(
        num_scalar_prefetch=1,
        grid=(db, npg // pp),
        in_specs=[per_b(IDX_HEADS * t_new, IDX_DIM), per_b(t_new, IDX_HEADS), per_b(npad, IDX_DIM)]
        + [pool(i) for i in range(pp)],
        out_specs=[pl.BlockSpec((1, t_new, pp * PAGE), lambda b, p, pt_: (b, 0, p)), per_b(t_new, LANE)],
    )
    return pl.pallas_call(
        functools.partial(_indexer_sample_kernel, pp=pp, t_new=t_new),
        grid_spec=grid_spec,
        out_shape=[jax.ShapeDtypeStruct((db, t_new, npg * PAGE), F32), jax.ShapeDtypeStruct((db, t_new, LANE), F32)],
        compiler_params=_cp(("parallel", "arbitrary")),
        name="indexer_sample",
    )(pt, q, w, kidx_new, *([pool_kidxt] * pp))


def _select_rows_kernel(sc_ref, thr_ref, cut_ref, key_ref, *, topk, cw, tq, idx_bits):
    nch = sc_ref.shape[1] // cw

    def make_keys(c, carry):
        off = pl.multiple_of(c * cw, cw)
        key_ref[:, pl.ds(off, cw)] = _score_keys(sc_ref[:, pl.ds(off, cw)])
        return carry

    lax.fori_loop(0, nch, make_keys, 0)
    lane = lax.broadcasted_iota(I32, (tq, LANE), 1)

    def count(pred):
        def body(c, acc):
            off = pl.multiple_of(c * cw, cw)
            for u in range(cw // LANE):
                kk = key_ref[:, pl.ds(off + u * LANE, LANE)]
                acc = acc + jnp.where(pred(kk, off + u * LANE + lane), 1.0, 0.0)
            return acc
        acc = lax.fori_loop(0, nch, body, jnp.zeros((tq, LANE), F32))
        return jnp.sum(acc, axis=1, keepdims=True)

    thrk, cut = _threshold_search(count, (tq, 1), topk, idx_bits)
    thr_ref[...] = _key_to_score(thrk, (tq, LANE))
    cut_ref[...] = jnp.broadcast_to(cut, (tq, LANE))


def _threshold_search(count, shape, topk, idx_bits):
    def value_bit(it, ans):
        cand = ans | jnp.left_shift(jnp.int32(1), 31 - it)
        cs = cand ^ MIN_I32
        return jnp.where(count(lambda kk, idx: kk >= cs) >= topk, cand, ans)

    thrk = lax.fori_loop(0, 32, value_bit, jnp.zeros(shape, I32)) ^ MIN_I32
    n_gt = count(lambda kk, idx: kk > thrk)
    n_ge = count(lambda kk, idx: kk >= thrk)
    need = topk - n_gt

    def tie_bit(it, m):
        cand = m | jnp.left_shift(jnp.int32(1), idx_bits - 1 - it)
        cnt = count(lambda kk, idx: jnp.where(kk == thrk, idx, cand) < cand)
        return jnp.where(cnt < need, cand, m)

    any_tie = jnp.max(jnp.abs(n_ge - topk)) > 0.0
    cut = lax.cond(any_tie, lambda: lax.fori_loop(0, idx_bits, tie_bit, jnp.zeros(shape, I32)),
                   lambda: jnp.full(shape, 2 ** 30, I32))
    return thrk, cut


def _select_rows(sc, topk, cw, tq):
    n, length = sc.shape
    idx_bits = max(1, (length - 1).bit_length())
    return pl.pallas_call(
        functools.partial(_select_rows_kernel, topk=topk, cw=cw, tq=tq, idx_bits=idx_bits),
        grid=(n // tq,),
        in_specs=[pl.BlockSpec((tq, length), lambda i: (i, 0))],
        out_specs=[pl.BlockSpec((tq, LANE), lambda i: (i, 0)), pl.BlockSpec((tq, LANE), lambda i: (i, 0))],
        out_shape=[jax.ShapeDtypeStruct((n, LANE), F32), jax.ShapeDtypeStruct((n, LANE), I32)],
        scratch_shapes=[pltpu.VMEM((tq, length), I32)],
        compiler_params=_cp(("parallel",)),
        name="topk_select_rows",
    )(sc)


def _select_cols_kernel(sc_ref, thr_ref, cut_ref, key_ref, *, topk, cw, idx_bits, nqt):
    qt = pl.program_id(0) % nqt
    nch = (jnp.maximum((qt + 1) * LANE, topk) + cw - 1) // cw

    def make_keys(c, carry):
        off = pl.multiple_of(c * cw, cw)
        key_ref[pl.ds(off, cw), :] = _score_keys(sc_ref[pl.ds(off, cw), :])
        return carry

    lax.fori_loop(0, nch, make_keys, 0)
    sub = lax.broadcasted_iota(I32, (SUBLANE, LANE), 0)

    def count(pred):
        def body(c, accs):
            off = pl.multiple_of(c * cw, cw)
            accs = list(accs)
            for u in range(cw // SUBLANE):
                kk = key_ref[pl.ds(off + u * SUBLANE, SUBLANE), :]
                accs[u % N_ACC] = accs[u % N_ACC] + jnp.where(pred(kk, off + u * SUBLANE + sub), 1.0, 0.0)
            return tuple(accs)
        accs = lax.fori_loop(0, nch, body, tuple(jnp.zeros((SUBLANE, LANE), F32) for _ in range(N_ACC)))
        return jnp.sum(functools.reduce(lambda a, b: a + b, accs), axis=0, keepdims=True)

    thrk, cut = _threshold_search(count, (1, LANE), topk, idx_bits)
    thr_ref[...] = _key_to_score(thrk, (SUBLANE, LANE))
    cut_ref[...] = jnp.broadcast_to(cut, (SUBLANE, LANE))


def _select_cols(sct, batch, topk, cw):
    n, s = sct.shape
    nqt = s // LANE
    idx_bits = max(1, (s - 1).bit_length())
    out = pl.BlockSpec((SUBLANE, LANE), lambda i: (0, i))
    return pl.pallas_call(
        functools.partial(_select_cols_kernel, topk=topk, cw=cw, idx_bits=idx_bits, nqt=nqt),
        grid=(batch * nqt,),
        in_specs=[pl.BlockSpec((s, LANE), lambda i: (i // nqt, i % nqt))],
        out_specs=[out, out],
        out_shape=[jax.ShapeDtypeStruct((SUBLANE, n), F32), jax.ShapeDtypeStruct((SUBLANE, n), I32)],
        scratch_shapes=[pltpu.VMEM((s, LANE), I32)],
        compiler_params=_cp(("parallel",)),
        name="topk_select_cols",
    )(sct)


def _dsa_attn_kernel(qt_ref, kt_ref, q_ref, k_ref, vt_ref, sc_ref, thr_ref, cut_ref, o_ref, m_ref, l_ref, acc_ref,
                     *, tq, tk):
    qi = qt_ref[pl.program_id(1)]
    ki = kt_ref[pl.program_id(1)]
    last = ((qi + 1) * tq - 1) // tk

    @pl.when(ki == 0)
    def _():
        m_ref[...] = jnp.full(m_ref.shape, NEG, F32)
        l_ref[...] = jnp.zeros(l_ref.shape, F32)
        acc_ref[...] = jnp.zeros(acc_ref.shape, F32)

    key = ki * tk + lax.broadcasted_iota(I32, (tk, tq), 0)
    qry = qi * tq + lax.broadcasted_iota(I32, (tk, tq), 1)
    sc = sc_ref[...]
    thr = thr_ref[0:1, :]
    cut = cut_ref[0:1, :]
    sel = jnp.where(sc > thr, 1.0, jnp.where(sc == thr, jnp.where(key <= cut, 1.0, 0.0), 0.0))
    keep = jnp.where(key <= qry, sel, 0.0)
    keep3 = jnp.concatenate([keep, keep, keep], axis=1)
    ss = [_dot_nt(k_ref[g // 2], q_ref[3 * g:3 * g + 3].reshape(3 * tq, LANE)) for g in range(KV_HEADS)]
    for g in range(KV_HEADS):
        s = jnp.where(keep3 > 0.0, ss[g], NEG)
        m_new, a, p, l_new = _softmax_step(s, m_ref[g], l_ref[g], axis=0)
        acc_ref[g] = a * acc_ref[g] + _dot(vt_ref[g * HEAD_DIM:(g + 1) * HEAD_DIM, :], p.astype(BF16))
        m_ref[g] = m_new
        l_ref[g] = l_new

    @pl.when(ki == last)
    def _():
        for gp in range(KV_HEADS // 2):
            oe = acc_ref[2 * gp] / l_ref[2 * gp]
            oo = acc_ref[2 * gp + 1] / l_ref[2 * gp + 1]
            for r in range(3):
                blk = gp * 3 + r
                ot = jnp.concatenate([oe[:, r * tq:(r + 1) * tq], oo[:, r * tq:(r + 1) * tq]], axis=0)
                o_ref[:, blk * LANE:(blk + 1) * LANE] = ot.T.astype(o_ref.dtype)


def _dsa_attn(q, k, vt, sct, thr, cut, batch, tq, tk):
    n = q.shape[1]
    s = n // batch
    nq, nk = s // tq, s // tk
    qtab, ktab = _causal_steps(nq, tq, tk)
    qcol = pl.BlockSpec((SUBLANE, tq), lambda b, t, qt, kt: (0, b * nq + qt[t]))
    grid_spec = pltpu.PrefetchScalarGridSpec(
        num_scalar_prefetch=2,
        grid=(batch, qtab.shape[0]),
        in_specs=[pl.BlockSpec((N_HEADS, tq, LANE), lambda b, t, qt, kt: (0, b * nq + qt[t], 0)),
                  pl.BlockSpec((KV_HEADS // 2, tk, LANE), lambda b, t, qt, kt: (0, b * nk + kt[t], 0)),
                  pl.BlockSpec((KV_HEADS * HEAD_DIM, tk), lambda b, t, qt, kt: (0, b * nk + kt[t])),
                  pl.BlockSpec((tk, tq), lambda b, t, qt, kt: (b * nk + kt[t], qt[t])),
                  qcol, qcol],
        out_specs=pl.BlockSpec((tq, N_HEADS * HEAD_DIM), lambda b, t, qt, kt: (b * nq + qt[t], 0)),
        scratch_shapes=[pltpu.VMEM((KV_HEADS, 1, 3 * tq), F32), pltpu.VMEM((KV_HEADS, 1, 3 * tq), F32),
                        pltpu.VMEM((KV_HEADS, HEAD_DIM, 3 * tq), F32)],
    )
    return pl.pallas_call(
        functools.partial(_dsa_attn_kernel, tq=tq, tk=tk),
        grid_spec=grid_spec,
        out_shape=jax.ShapeDtypeStruct((n, N_HEADS * HEAD_DIM), BF16),
        compiler_params=_cp(("parallel", "arbitrary"), 56),
        name="dsa_attn",
    )(qtab, ktab, q, k, vt, sct, thr, cut)


def _dsa_decode_kernel(pt_ref, q_ref, sc_ref, scn_ref, thr_ref, cut_ref, knt_ref, vn_ref, *rest, pp, ns, t_new, past):
    kt_refs, vt_refs = rest[:pp], rest[pp:2 * pp]
    o_ref, m_ref, l_ref, acc_ref = rest[2 * pp:]
    p_id = pl.program_id(1)
    rows = q_ref.shape[1]
    reps = rows // t_new
    per = pp // ns
    width = per * PAGE

    @pl.when(p_id == 0)
    def _():
        m_ref[...] = jnp.full(m_ref.shape, NEG, F32)
        l_ref[...] = jnp.zeros(l_ref.shape, F32)
        acc_ref[...] = jnp.zeros(acc_ref.shape, F32)

    thr = thr_ref[0][:, :1]
    cut = cut_ref[0][:, :1]
    q = q_ref[0]

    def keep_rows(sc, col, extra):
        sel = jnp.where(sc > thr, 1.0, jnp.where(sc == thr, jnp.where(col <= cut, 1.0, 0.0), 0.0))
        if extra is not None:
            sel = jnp.where(extra, sel, 0.0)
        return jnp.concatenate([sel] * reps, axis=0)

    for st in range(ns):
        pages = range(st * per, (st + 1) * per)
        col = p_id * (pp * PAGE) + st * width + lax.broadcasted_iota(I32, (t_new, width), 1)
        keep = keep_rows(sc_ref[0, :, st * width:(st + 1) * width], col, None)
        s = jnp.concatenate([_dot(q, kt_refs[i][...].astype(BF16)) for i in pages], axis=1)
        s = jnp.where(keep > 0.0, s, NEG)
        m_new, a, p, l_new = _softmax_step(s, m_ref[st], l_ref[st])
        pb = p.astype(BF16)
        pv = None
        for n_, i in enumerate(pages):
            t = _dot_nt(pb[:, n_ * PAGE:(n_ + 1) * PAGE], vt_refs[i][...].astype(BF16))
            pv = t if pv is None else pv + t
        acc_ref[st] = a * acc_ref[st] + pv
        m_ref[st] = m_new
        l_ref[st] = l_new

    @pl.when(p_id == pl.num_programs(1) - 1)
    def _():
        m, l, acc = _merge_streams([m_ref[st] for st in range(ns)], [l_ref[st] for st in range(ns)],
                                   [acc_ref[st] for st in range(ns)])
        npad = vn_ref.shape[1]
        t = lax.broadcasted_iota(I32, (t_new, npad), 0)
        u = lax.broadcasted_iota(I32, (t_new, npad), 1)
        keep_n = keep_rows(scn_ref[0][:, :npad], past + u, u <= t)
        s2 = jnp.where(keep_n > 0.0, _dot(q, knt_ref[0].astype(BF16)), NEG)
        m2, a2, p2, l2 = _softmax_step(s2, m, l)
        o_ref[0] = (a2 * acc + _dot(p2.astype(BF16), vn_ref[0].astype(BF16))) / l2


def _dsa_decode(page_table, q, sc, scn, thr, cut, knt, v_new, pool_kt, pool_vt, j, pp, ns, t_new):
    db, rows, wd = q.shape
    npg = page_table.shape[1]
    pt = page_table.reshape(-1)
    npad = v_new.shape[1]
    per_b = lambda r, c: pl.BlockSpec((1, r, c), lambda b, p, pt_: (b, 0, 0))
    pool = lambda i: pl.BlockSpec((None, None, wd, PAGE), lambda b, p, pt_: (j, pt_[b * npg + p * pp + i], 0, 0))
    grid_spec = pltpu.PrefetchScalarGridSpec(
        num_scalar_prefetch=1,
        grid=(db, npg // pp),
        in_specs=[per_b(rows, wd), pl.BlockSpec((1, t_new, pp * PAGE), lambda b, p, pt_: (b, 0, p)),
                  per_b(t_new, LANE), per_b(t_new, LANE), per_b(t_new, LANE), per_b(wd, npad), per_b(npad, wd)]
        + [pool(i) for i in range(pp)] + [pool(i) for i in range(pp)],
        out_specs=per_b(rows, wd),
        scratch_shapes=[pltpu.VMEM((ns, rows, 1), F32), pltpu.VMEM((ns, rows, 1), F32),
                        pltpu.VMEM((ns, rows, wd), F32)],
    )
    return pl.pallas_call(
        functools.partial(_dsa_decode_kernel, pp=pp, ns=ns, t_new=t_new, past=npg * PAGE),
        grid_spec=grid_spec,
        out_shape=jax.ShapeDtypeStruct((db, rows, wd), F32),
        compiler_params=_cp(("parallel", "arbitrary")),
        name="dsa_decode",
    )(pt, q, sc, scn, thr, cut, knt, v_new, *([pool_kt] * pp), *([pool_vt] * pp))


def _mem_kv_kernel(m_ref, w_ref, kf_ref, vf_ref, kb_ref, vb_ref):
    y = _dot(m_ref[...].astype(BF16), w_ref[...])
    w = MEM_HEADS * HEAD_DIM
    kf_ref[...] = y[:, :w]
    vf_ref[...] = y[:, w:]
    kb_ref[...] = y[:, :w].astype(BF16)
    vb_ref[...] = y[:, w:].astype(BF16)


def _mem_kv(mem2d, w):
    n = mem2d.shape[0]
    wd = MEM_HEADS * HEAD_DIM
    wb = w.astype(BF16)
    spec = pl.BlockSpec((n, wd), lambda i: (0, 0))
    return pl.pallas_call(
        _mem_kv_kernel,
        grid=(1,),
        in_specs=[pl.BlockSpec(mem2d.shape, lambda i: (0, 0)), pl.BlockSpec(wb.shape, lambda i: (0, 0))],
        out_specs=[spec] * 4,
        out_shape=[jax.ShapeDtypeStruct((n, wd), F32)] * 2 + [jax.ShapeDtypeStruct((n, wd), BF16)] * 2,
        compiler_params=_cp(("arbitrary",)),
        name="mem_kv",
    )(mem2d, wb)


def _mem_attend_kernel(q_ref, mk_ref, mv_ref, o_ref, *, bb, kv_t):
    tt = q_ref.shape[1]
    lane = lax.broadcasted_iota(I32, (tt, LANE), 1)
    low = lane < HEAD_DIM
    for b in range(bb):
        for blk in range(MEM_HEADS // 2):
            sl = slice(blk * LANE, (blk + 1) * LANE)
            qb = q_ref[b, :, sl]
            mkb = (mk_ref[b, sl, :] if kv_t else mk_ref[b, :, sl]).astype(BF16)
            mvb = (mv_ref[b, sl, :] if kv_t else mv_ref[b, :, sl]).astype(BF16)
            res = []
            for half in range(2):
                qh = jnp.where(low if half == 0 else jnp.logical_not(low), qb, jnp.zeros_like(qb))
                s = _dot(qh, mkb) if kv_t else _dot_nt(qh, mkb)
                p = jnp.exp(s - jnp.max(s, axis=1, keepdims=True))
                p = (p / jnp.sum(p, axis=1, keepdims=True)).astype(BF16)
                res.append(_dot_nt(p, mvb) if kv_t else _dot(p, mvb))
            o_ref[b, :, sl] = jnp.where(low, res[0], res[1]).astype(o_ref.dtype)


def _mem_attend(q, mk, mv, bb, tt, kv_t):
    bm, t, wd = q.shape
    kv = pl.BlockSpec((bb,) + mk.shape[1:], lambda b, i: (b, 0, 0))
    return pl.pallas_call(
        functools.partial(_mem_attend_kernel, bb=bb, kv_t=kv_t),
        grid=(bm // bb, t // tt),
        in_specs=[pl.BlockSpec((bb, tt, wd), lambda b, i: (b, i, 0)), kv, kv],
        out_specs=pl.BlockSpec((bb, tt, wd), lambda b, i: (b, i, 0)),
        out_shape=jax.ShapeDtypeStruct((bm, t, wd), BF16),
        compiler_params=_cp(("parallel", "parallel")),
        name="mem_attend",
    )(q, mk, mv)


def _split_bf16(x):
    hi = x.astype(BF16)
    return hi, (x - hi.astype(F32)).astype(BF16)


def _outproj_router_kernel(x_ref, mix_ref, mem_ref, wo1_ref, wo2_ref, g_ref, b_ref, wrh_ref, wrl_ref, rb_ref,
                           x1_ref, gate_ref):
    y = ALPHA * x_ref[...] + _dot(mix_ref[...].astype(BF16), wo1_ref[...]) + _dot(mem_ref[...], wo2_ref[...])
    x1 = _ln(y, g_ref[...], b_ref[...])
    x1_ref[...] = x1
    xh, xl = _split_bf16(x1)
    logit = _dot_nt(wrh_ref[...], xh) + _dot_nt(wrh_ref[...], xl) + _dot_nt(wrl_ref[...], xh)
    s = 1.0 / (1.0 + jnp.exp(-logit))
    sb = s + rb_ref[...]
    epg = N_EXPERTS // N_GROUPS
    rows = [sb[e:e + 1] for e in range(N_EXPERTS)]
    gscore = []
    for g in range(N_GROUPS):
        v = rows[g * epg:(g + 1) * epg]
        best = None
        for a in range(epg):
            for b in range(a + 1, epg):
                pr = v[a] + v[b]
                best = pr if best is None else jnp.maximum(best, pr)
        gscore.append(best)
    gmax = functools.reduce(jnp.maximum, gscore)
    taken = jnp.zeros_like(gmax)
    gsel = []
    for g in range(N_GROUPS):
        pick = jnp.where(gscore[g] == gmax, 1.0, 0.0) * (1.0 - taken)
        taken = taken + pick
        gsel.append(pick)
    val = [jnp.where(gsel[e // epg] > 0.0, rows[e], NEG) for e in range(N_EXPERTS)]
    chosen = [jnp.zeros_like(gmax) for _ in range(N_EXPERTS)]
    for _ in range(2):
        vmax = functools.reduce(jnp.maximum, val)
        taken = jnp.zeros_like(gmax)
        for e in range(N_EXPERTS):
            pick = jnp.where(val[e] == vmax, 1.0, 0.0) * (1.0 - taken)
            taken = taken + pick
            chosen[e] = chosen[e] + pick
            val[e] = jnp.where(pick > 0.0, 2.0 * NEG, val[e])
    wsel = [jnp.where(chosen[e] > 0.0, s[e:e + 1], 0.0) for e in range(N_EXPERTS)]
    wsum = functools.reduce(lambda a, b: a + b, wsel)
    gate_ref[...] = jnp.concatenate([w / wsum for w in wsel], axis=0)


def _outproj_router(x, mix, mem, wo1, wo2, g, b, wrh, wrl, rb, tm):
    n = x.shape[0]
    full = lambda a: pl.BlockSpec(a.shape, lambda i: (0,) * a.ndim)
    rows = lambda w: pl.BlockSpec((tm, w), lambda i: (i, 0))
    return pl.pallas_call(
        _outproj_router_kernel,
        grid=(n // tm,),
        in_specs=[rows(D_MODEL), rows(mix.shape[1]), rows(mem.shape[1]), full(wo1), full(wo2), full(g), full(b),
                  full(wrh), full(wrl), full(rb)],
        out_specs=[rows(D_MODEL), pl.BlockSpec((N_EXPERTS, tm), lambda i: (0, i))],
        out_shape=[jax.ShapeDtypeStruct((n, D_MODEL), F32), jax.ShapeDtypeStruct((N_EXPERTS, n), F32)],
        compiler_params=_cp(("parallel",)),
        name="outproj_router",
    )(x, mix, mem, wo1, wo2, g, b, wrh, wrl, rb)


def _moe_kernel(x_ref, gate_ref, wgu_ref, wd_ref, g_ref, b_ref, o_ref, acc_ref, xb_ref):
    e = pl.program_id(1)

    @pl.when(e == 0)
    def _():
        acc_ref[...] = jnp.zeros(acc_ref.shape, F32)
        xb_ref[...] = x_ref[...].astype(BF16)

    gate = gate_ref[...]
    lane = lax.broadcasted_iota(I32, gate.shape, 1)
    gcol = jnp.sum(jnp.where(lane == e, gate, 0.0), axis=1, keepdims=True)
    hgu = _dot(xb_ref[...], wgu_ref[...])
    hg, hu = hgu[:, :D_EXPERT], hgu[:, D_EXPERT:]
    h = hg * (1.0 / (1.0 + jnp.exp(-hg))) * hu * gcol
    acc_ref[...] += _dot(h.astype(BF16), wd_ref[...])

    @pl.when(e == pl.num_programs(1) - 1)
    def _():
        o_ref[...] = _ln(ALPHA * x_ref[...] + acc_ref[...], g_ref[...], b_ref[...])


def _moe(x, gate, wgu, wd, g, b, tm):
    n = x.shape[0]
    full = lambda a: pl.BlockSpec(a.shape, lambda i, e: (0,) * a.ndim)
    return pl.pallas_call(
        _moe_kernel,
        grid=(n // tm, N_EXPERTS),
        in_specs=[pl.BlockSpec((tm, D_MODEL), lambda i, e: (i, 0)), pl.BlockSpec((tm, N_EXPERTS), lambda i, e: (i, 0)),
                  pl.BlockSpec((None, D_MODEL, 2 * D_EXPERT), lambda i, e: (e, 0, 0)),
                  pl.BlockSpec((None, D_EXPERT, D_MODEL), lambda i, e: (e, 0, 0)), full(g), full(b)],
        out_specs=pl.BlockSpec((tm, D_MODEL), lambda i, e: (i, 0)),
        out_shape=jax.ShapeDtypeStruct((n, D_MODEL), F32),
        scratch_shapes=[pltpu.VMEM((tm, D_MODEL), F32), pltpu.VMEM((tm, D_MODEL), BF16)],
        compiler_params=_cp(("parallel", "arbitrary")),
        name="moe",
    )(x, gate, wgu, wd, g, b)


def _tile(n, pref):
    t = min(n, pref)
    while n % t:
        t //= 2
    return t


def kernel(x_prompt, x_sample, cache_mla_ckv, cache_mla_krope, cache_dsa_k, cache_dsa_v, cache_dsa_kidx, cache_mem_k, cache_mem_v, page_table, mem_prompt, w_in_mla, mla_q_norm, mla_kv_norm, w_uq, w_uk, w_uv, w_in_dsa, idx_k_norm_g, idx_k_norm_b, w_mem_kv, w_out, ln1_g, ln1_b, ln2_g, ln2_b, w_router, router_bias, w_gate, w_up, w_down):
    B, S, _ = x_prompt.shape
    DB, T, _ = x_sample.shape
    n_mem = mem_prompt.shape[1]
    npg = page_table.shape[1]
    past = npg * PAGE
    n_pool = cache_mla_ckv.shape[1]
    NP, NS = B * S, DB * T
    TPAD = 16
    pp_mla = 16 if npg % 16 == 0 else npg
    pp_dsa = 8 if npg % 8 == 0 else npg
    ns_mla = 2 if pp_mla % 2 == 0 else 1
    ns_dsa = 2 if pp_dsa % 2 == 0 else 1
    tm_p, tm_s = _tile(NP, 512), _tile(NS, 512)
    t_att = _tile(S, 512)
    KVW = KV_HEADS * HEAD_DIM

    pos_p = jnp.arange(S, dtype=jnp.int32)
    pos_s = jnp.tile(past + jnp.arange(T, dtype=jnp.int32), DB)
    cos_mp, sin_mp = _rope_tables(pos_p, QK_ROPE, LANE, QK_NOPE, LANE)
    cos_ms, sin_ms = _rope_tables(pos_s, QK_ROPE, LANE, QK_NOPE, LANE)
    cos_mr, sin_mr = _rope_tables(pos_s, QK_ROPE, QK_ROPE, 0, N_HEADS * QK_ROPE)
    cos_dp, sin_dp = _rope_tables(pos_p, ROT_DIM, HEAD_DIM, 0, LANE)
    cos_ds, sin_ds = _rope_tables(pos_s, ROT_DIM, HEAD_DIM, 0, LANE)

    wrt = w_router.T
    wrh = wrt.astype(BF16)
    wrl = (wrt - wrh.astype(F32)).astype(BF16)
    rb = router_bias.reshape(N_EXPERTS, 1).astype(F32)
    perm = jnp.asarray(DSA_PERM)
    pool_kt = jnp.transpose(cache_dsa_k, (0, 1, 3, 4, 2)).reshape(-1, n_pool, KVW, PAGE)
    pool_vt = jnp.transpose(cache_dsa_v, (0, 1, 3, 4, 2)).reshape(-1, n_pool, KVW, PAGE)
    pool_kidxt = jnp.transpose(cache_dsa_kidx, (0, 1, 3, 2))
    pool_krt = jnp.transpose(cache_mla_krope, (0, 1, 3, 2))
    mem2d = mem_prompt.reshape(B * n_mem, D_MODEL)

    def pad_new(a):
        a = a.reshape(DB, T, a.shape[-1])
        return jnp.concatenate([a, jnp.zeros((DB, TPAD - T, a.shape[-1]), a.dtype)], axis=1)

    def post(x, mix, mem, i, wo_rows, tm):
        wo = w_out[i]
        wo1 = wo[:N_HEADS * HEAD_DIM]
        if wo_rows is not None:
            wo1 = wo1.reshape(N_HEADS, HEAD_DIM, D_MODEL)[wo_rows].reshape(N_HEADS * HEAD_DIM, D_MODEL)
        x1, gate_t = _outproj_router(x, mix, mem, wo1.astype(BF16), wo[N_HEADS * HEAD_DIM:].astype(BF16),
                                     ln1_g[i].reshape(1, -1), ln1_b[i].reshape(1, -1), wrh, wrl, rb, tm)
        wgu = jnp.concatenate([w_gate[i], w_up[i]], axis=-1).astype(BF16)
        return _moe(x1, gate_t.T, wgu, w_down[i].astype(BF16), ln2_g[i].reshape(1, -1), ln2_b[i].reshape(1, -1),
                    _tile(x.shape[0], 1024))

    xp = x_prompt.reshape(NP, D_MODEL)
    xs = x_sample.reshape(NS, D_MODEL)
    ckv_p, kr_p, ckv_s, kr_s = [], [], [], []
    k_p, v_p, ki_p, k_s, v_s, ki_s = [], [], [], [], [], []
    mk_list, mv_list = [], []
    for i in range(DEPTH):
        j = i // 2
        mkf, mvf, mkb, mvb = _mem_kv(mem2d, w_mem_kv[i])
        mk_list.append(mkf.reshape(B, n_mem, MEM_HEADS, HEAD_DIM))
        mv_list.append(mvf.reshape(B, n_mem, MEM_HEADS, HEAD_DIM))
        if i % 2 == 0:
            q, k, vt, ckv, krpad, qm_p = _mla_proj_prompt(xp, w_in_mla[j], mla_q_norm[j], mla_kv_norm[j], w_uq[j],
                                                          w_uk[j], w_uv[j], cos_mp, sin_mp, tm_p)
            mix_p = _mla_flash(q, k, vt, B, t_att, t_att)
            ckv_p.append(ckv.reshape(B, S, KV_LORA))
            kr_p.append(krpad[:, QK_NOPE:QK_NOPE + QK_ROPE].reshape(B, S, QK_ROPE))
            qlat, qr, ckv, krpad, qm_s = _mla_proj_sample(xs, w_in_mla[j], mla_q_norm[j], mla_kv_norm[j], w_uq[j],
                                                          w_uk[j], cos_ms, sin_ms, cos_mr, sin_mr, tm_s)
            kr = krpad[:, QK_NOPE:QK_NOPE + QK_ROPE]
            ckv_s.append(ckv.reshape(DB, T, KV_LORA))
            kr_s.append(kr.reshape(DB, T, QK_ROPE))
            qlat = qlat.reshape(DB, T, N_HEADS, KV_LORA).transpose(0, 2, 1, 3).reshape(DB, N_HEADS * T, KV_LORA)
            qr = qr.reshape(DB, T, N_HEADS, QK_ROPE).transpose(0, 2, 1, 3).reshape(DB, N_HEADS * T, QK_ROPE)
            o_lat = _mla_decode(page_table, qlat, qr, pad_new(ckv), pad_new(kr), cache_mla_ckv, pool_krt,
                                j, pp_mla, ns_mla, T)
            o_lat = o_lat.reshape(DB, N_HEADS, T, KV_LORA).transpose(1, 0, 2, 3).reshape(N_HEADS, NS, KV_LORA)
            o = _head_matmul(o_lat, jnp.transpose(w_uv[j], (1, 0, 2)).astype(BF16))
            mix_s = o.transpose(1, 0, 2).reshape(NS, N_HEADS * HEAD_DIM)
            wo_rows = None
        else:
            q, kb, kf, vf, vt, wt, qidx, kw, kib, qm_p = _dsa_proj(xp, w_in_dsa[j], idx_k_norm_g[j],
                                                                   idx_k_norm_b[j], cos_dp, sin_dp, tm_p)
            k_p.append(kf.reshape(B, S, KV_HEADS, HEAD_DIM))
            v_p.append(vf.reshape(B, S, KV_HEADS, HEAD_DIM))
            ki_p.append(kw[:, :IDX_DIM].reshape(B, S, IDX_DIM))
            sct = _indexer_prompt(qidx, kib, wt, B, t_att, t_att)
            thr, cut = _select_cols(sct, B, min(TOPK_MAX, S // 4), _tile(S, 512))
            mix_p = _dsa_attn(q, kb, vt, sct, thr, cut, B, t_att, t_att)

            q, kb, kf, vf, vt, wt, qidx, kw, kib, qm_s = _dsa_proj(xs, w_in_dsa[j], idx_k_norm_g[j],
                                                                   idx_k_norm_b[j], cos_ds, sin_ds, tm_s)
            k_s.append(kf.reshape(DB, T, KV_HEADS, HEAD_DIM))
            v_s.append(vf.reshape(DB, T, KV_HEADS, HEAD_DIM))
            kidx_new = kw[:, :IDX_DIM]
            ki_s.append(kidx_new.reshape(DB, T, IDX_DIM))
            qc = jnp.stack([qidx[h, :, (h % 2) * HEAD_DIM:(h % 2 + 1) * HEAD_DIM] for h in range(IDX_HEADS)])
            qc = qc.reshape(IDX_HEADS, DB, T, IDX_DIM).transpose(1, 0, 2, 3).reshape(DB, IDX_HEADS * T, IDX_DIM)
            w_s = kw[:, HEAD_DIM:HEAD_DIM + IDX_HEADS].reshape(DB, T, IDX_HEADS)
            sc_past, sc_new = _indexer_sample(page_table, qc, w_s, pad_new(kidx_new), pool_kidxt, j, pp_dsa, T)
            topk_s = min(TOPK_MAX, (past + T) // 4)
            cw_s = 5 * LANE if (past + LANE) % (5 * LANE) == 0 else LANE
            sc_all = jnp.concatenate([sc_past, sc_new], axis=-1).reshape(NS, past + LANE)
            thr, cut = _select_rows(sc_all, topk_s, cw_s, _tile(NS, 128))
            qh = jnp.stack([q[h, :, ((h // 3) % 2) * HEAD_DIM:((h // 3) % 2 + 1) * HEAD_DIM] for h in range(N_HEADS)])
            qh = qh.reshape(KV_HEADS, 3, DB, T, HEAD_DIM).transpose(2, 0, 1, 3, 4)
            eye = jnp.eye(KV_HEADS, dtype=BF16)
            qbd = (qh[:, :, :, :, None, :] * eye[None, :, None, None, :, None]).reshape(DB, N_HEADS * T, KVW)
            o = _dsa_decode(page_table, qbd, sc_past, sc_new, thr.reshape(DB, T, LANE), cut.reshape(DB, T, LANE),
                            pad_new(kf).transpose(0, 2, 1), pad_new(vf), pool_kt, pool_vt, j, pp_dsa, ns_dsa, T)
            o = o.reshape(DB, KV_HEADS, 3, T, KV_HEADS, HEAD_DIM)
            o = jnp.stack([o[:, g, :, :, g, :] for g in range(KV_HEADS)], axis=1)
            o = o.reshape(DB, N_HEADS, T, HEAD_DIM)[:, perm]
            mix_s = o.transpose(0, 2, 1, 3).reshape(NS, N_HEADS * HEAD_DIM)
            wo_rows = perm
        memo_p = _mem_attend(qm_p.reshape(B, S, -1), mkb.reshape(B, n_mem, -1), mvb.reshape(B, n_mem, -1), 1,
                             _tile(S, 512), False)
        qm_s3 = qm_s.reshape(DB, T, -1)
        qm_s3 = jnp.concatenate([qm_s3, jnp.zeros((DB, TPAD - T, qm_s3.shape[-1]), BF16)], axis=1)
        mem_kt = jnp.transpose(cache_mem_k[i], (0, 2, 3, 1)).reshape(DB, MEM_HEADS * HEAD_DIM, n_mem)
        mem_vt = jnp.transpose(cache_mem_v[i], (0, 2, 3, 1)).reshape(DB, MEM_HEADS * HEAD_DIM, n_mem)
        memo_s = _mem_attend(qm_s3, mem_kt, mem_vt, _tile(DB, 8), TPAD, True)[:, :T]
        xp = post(xp, mix_p, memo_p.reshape(NP, -1), i, wo_rows, tm_p)
        xs = post(xs, mix_s, memo_s.reshape(NS, -1), i, wo_rows, tm_s)
    return (xp.reshape(B, S, D_MODEL), xs.reshape(DB, T, D_MODEL),
            jnp.stack(ckv_p), jnp.stack(kr_p), jnp.stack(k_p), jnp.stack(v_p), jnp.stack(ki_p),
            jnp.stack(mk_list), jnp.stack(mv_list),
            jnp.stack(ckv_s), jnp.stack(kr_s), jnp.stack(k_s), jnp.stack(v_s), jnp.stack(ki_s))
```

```python
import functools

import jax
import jax.numpy as jnp
from jax import lax
from jax.experimental import pallas as pl
from jax.experimental.pallas import tpu as pltpu

F32 = jnp.float32
BF16 = jnp.bfloat16
I32 = jnp.int32

D_MODEL = 1024
DEPTH = 4
PAGE = 128
HEAD_DIM = 64
N_HEADS = 12
MEM_HEADS = 4
Q_LORA = 384
KV_LORA = 256
QK_NOPE = 64
QK_ROPE = 32
MLA_SCALE = (QK_NOPE + QK_ROPE) ** -0.5
HEAD_SCALE = HEAD_DIM ** -0.5
LOG2E = 1.4426950408889634
MLA_QSCALE = MLA_SCALE * LOG2E
DSA_QSCALE = HEAD_SCALE * LOG2E
KV_HEADS = 4
IDX_HEADS = 8
IDX_DIM = 64
TOPK_MAX = 256
ROT_DIM = HEAD_DIM // 4
ROPE_THETA = 500000.0
N_EXPERTS = 16
N_GROUPS = 4
D_EXPERT = 256
ALPHA = (2 * DEPTH) ** 0.25
LN_EPS = 1e-5
RMS_EPS = 1e-6
NEG = -1e30
LANE = 128
SUBLANE = 8
MIN_I32 = -2 ** 31
N_ACC = 8
DSA_PERM = (0, 3, 1, 4, 2, 5, 6, 9, 7, 10, 8, 11)

NT = (((1,), (1,)), ((), ()))


def _cp(sem, vmem_mb=48):
    return pltpu.CompilerParams(dimension_semantics=sem, vmem_limit_bytes=vmem_mb * 1024 * 1024)


def _dot(a, b):
    return jnp.dot(a, b, preferred_element_type=F32)


def _dot_nt(a, b):
    return lax.dot_general(a, b, NT, preferred_element_type=F32)


def _rms(x, g):
    return x * lax.rsqrt(jnp.mean(x * x, axis=-1, keepdims=True) + RMS_EPS) * g


def _ln(x, g, b):
    mu = jnp.mean(x, axis=-1, keepdims=True)
    xc = x - mu
    var = jnp.mean(xc * xc, axis=-1, keepdims=True)
    return xc * lax.rsqrt(var + LN_EPS) * g + b


def _softmax_step(s, m_prev, l_prev, axis=1):
    m_new = jnp.maximum(m_prev, jnp.max(s, axis=axis, keepdims=True))
    a = jnp.exp2(m_prev - m_new)
    p = jnp.exp2(s - m_new)
    return m_new, a, p, a * l_prev + jnp.sum(p, axis=axis, keepdims=True)


def _merge_streams(ms, ls, accs):
    m = functools.reduce(jnp.maximum, ms)
    ws = [jnp.exp2(mi - m) for mi in ms]
    l = functools.reduce(lambda a, b: a + b, [w * li for w, li in zip(ws, ls)])
    acc = functools.reduce(lambda a, b: a + b, [w * ai for w, ai in zip(ws, accs)])
    return m, l, acc


def _score_keys(s):
    b = lax.bitcast_convert_type(s, I32)
    return jnp.where(s == 0.0, 0, b ^ ((b >> 31) & 0x7FFFFFFF))


def _rope_tables(pos, rot_dim, period, offset, width):
    half = rot_dim // 2
    inv = ROPE_THETA ** (-jnp.arange(half, dtype=jnp.float32) / half)
    ang = pos.astype(jnp.float32)[:, None] * inv
    c, s = jnp.cos(ang), jnp.sin(ang)
    t = pos.shape[0]
    cg = jnp.ones((t, period), F32).at[:, offset:offset + rot_dim].set(jnp.concatenate([c, c], 1))
    sg = jnp.zeros((t, period), F32).at[:, offset:offset + rot_dim].set(jnp.concatenate([s, s], 1))
    return jnp.tile(cg, (1, width // period)), jnp.tile(sg, (1, width // period))


def _rot_cols(w, half):
    return jnp.concatenate([-w[..., half:2 * half], w[..., :half]], axis=-1)


def _rot_heads(w, heads, dim, rot):
    w3 = w.reshape(w.shape[0], heads, dim)
    r = jnp.concatenate([_rot_cols(w3[..., :rot], rot // 2), jnp.zeros_like(w3[..., rot:])], axis=-1)
    return r.reshape(w.shape[0], heads * dim)


def _mla_front(x_ref, w1_ref, w2_ref, qn_ref, kvn_ref, cos_ref, sin_ref):
    xb = x_ref[...].astype(BF16)
    y = _dot(xb, w1_ref[...])
    cqn = _rms(y[:, :Q_LORA], qn_ref[...]).astype(BF16)
    ckv = _rms(y[:, Q_LORA:Q_LORA + KV_LORA], kvn_ref[...])
    qmem = y[:, 640:896]
    cos = cos_ref[...]
    sin = sin_ref[...]
    krp = y[:, 896:1024] * cos + y[:, 1024:1152] * sin
    q2 = _dot(cqn, w2_ref[...])
    return q2, ckv, qmem, krp, cos, sin


def _mla_proj_prompt_kernel(x_ref, w1_ref, w2_ref, wk_ref, wvt_ref, qn_ref, kvn_ref, cos_ref, sin_ref,
                            q_ref, k_ref, vt_ref, ckv_ref, kr_ref, qm_ref):
    q2, ckv, qmem, krp, cos, sin = _mla_front(x_ref, w1_ref, w2_ref, qn_ref, kvn_ref, cos_ref, sin_ref)
    hw = N_HEADS * LANE
    for h in range(N_HEADS):
        qh = q2[:, h * LANE:(h + 1) * LANE] * cos + q2[:, hw + h * LANE:hw + (h + 1) * LANE] * sin
        q_ref[h] = (qh * MLA_QSCALE).astype(BF16)
    ckv_ref[...] = ckv
    kr_ref[...] = krp
    qm_ref[...] = (qmem * HEAD_SCALE).astype(BF16)
    cb = ckv.astype(BF16)
    kn = _dot(cb, wk_ref[...])
    for h in range(N_HEADS):
        k_ref[h] = (kn[:, h * LANE:(h + 1) * LANE] + krp).astype(BF16)
    vt_ref[...] = _dot_nt(wvt_ref[...], cb).astype(BF16)


def _mla_proj_sample_kernel(x_ref, w1_ref, w2_ref, wukt_ref, qn_ref, kvn_ref, cos_ref, sin_ref,
                            cosr_ref, sinr_ref, qlat_ref, qr_ref, ckv_ref, kr_ref, qm_ref):
    q2, ckv, qmem, krp, _, _ = _mla_front(x_ref, w1_ref, w2_ref, qn_ref, kvn_ref, cos_ref, sin_ref)
    hw = N_HEADS * LANE
    rw = N_HEADS * QK_ROPE
    for h in range(N_HEADS):
        qh = (q2[:, h * LANE:(h + 1) * LANE] * MLA_QSCALE).astype(BF16)
        qlat_ref[:, h * KV_LORA:(h + 1) * KV_LORA] = _dot(qh, wukt_ref[h]).astype(BF16)
    qr = q2[:, hw:hw + rw] * cosr_ref[...] + q2[:, hw + rw:hw + 2 * rw] * sinr_ref[...]
    qr_ref[...] = (qr * MLA_QSCALE).astype(BF16)
    ckv_ref[...] = ckv
    kr_ref[...] = krp
    qm_ref[...] = (qmem * HEAD_SCALE).astype(BF16)


def _mla_w1(w_in):
    w_cq, w_ckv = w_in[:, :Q_LORA], w_in[:, Q_LORA:Q_LORA + KV_LORA]
    w_kr = w_in[:, Q_LORA + KV_LORA:Q_LORA + KV_LORA + QK_ROPE]
    w_qm = w_in[:, Q_LORA + KV_LORA + QK_ROPE:]
    z = lambda n: jnp.zeros((D_MODEL, n), F32)
    kr_pad = jnp.concatenate([z(QK_NOPE), w_kr, z(LANE - QK_NOPE - QK_ROPE)], 1)
    kr_rot = jnp.concatenate([z(QK_NOPE), _rot_cols(w_kr, QK_ROPE // 2), z(LANE - QK_NOPE - QK_ROPE)], 1)
    return jnp.concatenate([w_cq, w_ckv, w_qm, kr_pad, kr_rot], 1).astype(BF16)


def _mla_proj_prompt(x, w_in, q_norm, kv_norm, w_uq, w_uk, w_uv, cos, sin, tm):
    n = x.shape[0]
    w1 = _mla_w1(w_in)
    zq = jnp.zeros((Q_LORA, N_HEADS, LANE - QK_NOPE - QK_ROPE), F32)
    uq_pad = jnp.concatenate([w_uq, zq], -1).reshape(Q_LORA, N_HEADS * LANE)
    uq_rot = jnp.concatenate([jnp.zeros((Q_LORA, N_HEADS, QK_NOPE), F32),
                              _rot_cols(w_uq[..., QK_NOPE:], QK_ROPE // 2), zq], -1).reshape(Q_LORA, N_HEADS * LANE)
    w2 = jnp.concatenate([uq_pad, uq_rot], 1).astype(BF16)
    wk = jnp.concatenate([w_uk, jnp.zeros((KV_LORA, N_HEADS, LANE - QK_NOPE), F32)], -1)
    wk = wk.reshape(KV_LORA, N_HEADS * LANE).astype(BF16)
    wvt = w_uv.reshape(KV_LORA, N_HEADS * HEAD_DIM).T.astype(BF16)
    nt = cos.shape[0] // tm
    full = lambda a: pl.BlockSpec(a.shape, lambda i: (0,) * a.ndim)
    rows = lambda w: pl.BlockSpec((tm, w), lambda i: (i, 0))
    heads = pl.BlockSpec((N_HEADS, tm, LANE), lambda i: (0, i, 0))
    tab = pl.BlockSpec((tm, LANE), lambda i: (i % nt, 0))
    qn, kvn = q_norm.reshape(1, -1), kv_norm.reshape(1, -1)
    return pl.pallas_call(
        _mla_proj_prompt_kernel,
        grid=(n // tm,),
        in_specs=[rows(D_MODEL), full(w1), full(w2), full(wk), full(wvt), full(qn), full(kvn), tab, tab],
        out_specs=[heads, heads, pl.BlockSpec((N_HEADS * HEAD_DIM, tm), lambda i: (0, i)), rows(KV_LORA), rows(LANE),
                   rows(MEM_HEADS * HEAD_DIM)],
        out_shape=[jax.ShapeDtypeStruct((N_HEADS, n, LANE), BF16), jax.ShapeDtypeStruct((N_HEADS, n, LANE), BF16),
                   jax.ShapeDtypeStruct((N_HEADS * HEAD_DIM, n), BF16), jax.ShapeDtypeStruct((n, KV_LORA), F32),
                   jax.ShapeDtypeStruct((n, LANE), F32), jax.ShapeDtypeStruct((n, MEM_HEADS * HEAD_DIM), BF16)],
        compiler_params=_cp(("parallel",)),
        name="mla_proj_prompt",
    )(x, w1, w2, wk, wvt, qn, kvn, cos, sin)


def _mla_proj_sample(x, w_in, q_norm, kv_norm, w_uq, w_uk, cos, sin, cosr, sinr, tm):
    n = x.shape[0]
    w1 = _mla_w1(w_in)
    zq = jnp.zeros((Q_LORA, N_HEADS, LANE - QK_NOPE - QK_ROPE), F32)
    uq_pad = jnp.concatenate([w_uq, zq], -1).reshape(Q_LORA, N_HEADS * LANE)
    w_qr = w_uq[..., QK_NOPE:]
    w2 = jnp.concatenate([uq_pad, w_qr.reshape(Q_LORA, -1),
                          _rot_cols(w_qr, QK_ROPE // 2).reshape(Q_LORA, -1)], 1).astype(BF16)
    wukt = jnp.transpose(w_uk, (1, 2, 0))
    wukt = jnp.concatenate([wukt, jnp.zeros((N_HEADS, LANE - QK_NOPE, KV_LORA), F32)], 1).astype(BF16)
    full = lambda a: pl.BlockSpec(a.shape, lambda i: (0,) * a.ndim)
    rows = lambda w: pl.BlockSpec((tm, w), lambda i: (i, 0))
    qn, kvn = q_norm.reshape(1, -1), kv_norm.reshape(1, -1)
    return pl.pallas_call(
        _mla_proj_sample_kernel,
        grid=(n // tm,),
        in_specs=[rows(D_MODEL), full(w1), full(w2), full(wukt), full(qn), full(kvn), rows(LANE), rows(LANE),
                  rows(N_HEADS * QK_ROPE), rows(N_HEADS * QK_ROPE)],
        out_specs=[rows(N_HEADS * KV_LORA), rows(N_HEADS * QK_ROPE), rows(KV_LORA), rows(LANE),
                   rows(MEM_HEADS * HEAD_DIM)],
        out_shape=[jax.ShapeDtypeStruct((n, N_HEADS * KV_LORA), BF16),
                   jax.ShapeDtypeStruct((n, N_HEADS * QK_ROPE), BF16), jax.ShapeDtypeStruct((n, KV_LORA), F32),
                   jax.ShapeDtypeStruct((n, LANE), F32), jax.ShapeDtypeStruct((n, MEM_HEADS * HEAD_DIM), BF16)],
        compiler_params=_cp(("parallel",)),
        name="mla_proj_sample",
    )(x, w1, w2, wukt, qn, kvn, cos, sin, cosr, sinr)


def _causal_steps(nq, tq, tk):
    qs, ks = [], []
    for qi in range(nq):
        for ki in range(((qi + 1) * tq - 1) // tk + 1):
            qs.append(qi)
            ks.append(ki)
    return jnp.asarray(qs, I32), jnp.asarray(ks, I32)


def _mla_flash_kernel(qt_ref, kt_ref, q_ref, k_ref, vt_ref, o_ref, m_ref, l_ref, acc_ref, *, tq, tk):
    qi = qt_ref[pl.program_id(2)]
    ki = kt_ref[pl.program_id(2)]
    last = ((qi + 1) * tq - 1) // tk

    @pl.when(ki == 0)
    def _():
        m_ref[...] = jnp.full(m_ref.shape, NEG, F32)
        l_ref[...] = jnp.zeros(l_ref.shape, F32)
        acc_ref[...] = jnp.zeros(acc_ref.shape, F32)

    def step(masked):
        ss = [_dot_nt(k_ref[hh], q_ref[hh]) for hh in range(2)]
        if masked:
            key = ki * tk + lax.broadcasted_iota(I32, (tk, tq), 0)
            qry = qi * tq + lax.broadcasted_iota(I32, (tk, tq), 1)
            causal = key <= qry
        for hh in range(2):
            s = jnp.where(causal, ss[hh], NEG) if masked else ss[hh]
            m_new, a, p, l_new = _softmax_step(s, m_ref[hh], l_ref[hh], axis=0)
            vt = vt_ref[hh * HEAD_DIM:(hh + 1) * HEAD_DIM, :]
            acc_ref[hh] = a * acc_ref[hh] + _dot(vt, p.astype(BF16))
            m_ref[hh] = m_new
            l_ref[hh] = l_new

    first_masked = (qi * tq + 1) // tk

    @pl.when(ki < first_masked)
    def _():
        step(False)

    @pl.when(ki >= first_masked)
    def _():
        step(True)

    @pl.when(ki == last)
    def _():
        ot = jnp.concatenate([acc_ref[0] / l_ref[0], acc_ref[1] / l_ref[1]], axis=0)
        o_ref[...] = ot.T.astype(o_ref.dtype)


def _mla_flash(q, k, vt, batch, tq, tk):
    n = q.shape[1]
    s = n // batch
    nq, nk = s // tq, s // tk
    qtab, ktab = _causal_steps(nq, tq, tk)
    grid_spec = pltpu.PrefetchScalarGridSpec(
        num_scalar_prefetch=2,
        grid=(batch, N_HEADS // 2, qtab.shape[0]),
        in_specs=[pl.BlockSpec((2, tq, LANE), lambda b, hp, t, qt, kt: (hp, b * nq + qt[t], 0)),
                  pl.BlockSpec((2, tk, LANE), lambda b, hp, t, qt, kt: (hp, b * nk + kt[t], 0)),
                  pl.BlockSpec((2 * HEAD_DIM, tk), lambda b, hp, t, qt, kt: (hp, b * nk + kt[t]))],
        out_specs=pl.BlockSpec((tq, LANE), lambda b, hp, t, qt, kt: (b * nq + qt[t], hp)),
        scratch_shapes=[pltpu.VMEM((2, 1, tq), F32), pltpu.VMEM((2, 1, tq), F32), pltpu.VMEM((2, HEAD_DIM, tq), F32)],
    )
    return pl.pallas_call(
        functools.partial(_mla_flash_kernel, tq=tq, tk=tk),
        grid_spec=grid_spec,
        out_shape=jax.ShapeDtypeStruct((n, N_HEADS * HEAD_DIM), BF16),
        compiler_params=_cp(("parallel", "parallel", "arbitrary")),
        name="mla_flash",
    )(qtab, ktab, q, k, vt)


def _mla_decode_kernel(pt_ref, qlat_ref, qr_ref, cn_ref, kn_ref, *rest, pp, ns, t_new):
    ck_refs, krt_refs = rest[:pp], rest[pp:2 * pp]
    o_ref, m_ref, l_ref, acc_ref = rest[2 * pp:]
    p_id = pl.program_id(1)
    rows = qlat_ref.shape[1]
    per = pp // ns

    @pl.when(p_id == 0)
    def _():
        m_ref[...] = jnp.full(m_ref.shape, NEG, F32)
        l_ref[...] = jnp.zeros(l_ref.shape, F32)
        acc_ref[...] = jnp.zeros(acc_ref.shape, F32)

    qlat = qlat_ref[0]
    qr = qr_ref[0]
    for st in range(ns):
        pages = range(st * per, (st + 1) * per)
        cks = [ck_refs[i][...].astype(BF16) for i in pages]
        s = jnp.concatenate([_dot_nt(qlat, ck) + _dot(qr, krt_refs[i][...].astype(BF16))
                             for ck, i in zip(cks, pages)], axis=1)
        m_new, a, p, l_new = _softmax_step(s, m_ref[st], l_ref[st])
        pb = p.astype(BF16)
        pv = _dot(pb[:, :PAGE], cks[0])
        for i in range(1, per):
            pv = pv + _dot(pb[:, i * PAGE:(i + 1) * PAGE], cks[i])
        acc_ref[st] = a * acc_ref[st] + pv
        m_ref[st] = m_new
        l_ref[st] = l_new

    @pl.when(p_id == pl.num_programs(1) - 1)
    def _():
        m, l, acc = _merge_streams([m_ref[st] for st in range(ns)], [l_ref[st] for st in range(ns)],
                                   [acc_ref[st] for st in range(ns)])
        cn = cn_ref[0].astype(BF16)
        kn = kn_ref[0].astype(BF16)
        npad = cn.shape[0]
        t = lax.broadcasted_iota(I32, (rows, npad), 0) % t_new
        u = lax.broadcasted_iota(I32, (rows, npad), 1)
        s2 = jnp.where(u <= t, _dot_nt(qlat, cn) + _dot_nt(qr, kn), NEG)
        m2, a2, p2, l2 = _softmax_step(s2, m, l)
        o_ref[0] = (a2 * acc + _dot(p2.astype(BF16), cn)) / l2


def _mla_decode(page_table, qlat, qr, ckv_new, kr_new, pool_ckv, pool_krt, j, pp, ns, t_new):
    db, rows, _ = qlat.shape
    npg = page_table.shape[1]
    pt = page_table.reshape(-1)
    npad = ckv_new.shape[1]
    page = lambda i: (lambda b, p, pt_: (j, pt_[b * npg + p * pp + i], 0, 0))
    per_b = lambda r, w: pl.BlockSpec((1, r, w), lambda b, p, pt_: (b, 0, 0))
    grid_spec = pltpu.PrefetchScalarGridSpec(
        num_scalar_prefetch=1,
        grid=(db, npg // pp),
        in_specs=[per_b(rows, KV_LORA), per_b(rows, QK_ROPE), per_b(npad, KV_LORA), per_b(npad, QK_ROPE)]
        + [pl.BlockSpec((None, None, PAGE, KV_LORA), page(i)) for i in range(pp)]
        + [pl.BlockSpec((None, None, QK_ROPE, PAGE), page(i)) for i in range(pp)],
        out_specs=per_b(rows, KV_LORA),
        scratch_shapes=[pltpu.VMEM((ns, rows, 1), F32), pltpu.VMEM((ns, rows, 1), F32),
                        pltpu.VMEM((ns, rows, KV_LORA), F32)],
    )
    return pl.pallas_call(
        functools.partial(_mla_decode_kernel, pp=pp, ns=ns, t_new=t_new),
        grid_spec=grid_spec,
        out_shape=jax.ShapeDtypeStruct((db, rows, KV_LORA), F32),
        compiler_params=_cp(("parallel", "arbitrary")),
        name="mla_decode",
    )(pt, qlat, qr, ckv_new, kr_new, *([pool_ckv] * pp), *([pool_krt] * pp))


def _head_matmul_kernel(x_ref, w_ref, o_ref):
    o_ref[...] = _dot(x_ref[...].astype(BF16), w_ref[...]).astype(o_ref.dtype)


def _head_matmul(x, w):
    h, n, k = x.shape
    m = w.shape[2]
    return pl.pallas_call(
        _head_matmul_kernel,
        grid=(h,),
        in_specs=[pl.BlockSpec((None, n, k), lambda i: (i, 0, 0)), pl.BlockSpec((None, k, m), lambda i: (i, 0, 0))],
        out_specs=pl.BlockSpec((None, n, m), lambda i: (i, 0, 0)),
        out_shape=jax.ShapeDtypeStruct((h, n, m), BF16),
        compiler_params=_cp(("parallel",)),
        name="head_matmul",
    )(x, w)


def _dsa_proj_kernel(x_ref, wq_ref, wk_ref, wv_ref, wvt_ref, wwt_ref, wqi_ref, wm_ref, wki_ref, lng_ref, lnb_ref,
                     sgn_ref, gperm_ref, brot_ref, cos_ref, sin_ref,
                     q_ref, k_ref, kf_ref, vf_ref, vt_ref, wt_ref, qi_ref, kw_ref, kib_ref, qm_ref):
    tm = x_ref.shape[0]
    xb = x_ref[...].astype(BF16)
    cos = cos_ref[...]
    sin = sin_ref[...]
    lane = lax.broadcasted_iota(I32, (tm, LANE), 1)
    low = lane < HEAD_DIM

    def roped(w_ref, nblk):
        y = _dot(xb, w_ref[...])
        return [y[:, b * LANE:(b + 1) * LANE] * cos + y[:, (nblk + b) * LANE:(nblk + b + 1) * LANE] * sin
                for b in range(nblk)]

    qb = roped(wq_ref, N_HEADS // 2)
    for h in range(N_HEADS):
        g, r = h // 3, h % 3
        blk = qb[(g // 2) * 3 + r] * DSA_QSCALE
        q_ref[h] = jnp.where(low if g % 2 == 0 else jnp.logical_not(low), blk, 0.0).astype(BF16)
    kb = roped(wk_ref, KV_HEADS // 2)
    for b in range(KV_HEADS // 2):
        kf_ref[:, b * LANE:(b + 1) * LANE] = kb[b]
        k_ref[b] = kb[b].astype(BF16)
    vf_ref[...] = _dot(xb, wv_ref[...])
    vt_ref[...] = _dot_nt(wvt_ref[...], xb).astype(BF16)
    wt_ref[...] = _dot_nt(wwt_ref[...], xb)
    qib = roped(wqi_ref, IDX_HEADS // 2)
    for h in range(IDX_HEADS):
        qi_ref[h] = jnp.where(low if h % 2 == 0 else jnp.logical_not(low), qib[h // 2], 0.0).astype(BF16)
    qm_ref[...] = (_dot(xb, wm_ref[...]) * HEAD_SCALE).astype(BF16)
    yk = _dot(xb, wki_ref[...])
    c, c_rot, wpad = yk[:, :LANE], yk[:, LANE:2 * LANE], yk[:, 2 * LANE:3 * LANE]
    mu = jnp.sum(jnp.where(low, c, 0.0), axis=1, keepdims=True) * (1.0 / IDX_DIM)
    xc = c - mu
    var = jnp.sum(jnp.where(low, xc * xc, 0.0), axis=1, keepdims=True) * (1.0 / IDX_DIM)
    rstd = lax.rsqrt(var + LN_EPS)
    ki = xc * rstd * lng_ref[...] + lnb_ref[...]
    ki_rot = (c_rot - sgn_ref[...] * mu) * rstd * gperm_ref[...] + brot_ref[...]
    kir = ki * cos + ki_rot * sin
    kib_ref[...] = kir.astype(BF16)
    kw_ref[...] = jnp.where(low, kir, wpad)


def _dsa_proj(x, w_in, ln_g, ln_b, cos, sin, tm):
    n = x.shape[0]
    o = 0
    cols = []
    for wdt in (N_HEADS * HEAD_DIM, KV_HEADS * HEAD_DIM, KV_HEADS * HEAD_DIM, IDX_HEADS * IDX_DIM, IDX_DIM, IDX_HEADS,
                MEM_HEADS * HEAD_DIM):
        cols.append(w_in[:, o:o + wdt])
        o += wdt
    w_q, w_k, w_v, w_qi, w_ki, w_w, w_qm = cols
    perm = jnp.asarray(DSA_PERM)
    w_qp = w_q.reshape(D_MODEL, N_HEADS, HEAD_DIM)[:, perm].reshape(D_MODEL, -1)
    wq = jnp.concatenate([w_qp, _rot_heads(w_qp, N_HEADS, HEAD_DIM, ROT_DIM)], 1).astype(BF16)
    wk = jnp.concatenate([w_k, _rot_heads(w_k, KV_HEADS, HEAD_DIM, ROT_DIM)], 1).astype(BF16)
    wqi = jnp.concatenate([w_qi, _rot_heads(w_qi, IDX_HEADS, IDX_DIM, ROT_DIM)], 1).astype(BF16)
    w_ki_rot = _rot_heads(w_ki, 1, IDX_DIM, ROT_DIM)
    w_wpad = jnp.concatenate([jnp.zeros((D_MODEL, HEAD_DIM), F32), w_w,
                              jnp.zeros((D_MODEL, LANE - HEAD_DIM - IDX_HEADS), F32)], 1)
    wki = jnp.concatenate([w_ki, w_ki, w_ki_rot, w_ki_rot, w_wpad], 1).astype(BF16)
    wv, wm = w_v.astype(BF16), w_qm.astype(BF16)
    wvt = w_v.T.astype(BF16)
    wwt = jnp.concatenate([w_w.T, jnp.zeros((2 * SUBLANE - IDX_HEADS, D_MODEL), F32)], 0).astype(BF16)
    half = ROT_DIM // 2
    zr = jnp.zeros((IDX_DIM - ROT_DIM,), F32)
    dup = lambda a: jnp.concatenate([a, a]).reshape(1, LANE)
    sgn = dup(jnp.concatenate([-jnp.ones((half,), F32), jnp.ones((half,), F32), zr]))
    gperm = dup(jnp.concatenate([ln_g[half:ROT_DIM], ln_g[:half], zr]))
    brot = dup(jnp.concatenate([-ln_b[half:ROT_DIM], ln_b[:half], zr]))
    lng, lnb = dup(ln_g), dup(ln_b)
    nt = cos.shape[0] // tm
    full = lambda a: pl.BlockSpec(a.shape, lambda i: (0,) * a.ndim)
    rows = lambda w: pl.BlockSpec((tm, w), lambda i: (i, 0))
    heads = lambda h: pl.BlockSpec((h, tm, LANE), lambda i: (0, i, 0))
    colsp = lambda r: pl.BlockSpec((r, tm), lambda i: (0, i))
    tab = pl.BlockSpec((tm, LANE), lambda i: (i % nt, 0))
    sds = jax.ShapeDtypeStruct
    return pl.pallas_call(
        _dsa_proj_kernel,
        grid=(n // tm,),
        in_specs=[rows(D_MODEL), full(wq), full(wk), full(wv), full(wvt), full(wwt), full(wqi), full(wm), full(wki),
                  full(lng), full(lnb), full(sgn), full(gperm), full(brot), tab, tab],
        out_specs=[heads(N_HEADS), heads(KV_HEADS // 2), rows(2 * LANE), rows(2 * LANE), colsp(2 * LANE),
                   colsp(2 * SUBLANE), heads(IDX_HEADS), rows(LANE), rows(LANE), rows(2 * LANE)],
        out_shape=[sds((N_HEADS, n, LANE), BF16), sds((KV_HEADS // 2, n, LANE), BF16), sds((n, 2 * LANE), F32),
                   sds((n, 2 * LANE), F32), sds((2 * LANE, n), BF16), sds((2 * SUBLANE, n), F32),
                   sds((IDX_HEADS, n, LANE), BF16), sds((n, LANE), F32), sds((n, LANE), BF16),
                   sds((n, 2 * LANE), BF16)],
        compiler_params=_cp(("parallel",), 56),
        name="dsa_proj",
    )(x, wq, wk, wv, wvt, wwt, wqi, wm, wki, lng, lnb, sgn, gperm, brot, cos, sin)


def _indexer_prompt_kernel(qt_ref, kt_ref, qi_ref, ki_ref, wt_ref, sc_ref, *, tq, tk):
    qi = qt_ref[pl.program_id(1)]
    ki = kt_ref[pl.program_id(1)]
    kb = ki_ref[...]
    w = wt_ref[...]
    acc = None
    for h in range(IDX_HEADS):
        d = jnp.maximum(_dot_nt(kb, qi_ref[h]), 0.0) * w[h:h + 1]
        acc = d if acc is None else acc + d
    key = ki * tk + lax.broadcasted_iota(I32, (tk, tq), 0)
    qry = qi * tq + lax.broadcasted_iota(I32, (tk, tq), 1)
    sc_ref[...] = jnp.where(key <= qry, acc, NEG)


def _indexer_prompt(qidx, kib, wt, batch, tq, tk):
    n = kib.shape[0]
    s = n // batch
    nq, nk = s // tq, s // tk
    qtab, ktab = _causal_steps(nq, tq, tk)
    grid_spec = pltpu.PrefetchScalarGridSpec(
        num_scalar_prefetch=2,
        grid=(batch, qtab.shape[0]),
        in_specs=[pl.BlockSpec((IDX_HEADS, tq, LANE), lambda b, t, qt, kt: (0, b * nq + qt[t], 0)),
                  pl.BlockSpec((tk, LANE), lambda b, t, qt, kt: (b * nk + kt[t], 0)),
                  pl.BlockSpec((2 * SUBLANE, tq), lambda b, t, qt, kt: (0, b * nq + qt[t]))],
        out_specs=pl.BlockSpec((tk, tq), lambda b, t, qt, kt: (b * nk + kt[t], qt[t])),
    )
    return pl.pallas_call(
        functools.partial(_indexer_prompt_kernel, tq=tq, tk=tk),
        grid_spec=grid_spec,
        out_shape=jax.ShapeDtypeStruct((n, s), F32),
        compiler_params=_cp(("parallel", "arbitrary")),
        name="indexer_prompt",
    )(qtab, ktab, qidx, kib, wt)


def _indexer_sample_kernel(pt_ref, q_ref, w_ref, kn_ref, *rest, pp, t_new):
    kt_refs = rest[:pp]
    sc_ref, scn_ref = rest[pp:]
    q = q_ref[0]
    w = w_ref[0]

    def weighted(d):
        acc = None
        for h in range(IDX_HEADS):
            t = d[h * t_new:(h + 1) * t_new] * w[:, h:h + 1]
            acc = t if acc is None else acc + t
        return acc

    for i in range(pp):
        sc_ref[0, :, i * PAGE:(i + 1) * PAGE] = weighted(jnp.maximum(_dot(q, kt_refs[i][...].astype(BF16)), 0.0))

    @pl.when(pl.program_id(1) == pl.num_programs(1) - 1)
    def _():
        sn = weighted(jnp.maximum(_dot_nt(q, kn_ref[0].astype(BF16)), 0.0))
        npad = sn.shape[1]
        t = lax.broadcasted_iota(I32, (t_new, npad), 0)
        u = lax.broadcasted_iota(I32, (t_new, npad), 1)
        scn_ref[0] = jnp.full((t_new, LANE), NEG, F32)
        scn_ref[0, :, :npad] = jnp.where(u <= t, sn, NEG)


def _indexer_sample(page_table, q, w, kidx_new, pool_kidxt, j, pp, t_new):
    db = q.shape[0]
    npg = page_table.shape[1]
    pt = page_table.reshape(-1)
    npad = kidx_new.shape[1]
    per_b = lambda r, c: pl.BlockSpec((1, r, c), lambda b, p, pt_: (b, 0, 0))
    pool = lambda i: pl.BlockSpec((None, None, IDX_DIM, PAGE), lambda b, p, pt_: (j, pt_[b * npg + p * pp + i], 0, 0))
    grid_spec = pltpu.PrefetchScalarGridSpec(
        num_scalar_prefetch=1,
        grid=(db, npg // pp),
        in_specs=[per_b(IDX_HEADS * t_new, IDX_DIM), per_b(t_new, IDX_HEADS), per_b(npad, IDX_DIM)]
        + [pool(i) for i in range(pp)],
        out_specs=[pl.BlockSpec((1, t_new, pp * PAGE), lambda b, p, pt_: (b, 0, p)), per_b(t_new, LANE)],
    )
    return pl.pallas_call(
        functools.partial(_indexer_sample_kernel, pp=pp, t_new=t_new),
        grid_spec=grid_spec,
        out_shape=[jax.ShapeDtypeStruct((db, t_new, npg * PAGE), F32), jax.ShapeDtypeStruct((db, t_new, LANE), F32)],
        compiler_params=_cp(("parallel", "arbitrary")),
        name="indexer_sample",
    )(pt, q, w, kidx_new, *([pool_kidxt] * pp))


def _select_rows_kernel(sc_ref, thr_ref, cut_ref, key_ref, *, topk, cw, tq, idx_bits):
    nch = sc_ref.shape[1] // cw

    def make_keys(c, carry):
        off = pl.multiple_of(c * cw, cw)
        key_ref[:, pl.ds(off, cw)] = _score_keys(sc_ref[:, pl.ds(off, cw)])
        return carry

    lax.fori_loop(0, nch, make_keys, 0)
    lane = lax.broadcasted_iota(I32, (tq, LANE), 1)

    def count(pred):
        def body(c, acc):
            off = pl.multiple_of(c * cw, cw)
            for u in range(cw // LANE):
                kk = key_ref[:, pl.ds(off + u * LANE, LANE)]
                acc = acc + jnp.where(pred(kk, off + u * LANE + lane), 1.0, 0.0)
            return acc
        acc = lax.fori_loop(0, nch, body, jnp.zeros((tq, LANE), F32))
        return jnp.sum(acc, axis=1, keepdims=True)

    def min_ge(a):
        def body(c, acc):
            off = pl.multiple_of(c * cw, cw)
            for u in range(cw // LANE):
                cols = pl.ds(off + u * LANE, LANE)
                acc = jnp.minimum(acc, jnp.where(key_ref[:, cols] >= a, sc_ref[:, cols], jnp.inf))
            return acc
        acc = lax.fori_loop(0, nch, body, jnp.full((tq, LANE), jnp.inf, F32))
        return jnp.min(acc, axis=1, keepdims=True)

    thr, cut = _threshold_search(count, min_ge, (tq, 1), topk, idx_bits)
    thr_ref[...] = jnp.broadcast_to(thr, (tq, LANE))
    cut_ref[...] = jnp.broadcast_to(cut, (tq, LANE))


def _threshold_search(count, min_ge, shape, topk, idx_bits):
    def value_bit(it, carry):
        ans, cnt = carry

        def refine():
            cand = ans | jnp.left_shift(jnp.int32(1), 31 - it)
            c = count(lambda kk, idx: kk >= (cand ^ MIN_I32))
            ok = c >= topk
            return jnp.where(ok, cand, ans), jnp.where(ok, c, cnt)

        return lax.cond(jnp.max(jnp.abs(cnt - topk)) > 0.0, refine, lambda: (ans, cnt))

    ans, n_ge = lax.fori_loop(0, 32, value_bit, (jnp.zeros(shape, I32), jnp.full(shape, 3e38, F32)))
    thr = min_ge(ans ^ MIN_I32)
    thrk = _score_keys(thr)

    def tie_cut():
        need = topk - count(lambda kk, idx: kk > thrk)

        def tie_bit(it, m):
            cand = m | jnp.left_shift(jnp.int32(1), idx_bits - 1 - it)
            cnt = count(lambda kk, idx: jnp.where(kk == thrk, idx, cand) < cand)
            return jnp.where(cnt < need, cand, m)

        return lax.fori_loop(0, idx_bits, tie_bit, jnp.zeros(shape, I32))

    cut = lax.cond(jnp.max(jnp.abs(n_ge - topk)) > 0.0, tie_cut, lambda: jnp.full(shape, 2 ** 30, I32))
    return thr, cut


def _select_rows(sc, topk, cw, tq):
    n, length = sc.shape
    idx_bits = max(1, (length - 1).bit_length())
    return pl.pallas_call(
        functools.partial(_select_rows_kernel, topk=topk, cw=cw, tq=tq, idx_bits=idx_bits),
        grid=(n // tq,),
        in_specs=[pl.BlockSpec((tq, length), lambda i: (i, 0))],
        out_specs=[pl.BlockSpec((tq, LANE), lambda i: (i, 0)), pl.BlockSpec((tq, LANE), lambda i: (i, 0))],
        out_shape=[jax.ShapeDtypeStruct((n, LANE), F32), jax.ShapeDtypeStruct((n, LANE), I32)],
        scratch_shapes=[pltpu.VMEM((tq, length), I32)],
        compiler_params=_cp(("parallel",)),
        name="topk_select_rows",
    )(sc)


def _select_cols_kernel(sc_ref, thr_ref, cut_ref, key_ref, *, topk, cw, idx_bits, nqt):
    qt = pl.program_id(0) % nqt
    nch = (jnp.maximum((qt + 1) * LANE, topk) + cw - 1) // cw

    def make_keys(c, carry):
        off = pl.multiple_of(c * cw, cw)
        key_ref[pl.ds(off, cw), :] = _score_keys(sc_ref[pl.ds(off, cw), :])
        return carry

    lax.fori_loop(0, nch, make_keys, 0)
    sub = lax.broadcasted_iota(I32, (SUBLANE, LANE), 0)

    def sweep(init, step, combine):
        def body(c, accs):
            off = pl.multiple_of(c * cw, cw)
            accs = list(accs)
            for u in range(cw // SUBLANE):
                accs[u % N_ACC] = step(accs[u % N_ACC], pl.ds(off + u * SUBLANE, SUBLANE), off + u * SUBLANE + sub)
            return tuple(accs)
        accs = lax.fori_loop(0, nch, body, tuple(jnp.full((SUBLANE, LANE), init, F32) for _ in range(N_ACC)))
        return functools.reduce(combine, accs)

    def count(pred):
        acc = sweep(0.0, lambda acc, rows, idx: acc + jnp.where(pred(key_ref[rows, :], idx), 1.0, 0.0),
                    lambda a, b: a + b)
        return jnp.sum(acc, axis=0, keepdims=True)

    def min_ge(a):
        acc = sweep(jnp.inf, lambda acc, rows, idx: jnp.minimum(
            acc, jnp.where(key_ref[rows, :] >= a, sc_ref[rows, :], jnp.inf)), jnp.minimum)
        return jnp.min(acc, axis=0, keepdims=True)

    thr, cut = _threshold_search(count, min_ge, (1, LANE), topk, idx_bits)
    thr_ref[...] = jnp.broadcast_to(thr, (SUBLANE, LANE))
    cut_ref[...] = jnp.broadcast_to(cut, (SUBLANE, LANE))


def _select_cols(sct, batch, topk, cw):
    n, s = sct.shape
    nqt = s // LANE
    idx_bits = max(1, (s - 1).bit_length())
    out = pl.BlockSpec((SUBLANE, LANE), lambda i: (0, i))
    return pl.pallas_call(
        functools.partial(_select_cols_kernel, topk=topk, cw=cw, idx_bits=idx_bits, nqt=nqt),
        grid=(batch * nqt,),
        in_specs=[pl.BlockSpec((s, LANE), lambda i: (i // nqt, i % nqt))],
        out_specs=[out, out],
        out_shape=[jax.ShapeDtypeStruct((SUBLANE, n), F32), jax.ShapeDtypeStruct((SUBLANE, n), I32)],
        scratch_shapes=[pltpu.VMEM((s, LANE), I32)],
        compiler_params=_cp(("parallel",)),
        name="topk_select_cols",
    )(sct)


def _dsa_attn_kernel(qt_ref, kt_ref, q_ref, k_ref, vt_ref, sc_ref, thr_ref, cut_ref, o_ref, m_ref, l_ref, acc_ref,
                     *, tq, tk):
    qi = qt_ref[pl.program_id(1)]
    ki = kt_ref[pl.program_id(1)]
    last = ((qi + 1) * tq - 1) // tk

    @pl.when(ki == 0)
    def _():
        m_ref[...] = jnp.full(m_ref.shape, NEG, F32)
        l_ref[...] = jnp.zeros(l_ref.shape, F32)
        acc_ref[...] = jnp.zeros(acc_ref.shape, F32)

    key = ki * tk + lax.broadcasted_iota(I32, (tk, tq), 0)
    qry = qi * tq + lax.broadcasted_iota(I32, (tk, tq), 1)
    sc = sc_ref[...]
    thr = thr_ref[0:1, :]
    cut = cut_ref[0:1, :]
    sel = jnp.where(sc > thr, 1.0, jnp.where(sc == thr, jnp.where(key <= cut, 1.0, 0.0), 0.0))
    keep = jnp.where(key <= qry, sel, 0.0)
    keep3 = jnp.concatenate([keep, keep, keep], axis=1)
    ss = [_dot_nt(k_ref[g // 2], q_ref[3 * g:3 * g + 3].reshape(3 * tq, LANE)) for g in range(KV_HEADS)]
    for g in range(KV_HEADS):
        s = jnp.where(keep3 > 0.0, ss[g], NEG)
        m_new, a, p, l_new = _softmax_step(s, m_ref[g], l_ref[g], axis=0)
        acc_ref[g] = a * acc_ref[g] + _dot(vt_ref[g * HEAD_DIM:(g + 1) * HEAD_DIM, :], p.astype(BF16))
        m_ref[g] = m_new
        l_ref[g] = l_new

    @pl.when(ki == last)
    def _():
        for gp in range(KV_HEADS // 2):
            oe = acc_ref[2 * gp] / l_ref[2 * gp]
            oo = acc_ref[2 * gp + 1] / l_ref[2 * gp + 1]
            for r in range(3):
                blk = gp * 3 + r
                ot = jnp.concatenate([oe[:, r * tq:(r + 1) * tq], oo[:, r * tq:(r + 1) * tq]], axis=0)
                o_ref[:, blk * LANE:(blk + 1) * LANE] = ot.T.astype(o_ref.dtype)


def _dsa_attn(q, k, vt, sct, thr, cut, batch, tq, tk):
    n = q.shape[1]
    s = n // batch
    nq, nk = s // tq, s // tk
    qtab, ktab = _causal_steps(nq, tq, tk)
    qcol = pl.BlockSpec((SUBLANE, tq), lambda b, t, qt, kt: (0, b * nq + qt[t]))
    grid_spec = pltpu.PrefetchScalarGridSpec(
        num_scalar_prefetch=2,
        grid=(batch, qtab.shape[0]),
        in_specs=[pl.BlockSpec((N_HEADS, tq, LANE), lambda b, t, qt, kt: (0, b * nq + qt[t], 0)),
                  pl.BlockSpec((KV_HEADS // 2, tk, LANE), lambda b, t, qt, kt: (0, b * nk + kt[t], 0)),
                  pl.BlockSpec((KV_HEADS * HEAD_DIM, tk), lambda b, t, qt, kt: (0, b * nk + kt[t])),
                  pl.BlockSpec((tk, tq), lambda b, t, qt, kt: (b * nk + kt[t], qt[t])),
                  qcol, qcol],
        out_specs=pl.BlockSpec((tq, N_HEADS * HEAD_DIM), lambda b, t, qt, kt: (b * nq + qt[t], 0)),
        scratch_shapes=[pltpu.VMEM((KV_HEADS, 1, 3 * tq), F32), pltpu.VMEM((KV_HEADS, 1, 3 * tq), F32),
                        pltpu.VMEM((KV_HEADS, HEAD_DIM, 3 * tq), F32)],
    )
    return pl.pallas_call(
        functools.partial(_dsa_attn_kernel, tq=tq, tk=tk),
        grid_spec=grid_spec,
        out_shape=jax.ShapeDtypeStruct((n, N_HEADS * HEAD_DIM), BF16),
        compiler_params=_cp(("parallel", "arbitrary"), 56),
        name="dsa_attn",
    )(qtab, ktab, q, k, vt, sct, thr, cut)


def _dsa_decode_kernel(pt_ref, q_ref, sc_ref, scn_ref, thr_ref, cut_ref, knt_ref, vn_ref, *rest, pp, ns, t_new, past):
    kt_refs, vt_refs = rest[:pp], rest[pp:2 * pp]
    o_ref, m_ref, l_ref, acc_ref = rest[2 * pp:]
    p_id = pl.program_id(1)
    rows = q_ref.shape[1]
    reps = rows // t_new
    per = pp // ns
    width = per * PAGE

    @pl.when(p_id == 0)
    def _():
        m_ref[...] = jnp.full(m_ref.shape, NEG, F32)
        l_ref[...] = jnp.zeros(l_ref.shape, F32)
        acc_ref[...] = jnp.zeros(acc_ref.shape, F32)

    thr = thr_ref[0][:, :1]
    cut = cut_ref[0][:, :1]
    q = q_ref[0]

    def keep_rows(sc, col, extra):
        sel = jnp.where(sc > thr, 1.0, jnp.where(sc == thr, jnp.where(col <= cut, 1.0, 0.0), 0.0))
        if extra is not None:
            sel = jnp.where(extra, sel, 0.0)
        return jnp.concatenate([sel] * reps, axis=0)

    for st in range(ns):
        pages = range(st * per, (st + 1) * per)
        col = p_id * (pp * PAGE) + st * width + lax.broadcasted_iota(I32, (t_new, width), 1)
        keep = keep_rows(sc_ref[0, :, st * width:(st + 1) * width], col, None)
        s = jnp.concatenate([_dot(q, kt_refs[i][...].astype(BF16)) for i in pages], axis=1)
        s = jnp.where(keep > 0.0, s, NEG)
        m_new, a, p, l_new = _softmax_step(s, m_ref[st], l_ref[st])
        pb = p.astype(BF16)
        pv = None
        for n_, i in enumerate(pages):
            t = _dot_nt(pb[:, n_ * PAGE:(n_ + 1) * PAGE], vt_refs[i][...].astype(BF16))
            pv = t if pv is None else pv + t
        acc_ref[st] = a * acc_ref[st] + pv
        m_ref[st] = m_new
        l_ref[st] = l_new

    @pl.when(p_id == pl.num_programs(1) - 1)
    def _():
        m, l, acc = _merge_streams([m_ref[st] for st in range(ns)], [l_ref[st] for st in range(ns)],
                                   [acc_ref[st] for st in range(ns)])
        npad = vn_ref.shape[1]
        t = lax.broadcasted_iota(I32, (t_new, npad), 0)
        u = lax.broadcasted_iota(I32, (t_new, npad), 1)
        keep_n = keep_rows(scn_ref[0][:, :npad], past + u, u <= t)
        s2 = jnp.where(keep_n > 0.0, _dot(q, knt_ref[0].astype(BF16)), NEG)
        m2, a2, p2, l2 = _softmax_step(s2, m, l)
        o_ref[0] = (a2 * acc + _dot(p2.astype(BF16), vn_ref[0].astype(BF16))) / l2


def _dsa_decode(page_table, q, sc, scn, thr, cut, knt, v_new, pool_kt, pool_vt, j, pp, ns, t_new):
    db, rows, wd = q.shape
    npg = page_table.shape[1]
    pt = page_table.reshape(-1)
    npad = v_new.shape[1]
    per_b = lambda r, c: pl.BlockSpec((1, r, c), lambda b, p, pt_: (b, 0, 0))
    pool = lambda i: pl.BlockSpec((None, None, wd, PAGE), lambda b, p, pt_: (j, pt_[b * npg + p * pp + i], 0, 0))
    grid_spec = pltpu.PrefetchScalarGridSpec(
        num_scalar_prefetch=1,
        grid=(db, npg // pp),
        in_specs=[per_b(rows, wd), pl.BlockSpec((1, t_new, pp * PAGE), lambda b, p, pt_: (b, 0, p)),
                  per_b(t_new, LANE), per_b(t_new, LANE), per_b(t_new, LANE), per_b(wd, npad), per_b(npad, wd)]
        + [pool(i) for i in range(pp)] + [pool(i) for i in range(pp)],
        out_specs=per_b(rows, wd),
        scratch_shapes=[pltpu.VMEM((ns, rows, 1), F32), pltpu.VMEM((ns, rows, 1), F32),
                        pltpu.VMEM((ns, rows, wd), F32)],
    )
    return pl.pallas_call(
        functools.partial(_dsa_decode_kernel, pp=pp, ns=ns, t_new=t_new, past=npg * PAGE),
        grid_spec=grid_spec,
        out_shape=jax.ShapeDtypeStruct((db, rows, wd), F32),
        compiler_params=_cp(("parallel", "arbitrary")),
        name="dsa_decode",
    )(pt, q, sc, scn, thr, cut, knt, v_new, *([pool_kt] * pp), *([pool_vt] * pp))


def _mem_kv_kernel(m_ref, w_ref, kf_ref, vf_ref, kb_ref, vb_ref):
    y = _dot(m_ref[...].astype(BF16), w_ref[...])
    w = MEM_HEADS * HEAD_DIM
    kf_ref[...] = y[:, :w]
    vf_ref[...] = y[:, w:]
    kb_ref[...] = y[:, :w].astype(BF16)
    vb_ref[...] = y[:, w:].astype(BF16)


def _mem_kv(mem2d, w):
    n = mem2d.shape[0]
    wd = MEM_HEADS * HEAD_DIM
    wb = w.astype(BF16)
    spec = pl.BlockSpec((n, wd), lambda i: (0, 0))
    return pl.pallas_call(
        _mem_kv_kernel,
        grid=(1,),
        in_specs=[pl.BlockSpec(mem2d.shape, lambda i: (0, 0)), pl.BlockSpec(wb.shape, lambda i: (0, 0))],
        out_specs=[spec] * 4,
        out_shape=[jax.ShapeDtypeStruct((n, wd), F32)] * 2 + [jax.ShapeDtypeStruct((n, wd), BF16)] * 2,
        compiler_params=_cp(("arbitrary",)),
        name="mem_kv",
    )(mem2d, wb)


def _mem_attend_kernel(q_ref, mk_ref, mv_ref, o_ref, *, bb, kv_t):
    tt = q_ref.shape[1]
    lane = lax.broadcasted_iota(I32, (tt, LANE), 1)
    low = lane < HEAD_DIM
    for b in range(bb):
        for blk in range(MEM_HEADS // 2):
            sl = slice(blk * LANE, (blk + 1) * LANE)
            qb = q_ref[b, :, sl]
            mkb = (mk_ref[b, sl, :] if kv_t else mk_ref[b, :, sl]).astype(BF16)
            mvb = (mv_ref[b, sl, :] if kv_t else mv_ref[b, :, sl]).astype(BF16)
            res = []
            for half in range(2):
                qh = jnp.where(low if half == 0 else jnp.logical_not(low), qb, jnp.zeros_like(qb))
                s = _dot(qh, mkb) if kv_t else _dot_nt(qh, mkb)
                p = jnp.exp(s - jnp.max(s, axis=1, keepdims=True))
                p = (p / jnp.sum(p, axis=1, keepdims=True)).astype(BF16)
                res.append(_dot_nt(p, mvb) if kv_t else _dot(p, mvb))
            o_ref[b, :, sl] = jnp.where(low, res[0], res[1]).astype(o_ref.dtype)


def _mem_attend(q, mk, mv, bb, tt, kv_t, layer=None):
    bm, t, wd = q.shape
    if layer is None:
        kv = pl.BlockSpec((bb,) + mk.shape[1:], lambda b, i: (b, 0, 0))
    else:
        kv = pl.BlockSpec((None, bb) + mk.shape[2:], lambda b, i: (layer, b, 0, 0))
    return pl.pallas_call(
        functools.partial(_mem_attend_kernel, bb=bb, kv_t=kv_t),
        grid=(bm // bb, t // tt),
        in_specs=[pl.BlockSpec((bb, tt, wd), lambda b, i: (b, i, 0)), kv, kv],
        out_specs=pl.BlockSpec((bb, tt, wd), lambda b, i: (b, i, 0)),
        out_shape=jax.ShapeDtypeStruct((bm, t, wd), BF16),
        compiler_params=_cp(("parallel", "parallel")),
        name="mem_attend",
    )(q, mk, mv)


def _split_bf16(x):
    hi = x.astype(BF16)
    return hi, (x - hi.astype(F32)).astype(BF16)


def _outproj_router_kernel(x_ref, mix_ref, mem_ref, wo1_ref, wo2_ref, g_ref, b_ref, wrh_ref, wrl_ref, rb_ref,
                           x1_ref, gate_ref):
    y = ALPHA * x_ref[...] + _dot(mix_ref[...].astype(BF16), wo1_ref[...]) + _dot(mem_ref[...], wo2_ref[...])
    x1 = _ln(y, g_ref[...], b_ref[...])
    x1_ref[...] = x1
    xh, xl = _split_bf16(x1)
    logit = _dot_nt(wrh_ref[...], xh) + _dot_nt(wrh_ref[...], xl) + _dot_nt(wrl_ref[...], xh)
    s = 1.0 / (1.0 + jnp.exp(-logit))
    sb = s + rb_ref[...]
    epg = N_EXPERTS // N_GROUPS
    rows = [sb[e:e + 1] for e in range(N_EXPERTS)]
    gscore = []
    for g in range(N_GROUPS):
        v = rows[g * epg:(g + 1) * epg]
        best = None
        for a in range(epg):
            for b in range(a + 1, epg):
                pr = v[a] + v[b]
                best = pr if best is None else jnp.maximum(best, pr)
        gscore.append(best)
    gmax = functools.reduce(jnp.maximum, gscore)
    taken = jnp.zeros_like(gmax)
    gsel = []
    for g in range(N_GROUPS):
        pick = jnp.where(gscore[g] == gmax, 1.0, 0.0) * (1.0 - taken)
        taken = taken + pick
        gsel.append(pick)
    val = [jnp.where(gsel[e // epg] > 0.0, rows[e], NEG) for e in range(N_EXPERTS)]
    chosen = [jnp.zeros_like(gmax) for _ in range(N_EXPERTS)]
    for _ in range(2):
        vmax = functools.reduce(jnp.maximum, val)
        taken = jnp.zeros_like(gmax)
        for e in range(N_EXPERTS):
            pick = jnp.where(val[e] == vmax, 1.0, 0.0) * (1.0 - taken)
            taken = taken + pick
            chosen[e] = chosen[e] + pick
            val[e] = jnp.where(pick > 0.0, 2.0 * NEG, val[e])
    wsel = [jnp.where(chosen[e] > 0.0, s[e:e + 1], 0.0) for e in range(N_EXPERTS)]
    wsum = functools.reduce(lambda a, b: a + b, wsel)
    gate_ref[...] = jnp.concatenate([w / wsum for w in wsel], axis=0)


def _outproj_router(x, mix, mem, wo1, wo2, g, b, wrh, wrl, rb, tm):
    n = x.shape[0]
    full = lambda a: pl.BlockSpec(a.shape, lambda i: (0,) * a.ndim)
    rows = lambda w: pl.BlockSpec((tm, w), lambda i: (i, 0))
    return pl.pallas_call(
        _outproj_router_kernel,
        grid=(n // tm,),
        in_specs=[rows(D_MODEL), rows(mix.shape[1]), rows(mem.shape[1]), full(wo1), full(wo2), full(g), full(b),
                  full(wrh), full(wrl), full(rb)],
        out_specs=[rows(D_MODEL), pl.BlockSpec((N_EXPERTS, tm), lambda i: (0, i))],
        out_shape=[jax.ShapeDtypeStruct((n, D_MODEL), F32), jax.ShapeDtypeStruct((N_EXPERTS, n), F32)],
        compiler_params=_cp(("parallel",)),
        name="outproj_router",
    )(x, mix, mem, wo1, wo2, g, b, wrh, wrl, rb)


def _moe_kernel(x_ref, gate_ref, wgu_ref, wd_ref, g_ref, b_ref, o_ref, acc_ref, xb_ref):
    e = pl.program_id(1)

    @pl.when(e == 0)
    def _():
        acc_ref[...] = jnp.zeros(acc_ref.shape, F32)
        xb_ref[...] = x_ref[...].astype(BF16)

    gate = gate_ref[...]
    lane = lax.broadcasted_iota(I32, gate.shape, 1)
    gcol = jnp.sum(jnp.where(lane == e, gate, 0.0), axis=1, keepdims=True)
    hgu = _dot(xb_ref[...], wgu_ref[...])
    hg, hu = hgu[:, :D_EXPERT], hgu[:, D_EXPERT:]
    h = hg * (1.0 / (1.0 + jnp.exp(-hg))) * hu * gcol
    acc_ref[...] += _dot(h.astype(BF16), wd_ref[...])

    @pl.when(e == pl.num_programs(1) - 1)
    def _():
        o_ref[...] = _ln(ALPHA * x_ref[...] + acc_ref[...], g_ref[...], b_ref[...])


def _moe(x, gate, wgu, wd, g, b, tm):
    n = x.shape[0]
    full = lambda a: pl.BlockSpec(a.shape, lambda i, e: (0,) * a.ndim)
    return pl.pallas_call(
        _moe_kernel,
        grid=(n // tm, N_EXPERTS),
        in_specs=[pl.BlockSpec((tm, D_MODEL), lambda i, e: (i, 0)), pl.BlockSpec((tm, N_EXPERTS), lambda i, e: (i, 0)),
                  pl.BlockSpec((None, D_MODEL, 2 * D_EXPERT), lambda i, e: (e, 0, 0)),
                  pl.BlockSpec((None, D_EXPERT, D_MODEL), lambda i, e: (e, 0, 0)), full(g), full(b)],
        out_specs=pl.BlockSpec((tm, D_MODEL), lambda i, e: (i, 0)),
        out_shape=jax.ShapeDtypeStruct((n, D_MODEL), F32),
        scratch_shapes=[pltpu.VMEM((tm, D_MODEL), F32), pltpu.VMEM((tm, D_MODEL), BF16)],
        compiler_params=_cp(("parallel", "arbitrary")),
        name="moe",
    )(x, gate, wgu, wd, g, b)


def _tile(n, pref):
    t = min(n, pref)
    while n % t:
        t //= 2
    return t


def kernel(x_prompt, x_sample, cache_mla_ckv, cache_mla_krope, cache_dsa_k, cache_dsa_v, cache_dsa_kidx, cache_mem_k, cache_mem_v, page_table, mem_prompt, w_in_mla, mla_q_norm, mla_kv_norm, w_uq, w_uk, w_uv, w_in_dsa, idx_k_norm_g, idx_k_norm_b, w_mem_kv, w_out, ln1_g, ln1_b, ln2_g, ln2_b, w_router, router_bias, w_gate, w_up, w_down):
    B, S, _ = x_prompt.shape
    DB, T, _ = x_sample.shape
    n_mem = mem_prompt.shape[1]
    npg = page_table.shape[1]
    past = npg * PAGE
    n_pool = cache_mla_ckv.shape[1]
    NP, NS = B * S, DB * T
    TPAD = 16
    pp_mla = 16 if npg % 16 == 0 else npg
    pp_dsa = 8 if npg % 8 == 0 else npg
    pp_idx = 32 if npg % 32 == 0 else pp_dsa
    ns_mla = 2 if pp_mla % 2 == 0 else 1
    ns_dsa = 2 if pp_dsa % 2 == 0 else 1
    tm_p, tm_s = _tile(NP, 512), _tile(NS, 512)
    t_att = _tile(S, 512)
    KVW = KV_HEADS * HEAD_DIM

    pos_p = jnp.arange(S, dtype=jnp.int32)
    pos_s = jnp.tile(past + jnp.arange(T, dtype=jnp.int32), DB)
    cos_mp, sin_mp = _rope_tables(pos_p, QK_ROPE, LANE, QK_NOPE, LANE)
    cos_ms, sin_ms = _rope_tables(pos_s, QK_ROPE, LANE, QK_NOPE, LANE)
    cos_mr, sin_mr = _rope_tables(pos_s, QK_ROPE, QK_ROPE, 0, N_HEADS * QK_ROPE)
    cos_dp, sin_dp = _rope_tables(pos_p, ROT_DIM, HEAD_DIM, 0, LANE)
    cos_ds, sin_ds = _rope_tables(pos_s, ROT_DIM, HEAD_DIM, 0, LANE)

    wrt = w_router.T
    wrh = wrt.astype(BF16)
    wrl = (wrt - wrh.astype(F32)).astype(BF16)
    rb = router_bias.reshape(N_EXPERTS, 1).astype(F32)
    perm = jnp.asarray(DSA_PERM)
    pool_kt = jnp.transpose(cache_dsa_k, (0, 1, 3, 4, 2)).reshape(-1, n_pool, KVW, PAGE)
    pool_vt = jnp.transpose(cache_dsa_v, (0, 1, 3, 4, 2)).reshape(-1, n_pool, KVW, PAGE)
    pool_kidxt = jnp.transpose(cache_dsa_kidx, (0, 1, 3, 2))
    pool_krt = jnp.transpose(cache_mla_krope, (0, 1, 3, 2))
    mem_kt = jnp.transpose(cache_mem_k, (0, 1, 3, 4, 2)).reshape(DEPTH, DB, MEM_HEADS * HEAD_DIM, n_mem)
    mem_vt = jnp.transpose(cache_mem_v, (0, 1, 3, 4, 2)).reshape(DEPTH, DB, MEM_HEADS * HEAD_DIM, n_mem)
    mem2d = mem_prompt.reshape(B * n_mem, D_MODEL)

    def pad_new(a):
        a = a.reshape(DB, T, a.shape[-1])
        return jnp.concatenate([a, jnp.zeros((DB, TPAD - T, a.shape[-1]), a.dtype)], axis=1)

    def post(x, mix, mem, i, wo_rows, tm):
        wo = w_out[i]
        wo1 = wo[:N_HEADS * HEAD_DIM]
        if wo_rows is not None:
            wo1 = wo1.reshape(N_HEADS, HEAD_DIM, D_MODEL)[wo_rows].reshape(N_HEADS * HEAD_DIM, D_MODEL)
        x1, gate_t = _outproj_router(x, mix, mem, wo1.astype(BF16), wo[N_HEADS * HEAD_DIM:].astype(BF16),
                                     ln1_g[i].reshape(1, -1), ln1_b[i].reshape(1, -1), wrh, wrl, rb, tm)
        wgu = jnp.concatenate([w_gate[i], w_up[i]], axis=-1).astype(BF16)
        return _moe(x1, gate_t.T, wgu, w_down[i].astype(BF16), ln2_g[i].reshape(1, -1), ln2_b[i].reshape(1, -1),
                    _tile(x.shape[0], 1024))

    xp = x_prompt.reshape(NP, D_MODEL)
    xs = x_sample.reshape(NS, D_MODEL)
    ckv_p, kr_p, ckv_s, kr_s = [], [], [], []
    k_p, v_p, ki_p, k_s, v_s, ki_s = [], [], [], [], [], []
    mk_list, mv_list = [], []
    for i in range(DEPTH):
        j = i // 2
        mkf, mvf, mkb, mvb = _mem_kv(mem2d, w_mem_kv[i])
        mk_list.append(mkf.reshape(B, n_mem, MEM_HEADS, HEAD_DIM))
        mv_list.append(mvf.reshape(B, n_mem, MEM_HEADS, HEAD_DIM))
        if i % 2 == 0:
            q, k, vt, ckv, krpad, qm_p = _mla_proj_prompt(xp, w_in_mla[j], mla_q_norm[j], mla_kv_norm[j], w_uq[j],
                                                          w_uk[j], w_uv[j], cos_mp, sin_mp, tm_p)
            mix_p = _mla_flash(q, k, vt, B, t_att, _tile(S, 1024))
            ckv_p.append(ckv.reshape(B, S, KV_LORA))
            kr_p.append(krpad[:, QK_NOPE:QK_NOPE + QK_ROPE].reshape(B, S, QK_ROPE))
            qlat, qr, ckv, krpad, qm_s = _mla_proj_sample(xs, w_in_mla[j], mla_q_norm[j], mla_kv_norm[j], w_uq[j],
                                                          w_uk[j], cos_ms, sin_ms, cos_mr, sin_mr, tm_s)
            kr = krpad[:, QK_NOPE:QK_NOPE + QK_ROPE]
            ckv_s.append(ckv.reshape(DB, T, KV_LORA))
            kr_s.append(kr.reshape(DB, T, QK_ROPE))
            qlat = qlat.reshape(DB, T, N_HEADS, KV_LORA).transpose(0, 2, 1, 3).reshape(DB, N_HEADS * T, KV_LORA)
            qr = qr.reshape(DB, T, N_HEADS, QK_ROPE).transpose(0, 2, 1, 3).reshape(DB, N_HEADS * T, QK_ROPE)
            o_lat = _mla_decode(page_table, qlat, qr, pad_new(ckv), pad_new(kr), cache_mla_ckv, pool_krt,
                                j, pp_mla, ns_mla, T)
            o_lat = o_lat.reshape(DB, N_HEADS, T, KV_LORA).transpose(1, 0, 2, 3).reshape(N_HEADS, NS, KV_LORA)
            o = _head_matmul(o_lat, jnp.transpose(w_uv[j], (1, 0, 2)).astype(BF16))
            mix_s = o.transpose(1, 0, 2).reshape(NS, N_HEADS * HEAD_DIM)
            wo_rows = None
        else:
            q, kb, kf, vf, vt, wt, qidx, kw, kib, qm_p = _dsa_proj(xp, w_in_dsa[j], idx_k_norm_g[j],
                                                                   idx_k_norm_b[j], cos_dp, sin_dp, tm_p)
            k_p.append(kf.reshape(B, S, KV_HEADS, HEAD_DIM))
            v_p.append(vf.reshape(B, S, KV_HEADS, HEAD_DIM))
            ki_p.append(kw[:, :IDX_DIM].reshape(B, S, IDX_DIM))
            sct = _indexer_prompt(qidx, kib, wt, B, t_att, t_att)
            thr, cut = _select_cols(sct, B, min(TOPK_MAX, S // 4), _tile(S, 512))
            mix_p = _dsa_attn(q, kb, vt, sct, thr, cut, B, t_att, t_att)

            q, kb, kf, vf, vt, wt, qidx, kw, kib, qm_s = _dsa_proj(xs, w_in_dsa[j], idx_k_norm_g[j],
                                                                   idx_k_norm_b[j], cos_ds, sin_ds, tm_s)
            k_s.append(kf.reshape(DB, T, KV_HEADS, HEAD_DIM))
            v_s.append(vf.reshape(DB, T, KV_HEADS, HEAD_DIM))
            kidx_new = kw[:, :IDX_DIM]
            ki_s.append(kidx_new.reshape(DB, T, IDX_DIM))
            qc = jnp.stack([qidx[h, :, (h % 2) * HEAD_DIM:(h % 2 + 1) * HEAD_DIM] for h in range(IDX_HEADS)])
            qc = qc.reshape(IDX_HEADS, DB, T, IDX_DIM).transpose(1, 0, 2, 3).reshape(DB, IDX_HEADS * T, IDX_DIM)
            w_s = kw[:, HEAD_DIM:HEAD_DIM + IDX_HEADS].reshape(DB, T, IDX_HEADS)
            sc_past, sc_new = _indexer_sample(page_table, qc, w_s, pad_new(kidx_new), pool_kidxt, j, pp_idx, T)
            topk_s = min(TOPK_MAX, (past + T) // 4)
            cw_s = 5 * LANE if (past + LANE) % (5 * LANE) == 0 else LANE
            sc_all = jnp.concatenate([sc_past, sc_new], axis=-1).reshape(NS, past + LANE)
            thr, cut = _select_rows(sc_all, topk_s, cw_s, _tile(NS, 128))
            qh = jnp.stack([q[h, :, ((h // 3) % 2) * HEAD_DIM:((h // 3) % 2 + 1) * HEAD_DIM] for h in range(N_HEADS)])
            qh = qh.reshape(KV_HEADS, 3, DB, T, HEAD_DIM).transpose(2, 0, 1, 3, 4)
            eye = jnp.eye(KV_HEADS, dtype=BF16)
            qbd = (qh[:, :, :, :, None, :] * eye[None, :, None, None, :, None]).reshape(DB, N_HEADS * T, KVW)
            o = _dsa_decode(page_table, qbd, sc_past, sc_new, thr.reshape(DB, T, LANE), cut.reshape(DB, T, LANE),
                            pad_new(kf).transpose(0, 2, 1), pad_new(vf), pool_kt, pool_vt, j, pp_dsa, ns_dsa, T)
            o = o.reshape(DB, KV_HEADS, 3, T, KV_HEADS, HEAD_DIM)
            o = jnp.stack([o[:, g, :, :, g, :] for g in range(KV_HEADS)], axis=1)
            o = o.reshape(DB, N_HEADS, T, HEAD_DIM)[:, perm]
            mix_s = o.transpose(0, 2, 1, 3).reshape(NS, N_HEADS * HEAD_DIM)
            wo_rows = perm
        memo_p = _mem_attend(qm_p.reshape(B, S, -1), mkb.reshape(B, n_mem, -1), mvb.reshape(B, n_mem, -1), 1,
                             _tile(S, 512), False)
        qm_s3 = qm_s.reshape(DB, T, -1)
        qm_s3 = jnp.concatenate([qm_s3, jnp.zeros((DB, TPAD - T, qm_s3.shape[-1]), BF16)], axis=1)
        memo_s = _mem_attend(qm_s3, mem_kt, mem_vt, _tile(DB, 8), TPAD, True, layer=i)[:, :T]
        xp = post(xp, mix_p, memo_p.reshape(NP, -1), i, wo_rows, tm_p)
        xs = post(xs, mix_s, memo_s.reshape(NS, -1), i, wo_rows, tm_s)
    return (xp.reshape(B, S, D_MODEL), xs.reshape(DB, T, D_MODEL),
            jnp.stack(ckv_p), jnp.stack(kr_p), jnp.stack(k_p), jnp.stack(v_p), jnp.stack(ki_p),
            jnp.stack(mk_list), jnp.stack(mv_list),
            jnp.stack(ckv_s), jnp.stack(kr_s), jnp.stack(k_s), jnp.stack(v_s), jnp.stack(ki_s))
```

```python
import functools

import jax
import jax.numpy as jnp
from jax import lax
from jax.experimental import pallas as pl
from jax.experimental.pallas import tpu as pltpu

F32 = jnp.float32
BF16 = jnp.bfloat16
I32 = jnp.int32
I16 = jnp.int16

D_MODEL = 1024
DEPTH = 4
PAGE = 128
HEAD_DIM = 64
N_HEADS = 12
MEM_HEADS = 4
Q_LORA = 384
KV_LORA = 256
QK_NOPE = 64
QK_ROPE = 32
MLA_SCALE = (QK_NOPE + QK_ROPE) ** -0.5
HEAD_SCALE = HEAD_DIM ** -0.5
LOG2E = 1.4426950408889634
MLA_QSCALE = MLA_SCALE * LOG2E
DSA_QSCALE = HEAD_SCALE * LOG2E
KV_HEADS = 4
IDX_HEADS = 8
IDX_DIM = 64
TOPK_MAX = 256
ROT_DIM = HEAD_DIM // 4
ROPE_THETA = 500000.0
N_EXPERTS = 16
N_GROUPS = 4
D_EXPERT = 256
ALPHA = (2 * DEPTH) ** 0.25
LN_EPS = 1e-5
RMS_EPS = 1e-6
NEG = -1e30
LANE = 128
SUBLANE = 8
MIN_I32 = -2 ** 31
N_ACC = 8
DSA_PERM = (0, 3, 1, 4, 2, 5, 6, 9, 7, 10, 8, 11)

NT = (((1,), (1,)), ((), ()))


def _cp(sem, vmem_mb=48):
    return pltpu.CompilerParams(dimension_semantics=sem, vmem_limit_bytes=vmem_mb * 1024 * 1024)


def _dot(a, b):
    return jnp.dot(a, b, preferred_element_type=F32)


def _dot_nt(a, b):
    return lax.dot_general(a, b, NT, preferred_element_type=F32)


def _rms(x, g):
    return x * lax.rsqrt(jnp.mean(x * x, axis=-1, keepdims=True) + RMS_EPS) * g


def _ln(x, g, b):
    mu = jnp.mean(x, axis=-1, keepdims=True)
    xc = x - mu
    var = jnp.mean(xc * xc, axis=-1, keepdims=True)
    return xc * lax.rsqrt(var + LN_EPS) * g + b


def _softmax_step(s, m_prev, l_prev, axis=1):
    m_new = jnp.maximum(m_prev, jnp.max(s, axis=axis, keepdims=True))
    a = jnp.exp2(m_prev - m_new)
    p = jnp.exp2(s - m_new)
    return m_new, a, p, a * l_prev + jnp.sum(p, axis=axis, keepdims=True)


def _merge_streams(ms, ls, accs):
    m = functools.reduce(jnp.maximum, ms)
    ws = [jnp.exp2(mi - m) for mi in ms]
    l = functools.reduce(lambda a, b: a + b, [w * li for w, li in zip(ws, ls)])
    acc = functools.reduce(lambda a, b: a + b, [w * ai for w, ai in zip(ws, accs)])
    return m, l, acc


def _score_keys(s):
    b = lax.bitcast_convert_type(s, I32)
    return jnp.where(s == 0.0, 0, b ^ ((b >> 31) & 0x7FFFFFFF))


def _key_to_score(k, shape):
    return lax.bitcast_convert_type(jnp.broadcast_to(k ^ ((k >> 31) & 0x7FFFFFFF), shape), F32)


def _rope_tables(pos, rot_dim, period, offset, width):
    half = rot_dim // 2
    inv = ROPE_THETA ** (-jnp.arange(half, dtype=jnp.float32) / half)
    ang = pos.astype(jnp.float32)[:, None] * inv
    c, s = jnp.cos(ang), jnp.sin(ang)
    t = pos.shape[0]
    cg = jnp.ones((t, period), F32).at[:, offset:offset + rot_dim].set(jnp.concatenate([c, c], 1))
    sg = jnp.zeros((t, period), F32).at[:, offset:offset + rot_dim].set(jnp.concatenate([s, s], 1))
    return jnp.tile(cg, (1, width // period)), jnp.tile(sg, (1, width // period))


def _rot_cols(w, half):
    return jnp.concatenate([-w[..., half:2 * half], w[..., :half]], axis=-1)


def _rot_heads(w, heads, dim, rot):
    w3 = w.reshape(w.shape[0], heads, dim)
    r = jnp.concatenate([_rot_cols(w3[..., :rot], rot // 2), jnp.zeros_like(w3[..., rot:])], axis=-1)
    return r.reshape(w.shape[0], heads * dim)


def _mla_front(x_ref, w1_ref, w2_ref, qn_ref, kvn_ref, cos_ref, sin_ref):
    xb = x_ref[...].astype(BF16)
    y = _dot(xb, w1_ref[...])
    cqn = _rms(y[:, :Q_LORA], qn_ref[...]).astype(BF16)
    ckv = _rms(y[:, Q_LORA:Q_LORA + KV_LORA], kvn_ref[...])
    qmem = y[:, 640:896]
    cos = cos_ref[...]
    sin = sin_ref[...]
    krp = y[:, 896:1024] * cos + y[:, 1024:1152] * sin
    q2 = _dot(cqn, w2_ref[...])
    return q2, ckv, qmem, krp, cos, sin


def _mla_proj_prompt_kernel(x_ref, w1_ref, w2_ref, wk_ref, wvt_ref, qn_ref, kvn_ref, cos_ref, sin_ref,
                            q_ref, k_ref, vt_ref, ckv_ref, kr_ref, qm_ref):
    q2, ckv, qmem, krp, cos, sin = _mla_front(x_ref, w1_ref, w2_ref, qn_ref, kvn_ref, cos_ref, sin_ref)
    hw = N_HEADS * LANE
    for h in range(N_HEADS):
        qh = q2[:, h * LANE:(h + 1) * LANE] * cos + q2[:, hw + h * LANE:hw + (h + 1) * LANE] * sin
        q_ref[h] = (qh * MLA_QSCALE).astype(BF16)
    ckv_ref[...] = ckv
    kr_ref[...] = krp
    qm_ref[...] = (qmem * HEAD_SCALE).astype(BF16)
    cb = ckv.astype(BF16)
    kn = _dot(cb, wk_ref[...])
    for h in range(N_HEADS):
        k_ref[h] = (kn[:, h * LANE:(h + 1) * LANE] + krp).astype(BF16)
    vt_ref[...] = _dot_nt(wvt_ref[...], cb).astype(BF16)


def _mla_proj_sample_kernel(x_ref, w1_ref, w2_ref, wukt_ref, qn_ref, kvn_ref, cos_ref, sin_ref,
                            cosr_ref, sinr_ref, qlat_ref, qr_ref, ckv_ref, kr_ref, qm_ref):
    q2, ckv, qmem, krp, _, _ = _mla_front(x_ref, w1_ref, w2_ref, qn_ref, kvn_ref, cos_ref, sin_ref)
    hw = N_HEADS * LANE
    rw = N_HEADS * QK_ROPE
    for h in range(N_HEADS):
        qh = (q2[:, h * LANE:(h + 1) * LANE] * MLA_QSCALE).astype(BF16)
        qlat_ref[:, h * KV_LORA:(h + 1) * KV_LORA] = _dot(qh, wukt_ref[h]).astype(BF16)
    qr = q2[:, hw:hw + rw] * cosr_ref[...] + q2[:, hw + rw:hw + 2 * rw] * sinr_ref[...]
    qr_ref[...] = (qr * MLA_QSCALE).astype(BF16)
    ckv_ref[...] = ckv
    kr_ref[...] = krp
    qm_ref[...] = (qmem * HEAD_SCALE).astype(BF16)


def _mla_w1(w_in):
    w_cq, w_ckv = w_in[:, :Q_LORA], w_in[:, Q_LORA:Q_LORA + KV_LORA]
    w_kr = w_in[:, Q_LORA + KV_LORA:Q_LORA + KV_LORA + QK_ROPE]
    w_qm = w_in[:, Q_LORA + KV_LORA + QK_ROPE:]
    z = lambda n: jnp.zeros((D_MODEL, n), F32)
    kr_pad = jnp.concatenate([z(QK_NOPE), w_kr, z(LANE - QK_NOPE - QK_ROPE)], 1)
    kr_rot = jnp.concatenate([z(QK_NOPE), _rot_cols(w_kr, QK_ROPE // 2), z(LANE - QK_NOPE - QK_ROPE)], 1)
    return jnp.concatenate([w_cq, w_ckv, w_qm, kr_pad, kr_rot], 1).astype(BF16)


def _mla_proj_prompt(x, w_in, q_norm, kv_norm, w_uq, w_uk, w_uv, cos, sin, tm):
    n = x.shape[0]
    w1 = _mla_w1(w_in)
    zq = jnp.zeros((Q_LORA, N_HEADS, LANE - QK_NOPE - QK_ROPE), F32)
    uq_pad = jnp.concatenate([w_uq, zq], -1).reshape(Q_LORA, N_HEADS * LANE)
    uq_rot = jnp.concatenate([jnp.zeros((Q_LORA, N_HEADS, QK_NOPE), F32),
                              _rot_cols(w_uq[..., QK_NOPE:], QK_ROPE // 2), zq], -1).reshape(Q_LORA, N_HEADS * LANE)
    w2 = jnp.concatenate([uq_pad, uq_rot], 1).astype(BF16)
    wk = jnp.concatenate([w_uk, jnp.zeros((KV_LORA, N_HEADS, LANE - QK_NOPE), F32)], -1)
    wk = wk.reshape(KV_LORA, N_HEADS * LANE).astype(BF16)
    wvt = w_uv.reshape(KV_LORA, N_HEADS * HEAD_DIM).T.astype(BF16)
    nt = cos.shape[0] // tm
    full = lambda a: pl.BlockSpec(a.shape, lambda i: (0,) * a.ndim)
    rows = lambda w: pl.BlockSpec((tm, w), lambda i: (i, 0))
    heads = pl.BlockSpec((N_HEADS, tm, LANE), lambda i: (0, i, 0))
    tab = pl.BlockSpec((tm, LANE), lambda i: (i % nt, 0))
    qn, kvn = q_norm.reshape(1, -1), kv_norm.reshape(1, -1)
    return pl.pallas_call(
        _mla_proj_prompt_kernel,
        grid=(n // tm,),
        in_specs=[rows(D_MODEL), full(w1), full(w2), full(wk), full(wvt), full(qn), full(kvn), tab, tab],
        out_specs=[heads, heads, pl.BlockSpec((N_HEADS * HEAD_DIM, tm), lambda i: (0, i)), rows(KV_LORA), rows(LANE),
                   rows(MEM_HEADS * HEAD_DIM)],
        out_shape=[jax.ShapeDtypeStruct((N_HEADS, n, LANE), BF16), jax.ShapeDtypeStruct((N_HEADS, n, LANE), BF16),
                   jax.ShapeDtypeStruct((N_HEADS * HEAD_DIM, n), BF16), jax.ShapeDtypeStruct((n, KV_LORA), F32),
                   jax.ShapeDtypeStruct((n, LANE), F32), jax.ShapeDtypeStruct((n, MEM_HEADS * HEAD_DIM), BF16)],
        compiler_params=_cp(("parallel",)),
        name="mla_proj_prompt",
    )(x, w1, w2, wk, wvt, qn, kvn, cos, sin)


def _mla_proj_sample(x, w_in, q_norm, kv_norm, w_uq, w_uk, cos, sin, cosr, sinr, tm):
    n = x.shape[0]
    w1 = _mla_w1(w_in)
    zq = jnp.zeros((Q_LORA, N_HEADS, LANE - QK_NOPE - QK_ROPE), F32)
    uq_pad = jnp.concatenate([w_uq, zq], -1).reshape(Q_LORA, N_HEADS * LANE)
    w_qr = w_uq[..., QK_NOPE:]
    w2 = jnp.concatenate([uq_pad, w_qr.reshape(Q_LORA, -1),
                          _rot_cols(w_qr, QK_ROPE // 2).reshape(Q_LORA, -1)], 1).astype(BF16)
    wukt = jnp.transpose(w_uk, (1, 2, 0))
    wukt = jnp.concatenate([wukt, jnp.zeros((N_HEADS, LANE - QK_NOPE, KV_LORA), F32)], 1).astype(BF16)
    full = lambda a: pl.BlockSpec(a.shape, lambda i: (0,) * a.ndim)
    rows = lambda w: pl.BlockSpec((tm, w), lambda i: (i, 0))
    qn, kvn = q_norm.reshape(1, -1), kv_norm.reshape(1, -1)
    return pl.pallas_call(
        _mla_proj_sample_kernel,
        grid=(n // tm,),
        in_specs=[rows(D_MODEL), full(w1), full(w2), full(wukt), full(qn), full(kvn), rows(LANE), rows(LANE),
                  rows(N_HEADS * QK_ROPE), rows(N_HEADS * QK_ROPE)],
        out_specs=[rows(N_HEADS * KV_LORA), rows(N_HEADS * QK_ROPE), rows(KV_LORA), rows(LANE),
                   rows(MEM_HEADS * HEAD_DIM)],
        out_shape=[jax.ShapeDtypeStruct((n, N_HEADS * KV_LORA), BF16),
                   jax.ShapeDtypeStruct((n, N_HEADS * QK_ROPE), BF16), jax.ShapeDtypeStruct((n, KV_LORA), F32),
                   jax.ShapeDtypeStruct((n, LANE), F32), jax.ShapeDtypeStruct((n, MEM_HEADS * HEAD_DIM), BF16)],
        compiler_params=_cp(("parallel",)),
        name="mla_proj_sample",
    )(x, w1, w2, wukt, qn, kvn, cos, sin, cosr, sinr)


def _causal_steps(nq, tq, tk):
    qs, ks = [], []
    for qi in range(nq):
        for ki in range(((qi + 1) * tq - 1) // tk + 1):
            qs.append(qi)
            ks.append(ki)
    return jnp.asarray(qs, I32), jnp.asarray(ks, I32)


def _mla_flash_kernel(qt_ref, kt_ref, q_ref, k_ref, vt_ref, o_ref, m_ref, l_ref, acc_ref, *, tq, tk):
    qi = qt_ref[pl.program_id(2)]
    ki = kt_ref[pl.program_id(2)]
    last = ((qi + 1) * tq - 1) // tk

    @pl.when(ki == 0)
    def _():
        m_ref[...] = jnp.full(m_ref.shape, NEG, F32)
        l_ref[...] = jnp.zeros(l_ref.shape, F32)
        acc_ref[...] = jnp.zeros(acc_ref.shape, F32)

    def step(masked):
        ss = [_dot_nt(k_ref[hh], q_ref[hh]) for hh in range(2)]
        if masked:
            key = ki * tk + lax.broadcasted_iota(I32, (tk, tq), 0)
            qry = qi * tq + lax.broadcasted_iota(I32, (tk, tq), 1)
            causal = key <= qry
        for hh in range(2):
            s = jnp.where(causal, ss[hh], NEG) if masked else ss[hh]
            m_new, a, p, l_new = _softmax_step(s, m_ref[hh], l_ref[hh], axis=0)
            vt = vt_ref[hh * HEAD_DIM:(hh + 1) * HEAD_DIM, :]
            acc_ref[hh] = a * acc_ref[hh] + _dot(vt, p.astype(BF16))
            m_ref[hh] = m_new
            l_ref[hh] = l_new

    first_masked = (qi * tq + 1) // tk

    @pl.when(ki < first_masked)
    def _():
        step(False)

    @pl.when(ki >= first_masked)
    def _():
        step(True)

    @pl.when(ki == last)
    def _():
        ot = jnp.concatenate([acc_ref[0] / l_ref[0], acc_ref[1] / l_ref[1]], axis=0)
        o_ref[...] = ot.T.astype(o_ref.dtype)


def _mla_flash(q, k, vt, batch, tq, tk):
    n = q.shape[1]
    s = n // batch
    nq, nk = s // tq, s // tk
    qtab, ktab = _causal_steps(nq, tq, tk)
    grid_spec = pltpu.PrefetchScalarGridSpec(
        num_scalar_prefetch=2,
        grid=(batch, N_HEADS // 2, qtab.shape[0]),
        in_specs=[pl.BlockSpec((2, tq, LANE), lambda b, hp, t, qt, kt: (hp, b * nq + qt[t], 0)),
                  pl.BlockSpec((2, tk, LANE), lambda b, hp, t, qt, kt: (hp, b * nk + kt[t], 0)),
                  pl.BlockSpec((2 * HEAD_DIM, tk), lambda b, hp, t, qt, kt: (hp, b * nk + kt[t]))],
        out_specs=pl.BlockSpec((tq, LANE), lambda b, hp, t, qt, kt: (b * nq + qt[t], hp)),
        scratch_shapes=[pltpu.VMEM((2, 1, tq), F32), pltpu.VMEM((2, 1, tq), F32), pltpu.VMEM((2, HEAD_DIM, tq), F32)],
    )
    return pl.pallas_call(
        functools.partial(_mla_flash_kernel, tq=tq, tk=tk),
        grid_spec=grid_spec,
        out_shape=jax.ShapeDtypeStruct((n, N_HEADS * HEAD_DIM), BF16),
        compiler_params=_cp(("parallel", "parallel", "arbitrary")),
        name="mla_flash",
    )(qtab, ktab, q, k, vt)


def _mla_decode_kernel(pt_ref, qlat_ref, qr_ref, cn_ref, kn_ref, *rest, pp, ns, t_new):
    ck_refs, krt_refs = rest[:pp], rest[pp:2 * pp]
    o_ref, m_ref, l_ref, acc_ref = rest[2 * pp:]
    p_id = pl.program_id(1)
    rows = qlat_ref.shape[1]
    per = pp // ns

    @pl.when(p_id == 0)
    def _():
        m_ref[...] = jnp.full(m_ref.shape, NEG, F32)
        l_ref[...] = jnp.zeros(l_ref.shape, F32)
        acc_ref[...] = jnp.zeros(acc_ref.shape, F32)

    qlat = qlat_ref[0]
    qr = qr_ref[0]
    for st in range(ns):
        pages = range(st * per, (st + 1) * per)
        cks = [ck_refs[i][...].astype(BF16) for i in pages]
        s = jnp.concatenate([_dot_nt(qlat, ck) + _dot(qr, krt_refs[i][...].astype(BF16))
                             for ck, i in zip(cks, pages)], axis=1)
        m_new, a, p, l_new = _softmax_step(s, m_ref[st], l_ref[st])
        pb = p.astype(BF16)
        pv = _dot(pb[:, :PAGE], cks[0])
        for i in range(1, per):
            pv = pv + _dot(pb[:, i * PAGE:(i + 1) * PAGE], cks[i])
        acc_ref[st] = a * acc_ref[st] + pv
        m_ref[st] = m_new
        l_ref[st] = l_new

    @pl.when(p_id == pl.num_programs(1) - 1)
    def _():
        m, l, acc = _merge_streams([m_ref[st] for st in range(ns)], [l_ref[st] for st in range(ns)],
                                   [acc_ref[st] for st in range(ns)])
        cn = cn_ref[0].astype(BF16)
        kn = kn_ref[0].astype(BF16)
        npad = cn.shape[0]
        t = lax.broadcasted_iota(I32, (rows, npad), 0) % t_new
        u = lax.broadcasted_iota(I32, (rows, npad), 1)
        s2 = jnp.where(u <= t, _dot_nt(qlat, cn) + _dot_nt(qr, kn), NEG)
        m2, a2, p2, l2 = _softmax_step(s2, m, l)
        o_ref[0] = (a2 * acc + _dot(p2.astype(BF16), cn)) / l2


def _mla_decode(page_table, qlat, qr, ckv_new, kr_new, pool_ckv, pool_krt, j, pp, ns, t_new):
    db, rows, _ = qlat.shape
    npg = page_table.shape[1]
    pt = page_table.reshape(-1)
    npad = ckv_new.shape[1]
    page = lambda i: (lambda b, p, pt_: (j, pt_[b * npg + p * pp + i], 0, 0))
    per_b = lambda r, w: pl.BlockSpec((1, r, w), lambda b, p, pt_: (b, 0, 0))
    grid_spec = pltpu.PrefetchScalarGridSpec(
        num_scalar_prefetch=1,
        grid=(db, npg // pp),
        in_specs=[per_b(rows, KV_LORA), per_b(rows, QK_ROPE), per_b(npad, KV_LORA), per_b(npad, QK_ROPE)]
        + [pl.BlockSpec((None, None, PAGE, KV_LORA), page(i)) for i in range(pp)]
        + [pl.BlockSpec((None, None, QK_ROPE, PAGE), page(i)) for i in range(pp)],
        out_specs=per_b(rows, KV_LORA),
        scratch_shapes=[pltpu.VMEM((ns, rows, 1), F32), pltpu.VMEM((ns, rows, 1), F32),
                        pltpu.VMEM((ns, rows, KV_LORA), F32)],
    )
    return pl.pallas_call(
        functools.partial(_mla_decode_kernel, pp=pp, ns=ns, t_new=t_new),
        grid_spec=grid_spec,
        out_shape=jax.ShapeDtypeStruct((db, rows, KV_LORA), F32),
        compiler_params=_cp(("parallel", "arbitrary")),
        name="mla_decode",
    )(pt, qlat, qr, ckv_new, kr_new, *([pool_ckv] * pp), *([pool_krt] * pp))


def _head_matmul_kernel(x_ref, w_ref, o_ref):
    o_ref[...] = _dot(x_ref[...].astype(BF16), w_ref[...]).astype(o_ref.dtype)


def _head_matmul(x, w):
    h, n, k = x.shape
    m = w.shape[2]
    return pl.pallas_call(
        _head_matmul_kernel,
        grid=(h,),
        in_specs=[pl.BlockSpec((None, n, k), lambda i: (i, 0, 0)), pl.BlockSpec((None, k, m), lambda i: (i, 0, 0))],
        out_specs=pl.BlockSpec((None, n, m), lambda i: (i, 0, 0)),
        out_shape=jax.ShapeDtypeStruct((h, n, m), BF16),
        compiler_params=_cp(("parallel",)),
        name="head_matmul",
    )(x, w)


def _dsa_proj_kernel(x_ref, wq_ref, wk_ref, wv_ref, wvt_ref, wwt_ref, wqi_ref, wm_ref, wki_ref, lng_ref, lnb_ref,
                     sgn_ref, gperm_ref, brot_ref, cos_ref, sin_ref,
                     q_ref, k_ref, kf_ref, vf_ref, vt_ref, wt_ref, qi_ref, kw_ref, kib_ref, qm_ref):
    tm = x_ref.shape[0]
    xb = x_ref[...].astype(BF16)
    cos = cos_ref[...]
    sin = sin_ref[...]
    lane = lax.broadcasted_iota(I32, (tm, LANE), 1)
    low = lane < HEAD_DIM

    def roped(w_ref, nblk):
        y = _dot(xb, w_ref[...])
        return [y[:, b * LANE:(b + 1) * LANE] * cos + y[:, (nblk + b) * LANE:(nblk + b + 1) * LANE] * sin
                for b in range(nblk)]

    qb = roped(wq_ref, N_HEADS // 2)
    for h in range(N_HEADS):
        g, r = h // 3, h % 3
        blk = qb[(g // 2) * 3 + r] * DSA_QSCALE
        q_ref[h] = jnp.where(low if g % 2 == 0 else jnp.logical_not(low), blk, 0.0).astype(BF16)
    kb = roped(wk_ref, KV_HEADS // 2)
    for b in range(KV_HEADS // 2):
        kf_ref[:, b * LANE:(b + 1) * LANE] = kb[b]
        k_ref[b] = kb[b].astype(BF16)
    vf_ref[...] = _dot(xb, wv_ref[...])
    vt_ref[...] = _dot_nt(wvt_ref[...], xb).astype(BF16)
    wt_ref[...] = _dot_nt(wwt_ref[...], xb)
    qib = roped(wqi_ref, IDX_HEADS // 2)
    for h in range(IDX_HEADS):
        qi_ref[h] = jnp.where(low if h % 2 == 0 else jnp.logical_not(low), qib[h // 2], 0.0).astype(BF16)
    qm_ref[...] = (_dot(xb, wm_ref[...]) * HEAD_SCALE).astype(BF16)
    yk = _dot(xb, wki_ref[...])
    c, c_rot, wpad = yk[:, :LANE], yk[:, LANE:2 * LANE], yk[:, 2 * LANE:3 * LANE]
    mu = jnp.sum(jnp.where(low, c, 0.0), axis=1, keepdims=True) * (1.0 / IDX_DIM)
    xc = c - mu
    var = jnp.sum(jnp.where(low, xc * xc, 0.0), axis=1, keepdims=True) * (1.0 / IDX_DIM)
    rstd = lax.rsqrt(var + LN_EPS)
    ki = xc * rstd * lng_ref[...] + lnb_ref[...]
    ki_rot = (c_rot - sgn_ref[...] * mu) * rstd * gperm_ref[...] + brot_ref[...]
    kir = ki * cos + ki_rot * sin
    kib_ref[...] = kir.astype(BF16)
    kw_ref[...] = jnp.where(low, kir, wpad)


def _dsa_proj(x, w_in, ln_g, ln_b, cos, sin, tm):
    n = x.shape[0]
    o = 0
    cols = []
    for wdt in (N_HEADS * HEAD_DIM, KV_HEADS * HEAD_DIM, KV_HEADS * HEAD_DIM, IDX_HEADS * IDX_DIM, IDX_DIM, IDX_HEADS,
                MEM_HEADS * HEAD_DIM):
        cols.append(w_in[:, o:o + wdt])
        o += wdt
    w_q, w_k, w_v, w_qi, w_ki, w_w, w_qm = cols
    perm = jnp.asarray(DSA_PERM)
    w_qp = w_q.reshape(D_MODEL, N_HEADS, HEAD_DIM)[:, perm].reshape(D_MODEL, -1)
    wq = jnp.concatenate([w_qp, _rot_heads(w_qp, N_HEADS, HEAD_DIM, ROT_DIM)], 1).astype(BF16)
    wk = jnp.concatenate([w_k, _rot_heads(w_k, KV_HEADS, HEAD_DIM, ROT_DIM)], 1).astype(BF16)
    wqi = jnp.concatenate([w_qi, _rot_heads(w_qi, IDX_HEADS, IDX_DIM, ROT_DIM)], 1).astype(BF16)
    w_ki_rot = _rot_heads(w_ki, 1, IDX_DIM, ROT_DIM)
    w_wpad = jnp.concatenate([jnp.zeros((D_MODEL, HEAD_DIM), F32), w_w,
                              jnp.zeros((D_MODEL, LANE - HEAD_DIM - IDX_HEADS), F32)], 1)
    wki = jnp.concatenate([w_ki, w_ki, w_ki_rot, w_ki_rot, w_wpad], 1).astype(BF16)
    wv, wm = w_v.astype(BF16), w_qm.astype(BF16)
    wvt = w_v.T.astype(BF16)
    wwt = jnp.concatenate([w_w.T, jnp.zeros((2 * SUBLANE - IDX_HEADS, D_MODEL), F32)], 0).astype(BF16)
    half = ROT_DIM // 2
    zr = jnp.zeros((IDX_DIM - ROT_DIM,), F32)
    dup = lambda a: jnp.concatenate([a, a]).reshape(1, LANE)
    sgn = dup(jnp.concatenate([-jnp.ones((half,), F32), jnp.ones((half,), F32), zr]))
    gperm = dup(jnp.concatenate([ln_g[half:ROT_DIM], ln_g[:half], zr]))
    brot = dup(jnp.concatenate([-ln_b[half:ROT_DIM], ln_b[:half], zr]))
    lng, lnb = dup(ln_g), dup(ln_b)
    nt = cos.shape[0] // tm
    full = lambda a: pl.BlockSpec(a.shape, lambda i: (0,) * a.ndim)
    rows = lambda w: pl.BlockSpec((tm, w), lambda i: (i, 0))
    heads = lambda h: pl.BlockSpec((h, tm, LANE), lambda i: (0, i, 0))
    colsp = lambda r: pl.BlockSpec((r, tm), lambda i: (0, i))
    tab = pl.BlockSpec((tm, LANE), lambda i: (i % nt, 0))
    sds = jax.ShapeDtypeStruct
    return pl.pallas_call(
        _dsa_proj_kernel,
        grid=(n // tm,),
        in_specs=[rows(D_MODEL), full(wq), full(wk), full(wv), full(wvt), full(wwt), full(wqi), full(wm), full(wki),
                  full(lng), full(lnb), full(sgn), full(gperm), full(brot), tab, tab],
        out_specs=[heads(N_HEADS), heads(KV_HEADS // 2), rows(2 * LANE), rows(2 * LANE), colsp(2 * LANE),
                   colsp(2 * SUBLANE), heads(IDX_HEADS), rows(LANE), rows(LANE), rows(2 * LANE)],
        out_shape=[sds((N_HEADS, n, LANE), BF16), sds((KV_HEADS // 2, n, LANE), BF16), sds((n, 2 * LANE), F32),
                   sds((n, 2 * LANE), F32), sds((2 * LANE, n), BF16), sds((2 * SUBLANE, n), F32),
                   sds((IDX_HEADS, n, LANE), BF16), sds((n, LANE), F32), sds((n, LANE), BF16),
                   sds((n, 2 * LANE), BF16)],
        compiler_params=_cp(("parallel",), 56),
        name="dsa_proj",
    )(x, wq, wk, wv, wvt, wwt, wqi, wm, wki, lng, lnb, sgn, gperm, brot, cos, sin)


def _indexer_prompt_kernel(qt_ref, kt_ref, qi_ref, ki_ref, wt_ref, sc_ref, *, tq, tk):
    qi = qt_ref[pl.program_id(1)]
    ki = kt_ref[pl.program_id(1)]
    kb = ki_ref[...]
    w = wt_ref[...]
    acc = None
    for h in range(IDX_HEADS):
        d = jnp.maximum(_dot_nt(kb, qi_ref[h]), 0.0) * w[h:h + 1]
        acc = d if acc is None else acc + d
    key = ki * tk + lax.broadcasted_iota(I32, (tk, tq), 0)
    qry = qi * tq + lax.broadcasted_iota(I32, (tk, tq), 1)
    sc_ref[...] = jnp.where(key <= qry, acc, NEG)


def _indexer_prompt(qidx, kib, wt, batch, tq, tk):
    n = kib.shape[0]
    s = n // batch
    nq, nk = s // tq, s // tk
    qtab, ktab = _causal_steps(nq, tq, tk)
    grid_spec = pltpu.PrefetchScalarGridSpec(
        num_scalar_prefetch=2,
        grid=(batch, qtab.shape[0]),
        in_specs=[pl.BlockSpec((IDX_HEADS, tq, LANE), lambda b, t, qt, kt: (0, b * nq + qt[t], 0)),
                  pl.BlockSpec((tk, LANE), lambda b, t, qt, kt: (b * nk + kt[t], 0)),
                  pl.BlockSpec((2 * SUBLANE, tq), lambda b, t, qt, kt: (0, b * nq + qt[t]))],
        out_specs=pl.BlockSpec((tk, tq), lambda b, t, qt, kt: (b * nk + kt[t], qt[t])),
    )
    return pl.pallas_call(
        functools.partial(_indexer_prompt_kernel, tq=tq, tk=tk),
        grid_spec=grid_spec,
        out_shape=jax.ShapeDtypeStruct((n, s), F32),
        compiler_params=_cp(("parallel", "arbitrary")),
        name="indexer_prompt",
    )(qtab, ktab, qidx, kib, wt)


def _indexer_sample_kernel(pt_ref, q_ref, w_ref, kn_ref, *rest, pp, t_new):
    kt_refs = rest[:pp]
    sc_ref, scn_ref = rest[pp:]
    q = q_ref[0]
    w = w_ref[0]

    def weighted(d):
        acc = None
        for h in range(IDX_HEADS):
            t = d[h * t_new:(h + 1) * t_new] * w[:, h:h + 1]
            acc = t if acc is None else acc + t
        return acc

    for i in range(pp):
        sc_ref[0, :, i * PAGE:(i + 1) * PAGE] = weighted(jnp.maximum(_dot(q, kt_refs[i][...].astype(BF16)), 0.0))

    @pl.when(pl.program_id(1) == pl.num_programs(1) - 1)
    def _():
        sn = weighted(jnp.maximum(_dot_nt(q, kn_ref[0].astype(BF16)), 0.0))
        npad = sn.shape[1]
        t = lax.broadcasted_iota(I32, (t_new, npad), 0)
        u = lax.broadcasted_iota(I32, (t_new, npad), 1)
        scn_ref[0] = jnp.full((t_new, LANE), NEG, F32)
        scn_ref[0, :, :npad] = jnp.where(u <= t, sn, NEG)


def _indexer_sample(page_table, q, w, kidx_new, pool_kidxt, j, pp, t_new):
    db = q.shape[0]
    npg = page_table.shape[1]
    pt = page_table.reshape(-1)
    npad = kidx_new.shape[1]
    per_b = lambda r, c: pl.BlockSpec((1, r, c), lambda b, p, pt_: (b, 0, 0))
    pool = lambda i: pl.BlockSpec((None, None, IDX_DIM, PAGE), lambda b, p, pt_: (j, pt_[b * npg + p * pp + i], 0, 0))
    grid_spec = pltpu.PrefetchScalarGridSpec(
        num_scalar_prefetch=1,
        grid=(db, npg // pp),
        in_specs=[per_b(IDX_HEADS * t_new, IDX_DIM), per_b(t_new, IDX_HEADS), per_b(npad, IDX_DIM)]
        + [pool(i) for i in range(pp)],
        out_specs=[pl.BlockSpec((1, t_new, pp * PAGE), lambda b, p, pt_: (b, 0, p)), per_b(t_new, LANE)],
    )
    return pl.pallas_call(
        functools.partial(_indexer_sample_kernel, pp=pp, t_new=t_new),
        grid_spec=grid_spec,
        out_shape=[jax.ShapeDtypeStruct((db, t_new, npg * PAGE), F32), jax.ShapeDtypeStruct((db, t_new, LANE), F32)],
        compiler_params=_cp(("parallel", "arbitrary")),
        name="indexer_sample",
    )(pt, q, w, kidx_new, *([pool_kidxt] * pp))


def _select_rows_kernel(sc_ref, thr_ref, cut_ref, key_ref, *, topk, cw, tq, idx_bits):
    nch = sc_ref.shape[1] // cw

    def make_keys(c, carry):
        off = pl.multiple_of(c * cw, cw)
        key_ref[:, pl.ds(off, cw)] = _score_keys(sc_ref[:, pl.ds(off, cw)])
        return carry

    lax.fori_loop(0, nch, make_keys, 0)
    lane = lax.broadcasted_iota(I32, (tq, LANE), 1)

    def count(pred):
        def body(c, acc):
            off = pl.multiple_of(c * cw, cw)
            for u in range(cw // LANE):
                kk = key_ref[:, pl.ds(off + u * LANE, LANE)]
                acc = acc + jnp.where(pred(kk, off + u * LANE + lane), 1.0, 0.0)
            return acc
        acc = lax.fori_loop(0, nch, body, jnp.zeros((tq, LANE), F32))
        return jnp.sum(acc, axis=1, keepdims=True)

    thrk = _kth_key(count, (tq, 1), topk)
    cut = _tie_cut(count, thrk, (tq, 1), topk, idx_bits)
    thr_ref[...] = _key_to_score(thrk, (tq, LANE))
    cut_ref[...] = jnp.broadcast_to(cut, (tq, LANE))


def _kth_key(count, shape, topk):
    def value_bit(it, ans):
        cand = ans | jnp.left_shift(jnp.int32(1), 31 - it)
        return jnp.where(count(lambda kk, idx: kk >= (cand ^ MIN_I32)) >= topk, cand, ans)

    return lax.fori_loop(0, 32, value_bit, jnp.zeros(shape, I32)) ^ MIN_I32


def _tie_cut(count, thrk, shape, topk, idx_bits):
    n_gt = count(lambda kk, idx: kk > thrk)
    n_ge = count(lambda kk, idx: kk >= thrk)
    need = topk - n_gt

    def tie_bit(it, m):
        cand = m | jnp.left_shift(jnp.int32(1), idx_bits - 1 - it)
        cnt = count(lambda kk, idx: jnp.where(kk == thrk, idx, cand) < cand)
        return jnp.where(cnt < need, cand, m)

    return lax.cond(jnp.max(jnp.abs(n_ge - topk)) > 0.0,
                    lambda: lax.fori_loop(0, idx_bits, tie_bit, jnp.zeros(shape, I32)),
                    lambda: jnp.full(shape, 2 ** 30, I32))


def _select_rows(sc, topk, cw, tq):
    n, length = sc.shape
    idx_bits = max(1, (length - 1).bit_length())
    return pl.pallas_call(
        functools.partial(_select_rows_kernel, topk=topk, cw=cw, tq=tq, idx_bits=idx_bits),
        grid=(n // tq,),
        in_specs=[pl.BlockSpec((tq, length), lambda i: (i, 0))],
        out_specs=[pl.BlockSpec((tq, LANE), lambda i: (i, 0)), pl.BlockSpec((tq, LANE), lambda i: (i, 0))],
        out_shape=[jax.ShapeDtypeStruct((n, LANE), F32), jax.ShapeDtypeStruct((n, LANE), I32)],
        scratch_shapes=[pltpu.VMEM((tq, length), I32)],
        compiler_params=_cp(("parallel",)),
        name="topk_select_rows",
    )(sc)


def _select_cols_kernel(sc_ref, thr_ref, cut_ref, key_ref, hi_ref, lo_ref, *, topk, cw, idx_bits, nqt):
    qt = pl.program_id(0) % nqt
    nch = (jnp.maximum((qt + 1) * LANE, topk) + cw - 1) // cw
    pk = 2 * SUBLANE
    half = 2 ** 15

    def make_keys(c, carry):
        off = pl.multiple_of(c * cw, cw)
        k = _score_keys(sc_ref[pl.ds(off, cw), :])
        key_ref[pl.ds(off, cw), :] = k
        hi_ref[pl.ds(off, cw), :] = (k >> 16).astype(I16)
        lo_ref[pl.ds(off, cw), :] = ((k & 0xFFFF) - half).astype(I16)
        return carry

    lax.fori_loop(0, nch, make_keys, 0)
    sub = lax.broadcasted_iota(I32, (SUBLANE, LANE), 0)

    def count(pred):
        def body(c, accs):
            off = pl.multiple_of(c * cw, cw)
            accs = list(accs)
            blk = key_ref[pl.ds(off, cw), :]
            for u in range(cw // SUBLANE):
                kk = blk[u * SUBLANE:(u + 1) * SUBLANE]
                accs[u % N_ACC] = accs[u % N_ACC] + jnp.where(pred(kk, off + u * SUBLANE + sub), 1.0, 0.0)
            return tuple(accs)
        accs = lax.fori_loop(0, nch, body, tuple(jnp.zeros((SUBLANE, LANE), F32) for _ in range(N_ACC)))
        return jnp.sum(functools.reduce(lambda a, b: a + b, accs), axis=0, keepdims=True)

    def count16(ref, pred):
        def body(c, accs):
            off = pl.multiple_of(c * cw, cw)
            accs = list(accs)
            blk = ref[pl.ds(off, cw), :]
            for u in range(cw // pk):
                kk = blk[u * pk:(u + 1) * pk]
                accs[u % N_ACC] = accs[u % N_ACC] + jnp.where(pred(kk), jnp.int16(1), jnp.int16(0))
            return tuple(accs)
        accs = lax.fori_loop(0, nch, body, tuple(jnp.zeros((pk, LANE), I16) for _ in range(N_ACC)))
        tot = functools.reduce(lambda a, b: a + b, [a.astype(F32) for a in accs])
        return jnp.sum(tot, axis=0, keepdims=True)

    def packed(v):
        return jnp.broadcast_to(v - half, (pk, LANE)).astype(I16)

    def search16(ref, want):
        def bit(it, ans):
            cand = ans | jnp.left_shift(jnp.int32(1), 15 - it)
            cb = packed(cand)
            return jnp.where(count16(ref, lambda kk: kk >= cb) >= want, cand, ans)
        return lax.fori_loop(0, 16, bit, jnp.zeros((1, LANE), I32))

    hi = search16(hi_ref, topk)
    hb = packed(hi)
    want_lo = topk - count16(hi_ref, lambda kk: kk > hb)

    def mask_lo(c, carry):
        rows = pl.ds(pl.multiple_of(c * cw, cw), cw)
        hi_blk, lo_blk = hi_ref[rows, :], lo_ref[rows, :]
        lo_ref[rows, :] = jnp.concatenate(
            [jnp.where(hi_blk[u * pk:(u + 1) * pk] == hb, lo_blk[u * pk:(u + 1) * pk], jnp.int16(-half))
             for u in range(cw // pk)], axis=0)
        return carry

    lax.fori_loop(0, nch, mask_lo, 0)
    lo = search16(lo_ref, want_lo)
    thrk = (jnp.left_shift(hi, 16) | lo) ^ MIN_I32
    cut = _tie_cut(count, thrk, (1, LANE), topk, idx_bits)
    thr_ref[...] = _key_to_score(thrk, (SUBLANE, LANE))
    cut_ref[...] = jnp.broadcast_to(cut, (SUBLANE, LANE))


def _select_cols(sct, batch, topk, cw):
    n, s = sct.shape
    nqt = s // LANE
    idx_bits = max(1, (s - 1).bit_length())
    out = pl.BlockSpec((SUBLANE, LANE), lambda i: (0, i))
    return pl.pallas_call(
        functools.partial(_select_cols_kernel, topk=topk, cw=cw, idx_bits=idx_bits, nqt=nqt),
        grid=(batch * nqt,),
        in_specs=[pl.BlockSpec((s, LANE), lambda i: (i // nqt, i % nqt))],
        out_specs=[out, out],
        out_shape=[jax.ShapeDtypeStruct((SUBLANE, n), F32), jax.ShapeDtypeStruct((SUBLANE, n), I32)],
        scratch_shapes=[pltpu.VMEM((s, LANE), I32), pltpu.VMEM((s, LANE), I16), pltpu.VMEM((s, LANE), I16)],
        compiler_params=_cp(("parallel",)),
        name="topk_select_cols",
    )(sct)


def _dsa_attn_kernel(qt_ref, kt_ref, q_ref, k_ref, vt_ref, sc_ref, thr_ref, cut_ref, o_ref, m_ref, l_ref, acc_ref,
                     *, tq, tk):
    qi = qt_ref[pl.program_id(1)]
    ki = kt_ref[pl.program_id(1)]
    last = ((qi + 1) * tq - 1) // tk

    @pl.when(ki == 0)
    def _():
        m_ref[...] = jnp.full(m_ref.shape, NEG, F32)
        l_ref[...] = jnp.zeros(l_ref.shape, F32)
        acc_ref[...] = jnp.zeros(acc_ref.shape, F32)

    key = ki * tk + lax.broadcasted_iota(I32, (tk, tq), 0)
    qry = qi * tq + lax.broadcasted_iota(I32, (tk, tq), 1)
    sc = sc_ref[...]
    thr = thr_ref[0:1, :]
    cut = cut_ref[0:1, :]
    sel = jnp.where(sc > thr, 1.0, jnp.where(sc == thr, jnp.where(key <= cut, 1.0, 0.0), 0.0))
    keep = jnp.where(key <= qry, sel, 0.0)
    keep3 = jnp.concatenate([keep, keep, keep], axis=1)
    ss = [_dot_nt(k_ref[g // 2], q_ref[3 * g:3 * g + 3].reshape(3 * tq, LANE)) for g in range(KV_HEADS)]
    for g in range(KV_HEADS):
        s = jnp.where(keep3 > 0.0, ss[g], NEG)
        m_new, a, p, l_new = _softmax_step(s, m_ref[g], l_ref[g], axis=0)
        acc_ref[g] = a * acc_ref[g] + _dot(vt_ref[g * HEAD_DIM:(g + 1) * HEAD_DIM, :], p.astype(BF16))
        m_ref[g] = m_new
        l_ref[g] = l_new

    @pl.when(ki == last)
    def _():
        for gp in range(KV_HEADS // 2):
            oe = acc_ref[2 * gp] / l_ref[2 * gp]
            oo = acc_ref[2 * gp + 1] / l_ref[2 * gp + 1]
            for r in range(3):
                blk = gp * 3 + r
                ot = jnp.concatenate([oe[:, r * tq:(r + 1) * tq], oo[:, r * tq:(r + 1) * tq]], axis=0)
                o_ref[:, blk * LANE:(blk + 1) * LANE] = ot.T.astype(o_ref.dtype)


def _dsa_attn(q, k, vt, sct, thr, cut, batch, tq, tk):
    n = q.shape[1]
    s = n // batch
    nq, nk = s // tq, s // tk
    qtab, ktab = _causal_steps(nq, tq, tk)
    qcol = pl.BlockSpec((SUBLANE, tq), lambda b, t, qt, kt: (0, b * nq + qt[t]))
    grid_spec = pltpu.PrefetchScalarGridSpec(
        num_scalar_prefetch=2,
        grid=(batch, qtab.shape[0]),
        in_specs=[pl.BlockSpec((N_HEADS, tq, LANE), lambda b, t, qt, kt: (0, b * nq + qt[t], 0)),
                  pl.BlockSpec((KV_HEADS // 2, tk, LANE), lambda b, t, qt, kt: (0, b * nk + kt[t], 0)),
                  pl.BlockSpec((KV_HEADS * HEAD_DIM, tk), lambda b, t, qt, kt: (0, b * nk + kt[t])),
                  pl.BlockSpec((tk, tq), lambda b, t, qt, kt: (b * nk + kt[t], qt[t])),
                  qcol, qcol],
        out_specs=pl.BlockSpec((tq, N_HEADS * HEAD_DIM), lambda b, t, qt, kt: (b * nq + qt[t], 0)),
        scratch_shapes=[pltpu.VMEM((KV_HEADS, 1, 3 * tq), F32), pltpu.VMEM((KV_HEADS, 1, 3 * tq), F32),
                        pltpu.VMEM((KV_HEADS, HEAD_DIM, 3 * tq), F32)],
    )
    return pl.pallas_call(
        functools.partial(_dsa_attn_kernel, tq=tq, tk=tk),
        grid_spec=grid_spec,
        out_shape=jax.ShapeDtypeStruct((n, N_HEADS * HEAD_DIM), BF16),
        compiler_params=_cp(("parallel", "arbitrary"), 56),
        name="dsa_attn",
    )(qtab, ktab, q, k, vt, sct, thr, cut)


def _dsa_decode_kernel(pt_ref, q_ref, sc_ref, scn_ref, thr_ref, cut_ref, knt_ref, vn_ref, *rest, pp, ns, t_new, past):
    kt_refs, vt_refs = rest[:pp], rest[pp:2 * pp]
    o_ref, m_ref, l_ref, acc_ref = rest[2 * pp:]
    p_id = pl.program_id(1)
    rows = q_ref.shape[1]
    reps = rows // t_new
    per = pp // ns
    width = per * PAGE

    @pl.when(p_id == 0)
    def _():
        m_ref[...] = jnp.full(m_ref.shape, NEG, F32)
        l_ref[...] = jnp.zeros(l_ref.shape, F32)
        acc_ref[...] = jnp.zeros(acc_ref.shape, F32)

    thr = thr_ref[0][:, :1]
    cut = cut_ref[0][:, :1]
    q = q_ref[0]

    def keep_rows(sc, col, extra):
        sel = jnp.where(sc > thr, 1.0, jnp.where(sc == thr, jnp.where(col <= cut, 1.0, 0.0), 0.0))
        if extra is not None:
            sel = jnp.where(extra, sel, 0.0)
        return jnp.concatenate([sel] * reps, axis=0)

    for st in range(ns):
        pages = range(st * per, (st + 1) * per)
        col = p_id * (pp * PAGE) + st * width + lax.broadcasted_iota(I32, (t_new, width), 1)
        keep = keep_rows(sc_ref[0, :, st * width:(st + 1) * width], col, None)
        s = jnp.concatenate([_dot(q, kt_refs[i][...].astype(BF16)) for i in pages], axis=1)
        s = jnp.where(keep > 0.0, s, NEG)
        m_new, a, p, l_new = _softmax_step(s, m_ref[st], l_ref[st])
        pb = p.astype(BF16)
        pv = None
        for n_, i in enumerate(pages):
            t = _dot_nt(pb[:, n_ * PAGE:(n_ + 1) * PAGE], vt_refs[i][...].astype(BF16))
            pv = t if pv is None else pv + t
        acc_ref[st] = a * acc_ref[st] + pv
        m_ref[st] = m_new
        l_ref[st] = l_new

    @pl.when(p_id == pl.num_programs(1) - 1)
    def _():
        m, l, acc = _merge_streams([m_ref[st] for st in range(ns)], [l_ref[st] for st in range(ns)],
                                   [acc_ref[st] for st in range(ns)])
        npad = vn_ref.shape[1]
        t = lax.broadcasted_iota(I32, (t_new, npad), 0)
        u = lax.broadcasted_iota(I32, (t_new, npad), 1)
        keep_n = keep_rows(scn_ref[0][:, :npad], past + u, u <= t)
        s2 = jnp.where(keep_n > 0.0, _dot(q, knt_ref[0].astype(BF16)), NEG)
        m2, a2, p2, l2 = _softmax_step(s2, m, l)
        o_ref[0] = (a2 * acc + _dot(p2.astype(BF16), vn_ref[0].astype(BF16))) / l2


def _dsa_decode(page_table, q, sc, scn, thr, cut, knt, v_new, pool_kt, pool_vt, j, pp, ns, t_new):
    db, rows, wd = q.shape
    npg = page_table.shape[1]
    pt = page_table.reshape(-1)
    npad = v_new.shape[1]
    per_b = lambda r, c: pl.BlockSpec((1, r, c), lambda b, p, pt_: (b, 0, 0))
    pool = lambda i: pl.BlockSpec((None, None, wd, PAGE), lambda b, p, pt_: (j, pt_[b * npg + p * pp + i], 0, 0))
    grid_spec = pltpu.PrefetchScalarGridSpec(
        num_scalar_prefetch=1,
        grid=(db, npg // pp),
        in_specs=[per_b(rows, wd), pl.BlockSpec((1, t_new, pp * PAGE), lambda b, p, pt_: (b, 0, p)),
                  per_b(t_new, LANE), per_b(t_new, LANE), per_b(t_new, LANE), per_b(wd, npad), per_b(npad, wd)]
        + [pool(i) for i in range(pp)] + [pool(i) for i in range(pp)],
        out_specs=per_b(rows, wd),
        scratch_shapes=[pltpu.VMEM((ns, rows, 1), F32), pltpu.VMEM((ns, rows, 1), F32),
                        pltpu.VMEM((ns, rows, wd), F32)],
    )
    return pl.pallas_call(
        functools.partial(_dsa_decode_kernel, pp=pp, ns=ns, t_new=t_new, past=npg * PAGE),
        grid_spec=grid_spec,
        out_shape=jax.ShapeDtypeStruct((db, rows, wd), F32),
        compiler_params=_cp(("parallel", "arbitrary")),
        name="dsa_decode",
    )(pt, q, sc, scn, thr, cut, knt, v_new, *([pool_kt] * pp), *([pool_vt] * pp))


def _mem_kv_kernel(m_ref, w_ref, kf_ref, vf_ref, kb_ref, vb_ref):
    y = _dot(m_ref[...].astype(BF16), w_ref[...])
    w = MEM_HEADS * HEAD_DIM
    kf_ref[...] = y[:, :w]
    vf_ref[...] = y[:, w:]
    kb_ref[...] = y[:, :w].astype(BF16)
    vb_ref[...] = y[:, w:].astype(BF16)


def _mem_kv(mem2d, w):
    n = mem2d.shape[0]
    wd = MEM_HEADS * HEAD_DIM
    wb = w.astype(BF16)
    spec = pl.BlockSpec((n, wd), lambda i: (0, 0))
    return pl.pallas_call(
        _mem_kv_kernel,
        grid=(1,),
        in_specs=[pl.BlockSpec(mem2d.shape, lambda i: (0, 0)), pl.BlockSpec(wb.shape, lambda i: (0, 0))],
        out_specs=[spec] * 4,
        out_shape=[jax.ShapeDtypeStruct((n, wd), F32)] * 2 + [jax.ShapeDtypeStruct((n, wd), BF16)] * 2,
        compiler_params=_cp(("arbitrary",)),
        name="mem_kv",
    )(mem2d, wb)


def _mem_attend_kernel(q_ref, mk_ref, mv_ref, o_ref, *, bb, kv_t):
    tt = q_ref.shape[1]
    lane = lax.broadcasted_iota(I32, (tt, LANE), 1)
    low = lane < HEAD_DIM
    for b in range(bb):
        for blk in range(MEM_HEADS // 2):
            sl = slice(blk * LANE, (blk + 1) * LANE)
            qb = q_ref[b, :, sl]
            mkb = (mk_ref[b, sl, :] if kv_t else mk_ref[b, :, sl]).astype(BF16)
            mvb = (mv_ref[b, sl, :] if kv_t else mv_ref[b, :, sl]).astype(BF16)
            res = []
            for half in range(2):
                qh = jnp.where(low if half == 0 else jnp.logical_not(low), qb, jnp.zeros_like(qb))
                s = _dot(qh, mkb) if kv_t else _dot_nt(qh, mkb)
                p = jnp.exp(s - jnp.max(s, axis=1, keepdims=True))
                p = (p / jnp.sum(p, axis=1, keepdims=True)).astype(BF16)
                res.append(_dot_nt(p, mvb) if kv_t else _dot(p, mvb))
            o_ref[b, :, sl] = jnp.where(low, res[0], res[1]).astype(o_ref.dtype)


def _mem_attend(q, mk, mv, bb, tt, kv_t, layer=None):
    bm, t, wd = q.shape
    if layer is None:
        kv = pl.BlockSpec((bb,) + mk.shape[1:], lambda b, i: (b, 0, 0))
    else:
        kv = pl.BlockSpec((None, bb) + mk.shape[2:], lambda b, i: (layer, b, 0, 0))
    return pl.pallas_call(
        functools.partial(_mem_attend_kernel, bb=bb, kv_t=kv_t),
        grid=(bm // bb, t // tt),
        in_specs=[pl.BlockSpec((bb, tt, wd), lambda b, i: (b, i, 0)), kv, kv],
        out_specs=pl.BlockSpec((bb, tt, wd), lambda b, i: (b, i, 0)),
        out_shape=jax.ShapeDtypeStruct((bm, t, wd), BF16),
        compiler_params=_cp(("parallel", "parallel")),
        name="mem_attend",
    )(q, mk, mv)


def _split_bf16(x):
    hi = x.astype(BF16)
    return hi, (x - hi.astype(F32)).astype(BF16)


def _outproj_router_kernel(x_ref, mix_ref, mem_ref, wo1_ref, wo2_ref, g_ref, b_ref, wrh_ref, wrl_ref, rb_ref,
                           x1_ref, gate_ref):
    y = ALPHA * x_ref[...] + _dot(mix_ref[...].astype(BF16), wo1_ref[...]) + _dot(mem_ref[...], wo2_ref[...])
    x1 = _ln(y, g_ref[...], b_ref[...])
    x1_ref[...] = x1
    xh, xl = _split_bf16(x1)
    logit = _dot_nt(wrh_ref[...], xh) + _dot_nt(wrh_ref[...], xl) + _dot_nt(wrl_ref[...], xh)
    s = 1.0 / (1.0 + jnp.exp(-logit))
    sb = s + rb_ref[...]
    epg = N_EXPERTS // N_GROUPS
    rows = [sb[e:e + 1] for e in range(N_EXPERTS)]
    gscore = []
    for g in range(N_GROUPS):
        v = rows[g * epg:(g + 1) * epg]
        best = None
        for a in range(epg):
            for b in range(a + 1, epg):
                pr = v[a] + v[b]
                best = pr if best is None else jnp.maximum(best, pr)
        gscore.append(best)
    gmax = functools.reduce(jnp.maximum, gscore)
    taken = jnp.zeros_like(gmax)
    gsel = []
    for g in range(N_GROUPS):
        pick = jnp.where(gscore[g] == gmax, 1.0, 0.0) * (1.0 - taken)
        taken = taken + pick
        gsel.append(pick)
    val = [jnp.where(gsel[e // epg] > 0.0, rows[e], NEG) for e in range(N_EXPERTS)]
    chosen = [jnp.zeros_like(gmax) for _ in range(N_EXPERTS)]
    for _ in range(2):
        vmax = functools.reduce(jnp.maximum, val)
        taken = jnp.zeros_like(gmax)
        for e in range(N_EXPERTS):
            pick = jnp.where(val[e] == vmax, 1.0, 0.0) * (1.0 - taken)
            taken = taken + pick
            chosen[e] = chosen[e] + pick
            val[e] = jnp.where(pick > 0.0, 2.0 * NEG, val[e])
    wsel = [jnp.where(chosen[e] > 0.0, s[e:e + 1], 0.0) for e in range(N_EXPERTS)]
    wsum = functools.reduce(lambda a, b: a + b, wsel)
    gate_ref[...] = jnp.concatenate([w / wsum for w in wsel], axis=0)


def _outproj_router(x, mix, mem, wo1, wo2, g, b, wrh, wrl, rb, tm):
    n = x.shape[0]
    full = lambda a: pl.BlockSpec(a.shape, lambda i: (0,) * a.ndim)
    rows = lambda w: pl.BlockSpec((tm, w), lambda i: (i, 0))
    return pl.pallas_call(
        _outproj_router_kernel,
        grid=(n // tm,),
        in_specs=[rows(D_MODEL), rows(mix.shape[1]), rows(mem.shape[1]), full(wo1), full(wo2), full(g), full(b),
                  full(wrh), full(wrl), full(rb)],
        out_specs=[rows(D_MODEL), pl.BlockSpec((N_EXPERTS, tm), lambda i: (0, i))],
        out_shape=[jax.ShapeDtypeStruct((n, D_MODEL), F32), jax.ShapeDtypeStruct((N_EXPERTS, n), F32)],
        compiler_params=_cp(("parallel",)),
        name="outproj_router",
    )(x, mix, mem, wo1, wo2, g, b, wrh, wrl, rb)


def _moe_kernel(x_ref, gate_ref, wgu_ref, wd_ref, g_ref, b_ref, o_ref, acc_ref, xb_ref):
    e = pl.program_id(1)

    @pl.when(e == 0)
    def _():
        acc_ref[...] = jnp.zeros(acc_ref.shape, F32)
        xb_ref[...] = x_ref[...].astype(BF16)

    gate = gate_ref[...]
    lane = lax.broadcasted_iota(I32, gate.shape, 1)
    gcol = jnp.sum(jnp.where(lane == e, gate, 0.0), axis=1, keepdims=True)
    hgu = _dot(xb_ref[...], wgu_ref[...])
    hg, hu = hgu[:, :D_EXPERT], hgu[:, D_EXPERT:]
    h = hg * (1.0 / (1.0 + jnp.exp(-hg))) * hu * gcol
    acc_ref[...] += _dot(h.astype(BF16), wd_ref[...])

    @pl.when(e == pl.num_programs(1) - 1)
    def _():
        o_ref[...] = _ln(ALPHA * x_ref[...] + acc_ref[...], g_ref[...], b_ref[...])


def _moe(x, gate, wgu, wd, g, b, tm):
    n = x.shape[0]
    full = lambda a: pl.BlockSpec(a.shape, lambda i, e: (0,) * a.ndim)
    return pl.pallas_call(
        _moe_kernel,
        grid=(n // tm, N_EXPERTS),
        in_specs=[pl.BlockSpec((tm, D_MODEL), lambda i, e: (i, 0)), pl.BlockSpec((tm, N_EXPERTS), lambda i, e: (i, 0)),
                  pl.BlockSpec((None, D_MODEL, 2 * D_EXPERT), lambda i, e: (e, 0, 0)),
                  pl.BlockSpec((None, D_EXPERT, D_MODEL), lambda i, e: (e, 0, 0)), full(g), full(b)],
        out_specs=pl.BlockSpec((tm, D_MODEL), lambda i, e: (i, 0)),
        out_shape=jax.ShapeDtypeStruct((n, D_MODEL), F32),
        scratch_shapes=[pltpu.VMEM((tm, D_MODEL), F32), pltpu.VMEM((tm, D_MODEL), BF16)],
        compiler_params=_cp(("parallel", "arbitrary")),
        name="moe",
    )(x, gate, wgu, wd, g, b)


def _tile(n, pref):
    t = min(n, pref)
    while n % t:
        t //= 2
    return t


def kernel(x_prompt, x_sample, cache_mla_ckv, cache_mla_krope, cache_dsa_k, cache_dsa_v, cache_dsa_kidx, cache_mem_k, cache_mem_v, page_table, mem_prompt, w_in_mla, mla_q_norm, mla_kv_norm, w_uq, w_uk, w_uv, w_in_dsa, idx_k_norm_g, idx_k_norm_b, w_mem_kv, w_out, ln1_g, ln1_b, ln2_g, ln2_b, w_router, router_bias, w_gate, w_up, w_down):
    B, S, _ = x_prompt.shape
    DB, T, _ = x_sample.shape
    n_mem = mem_prompt.shape[1]
    npg = page_table.shape[1]
    past = npg * PAGE
    n_pool = cache_mla_ckv.shape[1]
    NP, NS = B * S, DB * T
    TPAD = 16
    pp_dsa = pp_idx = 32 if npg % 32 == 0 else npg
    pp_mla = 16 if npg % 16 == 0 else npg
    ns_dsa = 4 if pp_dsa % 4 == 0 else 1
    ns_mla = 2 if pp_mla % 2 == 0 else 1
    tm_p, tm_s = _tile(NP, 512), _tile(NS, 512)
    t_att = _tile(S, 512)
    KVW = KV_HEADS * HEAD_DIM

    pos_p = jnp.arange(S, dtype=jnp.int32)
    pos_s = jnp.tile(past + jnp.arange(T, dtype=jnp.int32), DB)
    cos_mp, sin_mp = _rope_tables(pos_p, QK_ROPE, LANE, QK_NOPE, LANE)
    cos_ms, sin_ms = _rope_tables(pos_s, QK_ROPE, LANE, QK_NOPE, LANE)
    cos_mr, sin_mr = _rope_tables(pos_s, QK_ROPE, QK_ROPE, 0, N_HEADS * QK_ROPE)
    cos_dp, sin_dp = _rope_tables(pos_p, ROT_DIM, HEAD_DIM, 0, LANE)
    cos_ds, sin_ds = _rope_tables(pos_s, ROT_DIM, HEAD_DIM, 0, LANE)

    wrt = w_router.T
    wrh = wrt.astype(BF16)
    wrl = (wrt - wrh.astype(F32)).astype(BF16)
    rb = router_bias.reshape(N_EXPERTS, 1).astype(F32)
    perm = jnp.asarray(DSA_PERM)
    pool_kt = jnp.transpose(cache_dsa_k, (0, 1, 3, 4, 2)).reshape(-1, n_pool, KVW, PAGE)
    pool_vt = jnp.transpose(cache_dsa_v, (0, 1, 3, 4, 2)).reshape(-1, n_pool, KVW, PAGE)
    pool_kidxt = jnp.transpose(cache_dsa_kidx, (0, 1, 3, 2))
    pool_krt = jnp.transpose(cache_mla_krope, (0, 1, 3, 2))
    mem_kt = jnp.transpose(cache_mem_k, (0, 1, 3, 4, 2)).reshape(DEPTH, DB, MEM_HEADS * HEAD_DIM, n_mem)
    mem_vt = jnp.transpose(cache_mem_v, (0, 1, 3, 4, 2)).reshape(DEPTH, DB, MEM_HEADS * HEAD_DIM, n_mem)
    mem2d = mem_prompt.reshape(B * n_mem, D_MODEL)

    def pad_new(a):
        a = a.reshape(DB, T, a.shape[-1])
        return jnp.concatenate([a, jnp.zeros((DB, TPAD - T, a.shape[-1]), a.dtype)], axis=1)

    def post(x, mix, mem, i, wo_rows, tm):
        wo = w_out[i]
        wo1 = wo[:N_HEADS * HEAD_DIM]
        if wo_rows is not None:
            wo1 = wo1.reshape(N_HEADS, HEAD_DIM, D_MODEL)[wo_rows].reshape(N_HEADS * HEAD_DIM, D_MODEL)
        x1, gate_t = _outproj_router(x, mix, mem, wo1.astype(BF16), wo[N_HEADS * HEAD_DIM:].astype(BF16),
                                     ln1_g[i].reshape(1, -1), ln1_b[i].reshape(1, -1), wrh, wrl, rb, tm)
        wgu = jnp.concatenate([w_gate[i], w_up[i]], axis=-1).astype(BF16)
        return _moe(x1, gate_t.T, wgu, w_down[i].astype(BF16), ln2_g[i].reshape(1, -1), ln2_b[i].reshape(1, -1),
                    _tile(x.shape[0], 1024))

    xp = x_prompt.reshape(NP, D_MODEL)
    xs = x_sample.reshape(NS, D_MODEL)
    ckv_p, kr_p, ckv_s, kr_s = [], [], [], []
    k_p, v_p, ki_p, k_s, v_s, ki_s = [], [], [], [], [], []
    mk_list, mv_list = [], []
    for i in range(DEPTH):
        j = i // 2
        mkf, mvf, mkb, mvb = _mem_kv(mem2d, w_mem_kv[i])
        mk_list.append(mkf.reshape(B, n_mem, MEM_HEADS, HEAD_DIM))
        mv_list.append(mvf.reshape(B, n_mem, MEM_HEADS, HEAD_DIM))
        if i % 2 == 0:
            q, k, vt, ckv, krpad, qm_p = _mla_proj_prompt(xp, w_in_mla[j], mla_q_norm[j], mla_kv_norm[j], w_uq[j],
                                                          w_uk[j], w_uv[j], cos_mp, sin_mp, tm_p)
            mix_p = _mla_flash(q, k, vt, B, t_att, _tile(S, 1024))
            ckv_p.append(ckv.reshape(B, S, KV_LORA))
            kr_p.append(krpad[:, QK_NOPE:QK_NOPE + QK_ROPE].reshape(B, S, QK_ROPE))
            qlat, qr, ckv, krpad, qm_s = _mla_proj_sample(xs, w_in_mla[j], mla_q_norm[j], mla_kv_norm[j], w_uq[j],
                                                          w_uk[j], cos_ms, sin_ms, cos_mr, sin_mr, tm_s)
            kr = krpad[:, QK_NOPE:QK_NOPE + QK_ROPE]
            ckv_s.append(ckv.reshape(DB, T, KV_LORA))
            kr_s.append(kr.reshape(DB, T, QK_ROPE))
            qlat = qlat.reshape(DB, T, N_HEADS, KV_LORA).transpose(0, 2, 1, 3).reshape(DB, N_HEADS * T, KV_LORA)
            qr = qr.reshape(DB, T, N_HEADS, QK_ROPE).transpose(0, 2, 1, 3).reshape(DB, N_HEADS * T, QK_ROPE)
            o_lat = _mla_decode(page_table, qlat, qr, pad_new(ckv), pad_new(kr), cache_mla_ckv, pool_krt,
                                j, pp_mla, ns_mla, T)
            o_lat = o_lat.reshape(DB, N_HEADS, T, KV_LORA).transpose(1, 0, 2, 3).reshape(N_HEADS, NS, KV_LORA)
            o = _head_matmul(o_lat, jnp.transpose(w_uv[j], (1, 0, 2)).astype(BF16))
            mix_s = o.transpose(1, 0, 2).reshape(NS, N_HEADS * HEAD_DIM)
            wo_rows = None
        else:
            q, kb, kf, vf, vt, wt, qidx, kw, kib, qm_p = _dsa_proj(xp, w_in_dsa[j], idx_k_norm_g[j],
                                                                   idx_k_norm_b[j], cos_dp, sin_dp, tm_p)
            k_p.append(kf.reshape(B, S, KV_HEADS, HEAD_DIM))
            v_p.append(vf.reshape(B, S, KV_HEADS, HEAD_DIM))
            ki_p.append(kw[:, :IDX_DIM].reshape(B, S, IDX_DIM))
            sct = _indexer_prompt(qidx, kib, wt, B, t_att, t_att)
            thr, cut = _select_cols(sct, B, min(TOPK_MAX, S // 4), _tile(S, 512))
            mix_p = _dsa_attn(q, kb, vt, sct, thr, cut, B, t_att, t_att)

            q, kb, kf, vf, vt, wt, qidx, kw, kib, qm_s = _dsa_proj(xs, w_in_dsa[j], idx_k_norm_g[j],
                                                                   idx_k_norm_b[j], cos_ds, sin_ds, tm_s)
            k_s.append(kf.reshape(DB, T, KV_HEADS, HEAD_DIM))
            v_s.append(vf.reshape(DB, T, KV_HEADS, HEAD_DIM))
            kidx_new = kw[:, :IDX_DIM]
            ki_s.append(kidx_new.reshape(DB, T, IDX_DIM))
            qc = jnp.stack([qidx[h, :, (h % 2) * HEAD_DIM:(h % 2 + 1) * HEAD_DIM] for h in range(IDX_HEADS)])
            qc = qc.reshape(IDX_HEADS, DB, T, IDX_DIM).transpose(1, 0, 2, 3).reshape(DB, IDX_HEADS * T, IDX_DIM)
            w_s = kw[:, HEAD_DIM:HEAD_DIM + IDX_HEADS].reshape(DB, T, IDX_HEADS)
            sc_past, sc_new = _indexer_sample(page_table, qc, w_s, pad_new(kidx_new), pool_kidxt, j, pp_idx, T)
            topk_s = min(TOPK_MAX, (past + T) // 4)
            cw_s = 5 * LANE if (past + LANE) % (5 * LANE) == 0 else LANE
            sc_all = jnp.concatenate([sc_past, sc_new], axis=-1).reshape(NS, past + LANE)
            thr, cut = _select_rows(sc_all, topk_s, cw_s, _tile(NS, 128))
            qh = jnp.stack([q[h, :, ((h // 3) % 2) * HEAD_DIM:((h // 3) % 2 + 1) * HEAD_DIM] for h in range(N_HEADS)])
            qh = qh.reshape(KV_HEADS, 3, DB, T, HEAD_DIM).transpose(2, 0, 1, 3, 4)
            eye = jnp.eye(KV_HEADS, dtype=BF16)
            qbd = (qh[:, :, :, :, None, :] * eye[None, :, None, None, :, None]).reshape(DB, N_HEADS * T, KVW)
            o = _dsa_decode(page_table, qbd, sc_past, sc_new, thr.reshape(DB, T, LANE), cut.reshape(DB, T, LANE),
                            pad_new(kf).transpose(0, 2, 1), pad_new(vf), pool_kt, pool_vt, j, pp_dsa, ns_dsa, T)
            o = o.reshape(DB, KV_HEADS, 3, T, KV_HEADS, HEAD_DIM)
            o = jnp.stack([o[:, g, :, :, g, :] for g in range(KV_HEADS)], axis=1)
            o = o.reshape(DB, N_HEADS, T, HEAD_DIM)[:, perm]
            mix_s = o.transpose(0, 2, 1, 3).reshape(NS, N_HEADS * HEAD_DIM)
            wo_rows = perm
        memo_p = _mem_attend(qm_p.reshape(B, S, -1), mkb.reshape(B, n_mem, -1), mvb.reshape(B, n_mem, -1), 1,
                             _tile(S, 512), False)
        qm_s3 = qm_s.reshape(DB, T, -1)
        qm_s3 = jnp.concatenate([qm_s3, jnp.zeros((DB, TPAD - T, qm_s3.shape[-1]), BF16)], axis=1)
        memo_s = _mem_attend(qm_s3, mem_kt, mem_vt, _tile(DB, 8), TPAD, True, layer=i)[:, :T]
        xp = post(xp, mix_p, memo_p.reshape(NP, -1), i, wo_rows, tm_p)
        xs = post(xs, mix_s, memo_s.reshape(NS, -1), i, wo_rows, tm_s)
    return (xp.reshape(B, S, D_MODEL), xs.reshape(DB, T, D_MODEL),
            jnp.stack(ckv_p), jnp.stack(kr_p), jnp.stack(k_p), jnp.stack(v_p), jnp.stack(ki_p),
            jnp.stack(mk_list), jnp.stack(mv_list),
            jnp.stack(ckv_s), jnp.stack(kr_s), jnp.stack(k_s), jnp.stack(v_s), jnp.stack(ki_s))
```

```python
import functools

import jax
import jax.numpy as jnp
from jax import lax
from jax.experimental import pallas as pl
from jax.experimental.pallas import tpu as pltpu

F32 = jnp.float32
BF16 = jnp.bfloat16
I32 = jnp.int32
I16 = jnp.int16

D_MODEL = 1024
DEPTH = 4
PAGE = 128
HEAD_DIM = 64
N_HEADS = 12
MEM_HEADS = 4
Q_LORA = 384
KV_LORA = 256
QK_NOPE = 64
QK_ROPE = 32
MLA_SCALE = (QK_NOPE + QK_ROPE) ** -0.5
HEAD_SCALE = HEAD_DIM ** -0.5
LOG2E = 1.4426950408889634
MLA_QSCALE = MLA_SCALE * LOG2E
DSA_QSCALE = HEAD_SCALE * LOG2E
KV_HEADS = 4
IDX_HEADS = 8
IDX_DIM = 64
TOPK_MAX = 256
ROT_DIM = HEAD_DIM // 4
ROPE_THETA = 500000.0
N_EXPERTS = 16
N_GROUPS = 4
D_EXPERT = 256
ALPHA = (2 * DEPTH) ** 0.25
LN_EPS = 1e-5
RMS_EPS = 1e-6
NEG = -1e30
LANE = 128
SUBLANE = 8
MIN_I32 = -2 ** 31
N_ACC = 8
DSA_PERM = (0, 3, 1, 4, 2, 5, 6, 9, 7, 10, 8, 11)

NT = (((1,), (1,)), ((), ()))


def _cp(sem, vmem_mb=48):
    return pltpu.CompilerParams(dimension_semantics=sem, vmem_limit_bytes=vmem_mb * 1024 * 1024)


def _dot(a, b):
    return jnp.dot(a, b, preferred_element_type=F32)


def _dot_nt(a, b):
    return lax.dot_general(a, b, NT, preferred_element_type=F32)


def _rms(x, g):
    return x * lax.rsqrt(jnp.mean(x * x, axis=-1, keepdims=True) + RMS_EPS) * g


def _ln(x, g, b):
    mu = jnp.mean(x, axis=-1, keepdims=True)
    xc = x - mu
    var = jnp.mean(xc * xc, axis=-1, keepdims=True)
    return xc * lax.rsqrt(var + LN_EPS) * g + b


def _softmax_step(s, m_prev, l_prev, axis=1):
    m_new = jnp.maximum(m_prev, jnp.max(s, axis=axis, keepdims=True))
    a = jnp.exp2(m_prev - m_new)
    p = jnp.exp2(s - m_new)
    return m_new, a, p, a * l_prev + jnp.sum(p, axis=axis, keepdims=True)


def _merge_streams(ms, ls, accs):
    m = functools.reduce(jnp.maximum, ms)
    ws = [jnp.exp2(mi - m) for mi in ms]
    l = functools.reduce(lambda a, b: a + b, [w * li for w, li in zip(ws, ls)])
    acc = functools.reduce(lambda a, b: a + b, [w * ai for w, ai in zip(ws, accs)])
    return m, l, acc


def _score_keys(s):
    b = lax.bitcast_convert_type(s, I32)
    return jnp.where(s == 0.0, 0, b ^ ((b >> 31) & 0x7FFFFFFF))


def _key_to_score(k, shape):
    return lax.bitcast_convert_type(jnp.broadcast_to(k ^ ((k >> 31) & 0x7FFFFFFF), shape), F32)


def _rope_tables(pos, rot_dim, period, offset, width):
    half = rot_dim // 2
    inv = ROPE_THETA ** (-jnp.arange(half, dtype=jnp.float32) / half)
    ang = pos.astype(jnp.float32)[:, None] * inv
    c, s = jnp.cos(ang), jnp.sin(ang)
    t = pos.shape[0]
    cg = jnp.ones((t, period), F32).at[:, offset:offset + rot_dim].set(jnp.concatenate([c, c], 1))
    sg = jnp.zeros((t, period), F32).at[:, offset:offset + rot_dim].set(jnp.concatenate([s, s], 1))
    return jnp.tile(cg, (1, width // period)), jnp.tile(sg, (1, width // period))


def _rot_cols(w, half):
    return jnp.concatenate([-w[..., half:2 * half], w[..., :half]], axis=-1)


def _rot_heads(w, heads, dim, rot):
    w3 = w.reshape(w.shape[0], heads, dim)
    r = jnp.concatenate([_rot_cols(w3[..., :rot], rot // 2), jnp.zeros_like(w3[..., rot:])], axis=-1)
    return r.reshape(w.shape[0], heads * dim)


def _mla_front(x_ref, w1_ref, w2_ref, qn_ref, kvn_ref, cos_ref, sin_ref):
    xb = x_ref[...].astype(BF16)
    y = _dot(xb, w1_ref[...])
    cqn = _rms(y[:, :Q_LORA], qn_ref[...]).astype(BF16)
    ckv = _rms(y[:, Q_LORA:Q_LORA + KV_LORA], kvn_ref[...])
    qmem = y[:, 640:896]
    cos = cos_ref[...]
    sin = sin_ref[...]
    krp = y[:, 896:1024] * cos + y[:, 1024:1152] * sin
    q2 = _dot(cqn, w2_ref[...])
    return q2, ckv, qmem, krp, cos, sin


def _mla_proj_prompt_kernel(x_ref, w1_ref, w2_ref, wk_ref, wvt_ref, qn_ref, kvn_ref, cos_ref, sin_ref,
                            q_ref, k_ref, vt_ref, ckv_ref, kr_ref, qm_ref):
    q2, ckv, qmem, krp, cos, sin = _mla_front(x_ref, w1_ref, w2_ref, qn_ref, kvn_ref, cos_ref, sin_ref)
    hw = N_HEADS * LANE
    for h in range(N_HEADS):
        qh = q2[:, h * LANE:(h + 1) * LANE] * cos + q2[:, hw + h * LANE:hw + (h + 1) * LANE] * sin
        q_ref[h] = (qh * MLA_QSCALE).astype(BF16)
    ckv_ref[...] = ckv
    kr_ref[...] = krp
    qm_ref[...] = (qmem * HEAD_SCALE).astype(BF16)
    cb = ckv.astype(BF16)
    kn = _dot(cb, wk_ref[...])
    for h in range(N_HEADS):
        k_ref[h] = (kn[:, h * LANE:(h + 1) * LANE] + krp).astype(BF16)
    vt_ref[...] = _dot_nt(wvt_ref[...], cb).astype(BF16)


def _mla_proj_sample_kernel(x_ref, w1_ref, w2_ref, wukt_ref, qn_ref, kvn_ref, cos_ref, sin_ref,
                            cosr_ref, sinr_ref, qlat_ref, qr_ref, ckv_ref, kr_ref, qm_ref):
    q2, ckv, qmem, krp, _, _ = _mla_front(x_ref, w1_ref, w2_ref, qn_ref, kvn_ref, cos_ref, sin_ref)
    hw = N_HEADS * LANE
    rw = N_HEADS * QK_ROPE
    for h in range(N_HEADS):
        qh = (q2[:, h * LANE:(h + 1) * LANE] * MLA_QSCALE).astype(BF16)
        qlat_ref[:, h * KV_LORA:(h + 1) * KV_LORA] = _dot(qh, wukt_ref[h]).astype(BF16)
    qr = q2[:, hw:hw + rw] * cosr_ref[...] + q2[:, hw + rw:hw + 2 * rw] * sinr_ref[...]
    qr_ref[...] = (qr * MLA_QSCALE).astype(BF16)
    ckv_ref[...] = ckv
    kr_ref[...] = krp
    qm_ref[...] = (qmem * HEAD_SCALE).astype(BF16)


def _mla_w1(w_in):
    w_cq, w_ckv = w_in[:, :Q_LORA], w_in[:, Q_LORA:Q_LORA + KV_LORA]
    w_kr = w_in[:, Q_LORA + KV_LORA:Q_LORA + KV_LORA + QK_ROPE]
    w_qm = w_in[:, Q_LORA + KV_LORA + QK_ROPE:]
    z = lambda n: jnp.zeros((D_MODEL, n), F32)
    kr_pad = jnp.concatenate([z(QK_NOPE), w_kr, z(LANE - QK_NOPE - QK_ROPE)], 1)
    kr_rot = jnp.concatenate([z(QK_NOPE), _rot_cols(w_kr, QK_ROPE // 2), z(LANE - QK_NOPE - QK_ROPE)], 1)
    return jnp.concatenate([w_cq, w_ckv, w_qm, kr_pad, kr_rot], 1).astype(BF16)


def _mla_proj_prompt(x, w_in, q_norm, kv_norm, w_uq, w_uk, w_uv, cos, sin, tm):
    n = x.shape[0]
    w1 = _mla_w1(w_in)
    zq = jnp.zeros((Q_LORA, N_HEADS, LANE - QK_NOPE - QK_ROPE), F32)
    uq_pad = jnp.concatenate([w_uq, zq], -1).reshape(Q_LORA, N_HEADS * LANE)
    uq_rot = jnp.concatenate([jnp.zeros((Q_LORA, N_HEADS, QK_NOPE), F32),
                              _rot_cols(w_uq[..., QK_NOPE:], QK_ROPE // 2), zq], -1).reshape(Q_LORA, N_HEADS * LANE)
    w2 = jnp.concatenate([uq_pad, uq_rot], 1).astype(BF16)
    wk = jnp.concatenate([w_uk, jnp.zeros((KV_LORA, N_HEADS, LANE - QK_NOPE), F32)], -1)
    wk = wk.reshape(KV_LORA, N_HEADS * LANE).astype(BF16)
    wvt = w_uv.reshape(KV_LORA, N_HEADS * HEAD_DIM).T.astype(BF16)
    nt = cos.shape[0] // tm
    full = lambda a: pl.BlockSpec(a.shape, lambda i: (0,) * a.ndim)
    rows = lambda w: pl.BlockSpec((tm, w), lambda i: (i, 0))
    heads = pl.BlockSpec((N_HEADS, tm, LANE), lambda i: (0, i, 0))
    tab = pl.BlockSpec((tm, LANE), lambda i: (i % nt, 0))
    qn, kvn = q_norm.reshape(1, -1), kv_norm.reshape(1, -1)
    return pl.pallas_call(
        _mla_proj_prompt_kernel,
        grid=(n // tm,),
        in_specs=[rows(D_MODEL), full(w1), full(w2), full(wk), full(wvt), full(qn), full(kvn), tab, tab],
        out_specs=[heads, heads, pl.BlockSpec((None, N_HEADS * HEAD_DIM, tm), lambda i: (i, 0, 0)), rows(KV_LORA),
                   rows(LANE), rows(MEM_HEADS * HEAD_DIM)],
        out_shape=[jax.ShapeDtypeStruct((N_HEADS, n, LANE), BF16), jax.ShapeDtypeStruct((N_HEADS, n, LANE), BF16),
                   jax.ShapeDtypeStruct((n // tm, N_HEADS * HEAD_DIM, tm), BF16), jax.ShapeDtypeStruct((n, KV_LORA), F32),
                   jax.ShapeDtypeStruct((n, LANE), F32), jax.ShapeDtypeStruct((n, MEM_HEADS * HEAD_DIM), BF16)],
        compiler_params=_cp(("parallel",)),
        name="mla_proj_prompt",
    )(x, w1, w2, wk, wvt, qn, kvn, cos, sin)


def _mla_proj_sample(x, w_in, q_norm, kv_norm, w_uq, w_uk, cos, sin, cosr, sinr, tm):
    n = x.shape[0]
    w1 = _mla_w1(w_in)
    zq = jnp.zeros((Q_LORA, N_HEADS, LANE - QK_NOPE - QK_ROPE), F32)
    uq_pad = jnp.concatenate([w_uq, zq], -1).reshape(Q_LORA, N_HEADS * LANE)
    w_qr = w_uq[..., QK_NOPE:]
    w2 = jnp.concatenate([uq_pad, w_qr.reshape(Q_LORA, -1),
                          _rot_cols(w_qr, QK_ROPE // 2).reshape(Q_LORA, -1)], 1).astype(BF16)
    wukt = jnp.transpose(w_uk, (1, 2, 0))
    wukt = jnp.concatenate([wukt, jnp.zeros((N_HEADS, LANE - QK_NOPE, KV_LORA), F32)], 1).astype(BF16)
    full = lambda a: pl.BlockSpec(a.shape, lambda i: (0,) * a.ndim)
    rows = lambda w: pl.BlockSpec((tm, w), lambda i: (i, 0))
    qn, kvn = q_norm.reshape(1, -1), kv_norm.reshape(1, -1)
    return pl.pallas_call(
        _mla_proj_sample_kernel,
        grid=(n // tm,),
        in_specs=[rows(D_MODEL), full(w1), full(w2), full(wukt), full(qn), full(kvn), rows(LANE), rows(LANE),
                  rows(N_HEADS * QK_ROPE), rows(N_HEADS * QK_ROPE)],
        out_specs=[rows(N_HEADS * KV_LORA), rows(N_HEADS * QK_ROPE), rows(KV_LORA), rows(LANE),
                   rows(MEM_HEADS * HEAD_DIM)],
        out_shape=[jax.ShapeDtypeStruct((n, N_HEADS * KV_LORA), BF16),
                   jax.ShapeDtypeStruct((n, N_HEADS * QK_ROPE), BF16), jax.ShapeDtypeStruct((n, KV_LORA), F32),
                   jax.ShapeDtypeStruct((n, LANE), F32), jax.ShapeDtypeStruct((n, MEM_HEADS * HEAD_DIM), BF16)],
        compiler_params=_cp(("parallel",)),
        name="mla_proj_sample",
    )(x, w1, w2, wukt, qn, kvn, cos, sin, cosr, sinr)


def _causal_steps(nq, tq, tk):
    qs, ks = [], []
    for qi in range(nq):
        for ki in range(((qi + 1) * tq - 1) // tk + 1):
            qs.append(qi)
            ks.append(ki)
    return jnp.asarray(qs, I32), jnp.asarray(ks, I32)


def _mla_flash_kernel(qt_ref, kt_ref, q_ref, k_ref, vt_ref, o_ref, m_ref, l_ref, acc_ref, *, tq, tk):
    qi = qt_ref[pl.program_id(2)]
    ki = kt_ref[pl.program_id(2)]
    last = ((qi + 1) * tq - 1) // tk

    @pl.when(ki == 0)
    def _():
        m_ref[...] = jnp.full(m_ref.shape, NEG, F32)
        l_ref[...] = jnp.zeros(l_ref.shape, F32)
        acc_ref[...] = jnp.zeros(acc_ref.shape, F32)

    def step(masked):
        ss = [_dot_nt(k_ref[hh], q_ref[hh]) for hh in range(2)]
        if masked:
            key = ki * tk + lax.broadcasted_iota(I32, (tk, tq), 0)
            qry = qi * tq + lax.broadcasted_iota(I32, (tk, tq), 1)
            causal = key <= qry
        for hh in range(2):
            s = jnp.where(causal, ss[hh], NEG) if masked else ss[hh]
            m_new, a, p, l_new = _softmax_step(s, m_ref[hh], l_ref[hh], axis=0)
            vt = jnp.concatenate([vt_ref[c, hh * HEAD_DIM:(hh + 1) * HEAD_DIM, :] for c in range(vt_ref.shape[0])],
                                 axis=1)
            acc_ref[hh] = a * acc_ref[hh] + _dot(vt, p.astype(BF16))
            m_ref[hh] = m_new
            l_ref[hh] = l_new

    first_masked = (qi * tq + 1) // tk

    @pl.when(ki < first_masked)
    def _():
        step(False)

    @pl.when(ki >= first_masked)
    def _():
        step(True)

    @pl.when(ki == last)
    def _():
        ot = jnp.concatenate([acc_ref[0] / l_ref[0], acc_ref[1] / l_ref[1]], axis=0)
        o_ref[...] = ot.T.astype(o_ref.dtype)


def _mla_flash(q, k, vt, batch, tq, tk):
    n = q.shape[1]
    s = n // batch
    nq, nk = s // tq, s // tk
    tv = vt.shape[2]
    qtab, ktab = _causal_steps(nq, tq, tk)
    grid_spec = pltpu.PrefetchScalarGridSpec(
        num_scalar_prefetch=2,
        grid=(batch, N_HEADS // 2, qtab.shape[0]),
        in_specs=[pl.BlockSpec((2, tq, LANE), lambda b, hp, t, qt, kt: (hp, b * nq + qt[t], 0)),
                  pl.BlockSpec((2, tk, LANE), lambda b, hp, t, qt, kt: (hp, b * nk + kt[t], 0)),
                  pl.BlockSpec((tk // tv, 2 * HEAD_DIM, tv), lambda b, hp, t, qt, kt: (b * nk + kt[t], hp, 0))],
        out_specs=pl.BlockSpec((tq, LANE), lambda b, hp, t, qt, kt: (b * nq + qt[t], hp)),
        scratch_shapes=[pltpu.VMEM((2, 1, tq), F32), pltpu.VMEM((2, 1, tq), F32), pltpu.VMEM((2, HEAD_DIM, tq), F32)],
    )
    return pl.pallas_call(
        functools.partial(_mla_flash_kernel, tq=tq, tk=tk),
        grid_spec=grid_spec,
        out_shape=jax.ShapeDtypeStruct((n, N_HEADS * HEAD_DIM), BF16),
        compiler_params=_cp(("parallel", "parallel", "arbitrary")),
        name="mla_flash",
    )(qtab, ktab, q, k, vt)


def _mla_decode_kernel(pt_ref, qlat_ref, qr_ref, cn_ref, kn_ref, *rest, pp, ns, t_new):
    ck_refs, krt_refs = rest[:pp], rest[pp:2 * pp]
    o_ref, m_ref, l_ref, acc_ref = rest[2 * pp:]
    p_id = pl.program_id(1)
    rows = qlat_ref.shape[1]
    per = pp // ns

    @pl.when(p_id == 0)
    def _():
        m_ref[...] = jnp.full(m_ref.shape, NEG, F32)
        l_ref[...] = jnp.zeros(l_ref.shape, F32)
        acc_ref[...] = jnp.zeros(acc_ref.shape, F32)

    qlat = qlat_ref[0]
    qr = qr_ref[0]
    for st in range(ns):
        pages = range(st * per, (st + 1) * per)
        cks = [ck_refs[i][...].astype(BF16) for i in pages]
        s = jnp.concatenate([_dot_nt(qlat, ck) + _dot(qr, krt_refs[i][...].astype(BF16))
                             for ck, i in zip(cks, pages)], axis=1)
        m_new, a, p, l_new = _softmax_step(s, m_ref[st], l_ref[st])
        pb = p.astype(BF16)
        pv = _dot(pb[:, :PAGE], cks[0])
        for i in range(1, per):
            pv = pv + _dot(pb[:, i * PAGE:(i + 1) * PAGE], cks[i])
        acc_ref[st] = a * acc_ref[st] + pv
        m_ref[st] = m_new
        l_ref[st] = l_new

    @pl.when(p_id == pl.num_programs(1) - 1)
    def _():
        m, l, acc = _merge_streams([m_ref[st] for st in range(ns)], [l_ref[st] for st in range(ns)],
                                   [acc_ref[st] for st in range(ns)])
        cn = cn_ref[0].astype(BF16)
        kn = kn_ref[0].astype(BF16)
        npad = cn.shape[0]
        t = lax.broadcasted_iota(I32, (rows, npad), 0) % t_new
        u = lax.broadcasted_iota(I32, (rows, npad), 1)
        s2 = jnp.where(u <= t, _dot_nt(qlat, cn) + _dot_nt(qr, kn), NEG)
        m2, a2, p2, l2 = _softmax_step(s2, m, l)
        o_ref[0] = (a2 * acc + _dot(p2.astype(BF16), cn)) / l2


def _mla_decode(page_table, qlat, qr, ckv_new, kr_new, pool_ckv, pool_krt, j, pp, ns, t_new):
    db, rows, _ = qlat.shape
    npg = page_table.shape[1]
    pt = page_table.reshape(-1)
    npad = ckv_new.shape[1]
    page = lambda i: (lambda b, p, pt_: (j, pt_[b * npg + p * pp + i], 0, 0))
    per_b = lambda r, w: pl.BlockSpec((1, r, w), lambda b, p, pt_: (b, 0, 0))
    grid_spec = pltpu.PrefetchScalarGridSpec(
        num_scalar_prefetch=1,
        grid=(db, npg // pp),
        in_specs=[per_b(rows, KV_LORA), per_b(rows, QK_ROPE), per_b(npad, KV_LORA), per_b(npad, QK_ROPE)]
        + [pl.BlockSpec((None, None, PAGE, KV_LORA), page(i)) for i in range(pp)]
        + [pl.BlockSpec((None, None, QK_ROPE, PAGE), page(i)) for i in range(pp)],
        out_specs=per_b(rows, KV_LORA),
        scratch_shapes=[pltpu.VMEM((ns, rows, 1), F32), pltpu.VMEM((ns, rows, 1), F32),
                        pltpu.VMEM((ns, rows, KV_LORA), F32)],
    )
    return pl.pallas_call(
        functools.partial(_mla_decode_kernel, pp=pp, ns=ns, t_new=t_new),
        grid_spec=grid_spec,
        out_shape=jax.ShapeDtypeStruct((db, rows, KV_LORA), F32),
        compiler_params=_cp(("parallel", "arbitrary")),
        name="mla_decode",
    )(pt, qlat, qr, ckv_new, kr_new, *([pool_ckv] * pp), *([pool_krt] * pp))


def _head_matmul_kernel(x_ref, w_ref, o_ref):
    o_ref[...] = _dot(x_ref[...].astype(BF16), w_ref[...]).astype(o_ref.dtype)


def _head_matmul(x, w):
    h, n, k = x.shape
    m = w.shape[2]
    return pl.pallas_call(
        _head_matmul_kernel,
        grid=(h,),
        in_specs=[pl.BlockSpec((None, n, k), lambda i: (i, 0, 0)), pl.BlockSpec((None, k, m), lambda i: (i, 0, 0))],
        out_specs=pl.BlockSpec((None, n, m), lambda i: (i, 0, 0)),
        out_shape=jax.ShapeDtypeStruct((h, n, m), BF16),
        compiler_params=_cp(("parallel",)),
        name="head_matmul",
    )(x, w)


def _dsa_proj_kernel(x_ref, wq_ref, wk_ref, wv_ref, wvt_ref, wwt_ref, wqi_ref, wm_ref, wki_ref, lng_ref, lnb_ref,
                     sgn_ref, gperm_ref, brot_ref, cos_ref, sin_ref,
                     q_ref, k_ref, kf_ref, vf_ref, vt_ref, wt_ref, qi_ref, kw_ref, kib_ref, qm_ref):
    tm = x_ref.shape[0]
    xb = x_ref[...].astype(BF16)
    cos = cos_ref[...]
    sin = sin_ref[...]
    lane = lax.broadcasted_iota(I32, (tm, LANE), 1)
    low = lane < HEAD_DIM

    def roped(w_ref, nblk):
        y = _dot(xb, w_ref[...])
        return [y[:, b * LANE:(b + 1) * LANE] * cos + y[:, (nblk + b) * LANE:(nblk + b + 1) * LANE] * sin
                for b in range(nblk)]

    qb = roped(wq_ref, N_HEADS // 2)
    for h in range(N_HEADS):
        g, r = h // 3, h % 3
        blk = qb[(g // 2) * 3 + r] * DSA_QSCALE
        q_ref[h] = jnp.where(low if g % 2 == 0 else jnp.logical_not(low), blk, 0.0).astype(BF16)
    kb = roped(wk_ref, KV_HEADS // 2)
    for b in range(KV_HEADS // 2):
        kf_ref[:, b * LANE:(b + 1) * LANE] = kb[b]
        k_ref[b] = kb[b].astype(BF16)
    vf_ref[...] = _dot(xb, wv_ref[...])
    vt_ref[...] = _dot_nt(wvt_ref[...], xb).astype(BF16)
    wt_ref[...] = _dot_nt(wwt_ref[...], xb)
    qib = roped(wqi_ref, IDX_HEADS // 2)
    for h in range(IDX_HEADS):
        qi_ref[h] = jnp.where(low if h % 2 == 0 else jnp.logical_not(low), qib[h // 2], 0.0).astype(BF16)
    qm_ref[...] = (_dot(xb, wm_ref[...]) * HEAD_SCALE).astype(BF16)
    yk = _dot(xb, wki_ref[...])
    c, c_rot, wpad = yk[:, :LANE], yk[:, LANE:2 * LANE], yk[:, 2 * LANE:3 * LANE]
    mu = jnp.sum(jnp.where(low, c, 0.0), axis=1, keepdims=True) * (1.0 / IDX_DIM)
    xc = c - mu
    var = jnp.sum(jnp.where(low, xc * xc, 0.0), axis=1, keepdims=True) * (1.0 / IDX_DIM)
    rstd = lax.rsqrt(var + LN_EPS)
    ki = xc * rstd * lng_ref[...] + lnb_ref[...]
    ki_rot = (c_rot - sgn_ref[...] * mu) * rstd * gperm_ref[...] + brot_ref[...]
    kir = ki * cos + ki_rot * sin
    kib_ref[...] = kir.astype(BF16)
    kw_ref[...] = jnp.where(low, kir, wpad)


def _dsa_proj(x, w_in, ln_g, ln_b, cos, sin, tm):
    n = x.shape[0]
    o = 0
    cols = []
    for wdt in (N_HEADS * HEAD_DIM, KV_HEADS * HEAD_DIM, KV_HEADS * HEAD_DIM, IDX_HEADS * IDX_DIM, IDX_DIM, IDX_HEADS,
                MEM_HEADS * HEAD_DIM):
        cols.append(w_in[:, o:o + wdt])
        o += wdt
    w_q, w_k, w_v, w_qi, w_ki, w_w, w_qm = cols
    perm = jnp.asarray(DSA_PERM)
    w_qp = w_q.reshape(D_MODEL, N_HEADS, HEAD_DIM)[:, perm].reshape(D_MODEL, -1)
    wq = jnp.concatenate([w_qp, _rot_heads(w_qp, N_HEADS, HEAD_DIM, ROT_DIM)], 1).astype(BF16)
    wk = jnp.concatenate([w_k, _rot_heads(w_k, KV_HEADS, HEAD_DIM, ROT_DIM)], 1).astype(BF16)
    wqi = jnp.concatenate([w_qi, _rot_heads(w_qi, IDX_HEADS, IDX_DIM, ROT_DIM)], 1).astype(BF16)
    w_ki_rot = _rot_heads(w_ki, 1, IDX_DIM, ROT_DIM)
    w_wpad = jnp.concatenate([jnp.zeros((D_MODEL, HEAD_DIM), F32), w_w,
                              jnp.zeros((D_MODEL, LANE - HEAD_DIM - IDX_HEADS), F32)], 1)
    wki = jnp.concatenate([w_ki, w_ki, w_ki_rot, w_ki_rot, w_wpad], 1).astype(BF16)
    wv, wm = w_v.astype(BF16), w_qm.astype(BF16)
    wvt = w_v.T.astype(BF16)
    wwt = jnp.concatenate([w_w.T, jnp.zeros((2 * SUBLANE - IDX_HEADS, D_MODEL), F32)], 0).astype(BF16)
    half = ROT_DIM // 2
    zr = jnp.zeros((IDX_DIM - ROT_DIM,), F32)
    dup = lambda a: jnp.concatenate([a, a]).reshape(1, LANE)
    sgn = dup(jnp.concatenate([-jnp.ones((half,), F32), jnp.ones((half,), F32), zr]))
    gperm = dup(jnp.concatenate([ln_g[half:ROT_DIM], ln_g[:half], zr]))
    brot = dup(jnp.concatenate([-ln_b[half:ROT_DIM], ln_b[:half], zr]))
    lng, lnb = dup(ln_g), dup(ln_b)
    nt = cos.shape[0] // tm
    full = lambda a: pl.BlockSpec(a.shape, lambda i: (0,) * a.ndim)
    rows = lambda w: pl.BlockSpec((tm, w), lambda i: (i, 0))
    heads = lambda h: pl.BlockSpec((h, tm, LANE), lambda i: (0, i, 0))
    colsp = lambda r: pl.BlockSpec((r, tm), lambda i: (0, i))
    tab = pl.BlockSpec((tm, LANE), lambda i: (i % nt, 0))
    sds = jax.ShapeDtypeStruct
    return pl.pallas_call(
        _dsa_proj_kernel,
        grid=(n // tm,),
        in_specs=[rows(D_MODEL), full(wq), full(wk), full(wv), full(wvt), full(wwt), full(wqi), full(wm), full(wki),
                  full(lng), full(lnb), full(sgn), full(gperm), full(brot), tab, tab],
        out_specs=[heads(N_HEADS), heads(KV_HEADS // 2), rows(2 * LANE), rows(2 * LANE),
                   pl.BlockSpec((None, 2 * LANE, tm), lambda i: (i, 0, 0)),
                   colsp(2 * SUBLANE), heads(IDX_HEADS), rows(LANE), rows(LANE), rows(2 * LANE)],
        out_shape=[sds((N_HEADS, n, LANE), BF16), sds((KV_HEADS // 2, n, LANE), BF16), sds((n, 2 * LANE), F32),
                   sds((n, 2 * LANE), F32), sds((n // tm, 2 * LANE, tm), BF16), sds((2 * SUBLANE, n), F32),
                   sds((IDX_HEADS, n, LANE), BF16), sds((n, LANE), F32), sds((n, LANE), BF16),
                   sds((n, 2 * LANE), BF16)],
        compiler_params=_cp(("parallel",), 56),
        name="dsa_proj",
    )(x, wq, wk, wv, wvt, wwt, wqi, wm, wki, lng, lnb, sgn, gperm, brot, cos, sin)


def _indexer_prompt_kernel(qt_ref, kt_ref, qi_ref, ki_ref, wt_ref, sc_ref, *, tq, tk):
    qi = qt_ref[pl.program_id(1)]
    ki = kt_ref[pl.program_id(1)]
    kb = ki_ref[...]
    w = wt_ref[...]
    acc = None
    for h in range(IDX_HEADS):
        d = jnp.maximum(_dot_nt(kb, qi_ref[h]), 0.0) * w[h:h + 1]
        acc = d if acc is None else acc + d
    key = ki * tk + lax.broadcasted_iota(I32, (tk, tq), 0)
    qry = qi * tq + lax.broadcasted_iota(I32, (tk, tq), 1)
    res = jnp.where(key <= qry, acc, NEG)
    for c in range(tq // LANE):
        sc_ref[c] = res[:, c * LANE:(c + 1) * LANE]


def _indexer_prompt(qidx, kib, wt, batch, tq, tk):
    n = kib.shape[0]
    s = n // batch
    nq, nk = s // tq, s // tk
    qtab, ktab = _causal_steps(nq, tq, tk)
    grid_spec = pltpu.PrefetchScalarGridSpec(
        num_scalar_prefetch=2,
        grid=(batch, qtab.shape[0]),
        in_specs=[pl.BlockSpec((IDX_HEADS, tq, LANE), lambda b, t, qt, kt: (0, b * nq + qt[t], 0)),
                  pl.BlockSpec((tk, LANE), lambda b, t, qt, kt: (b * nk + kt[t], 0)),
                  pl.BlockSpec((2 * SUBLANE, tq), lambda b, t, qt, kt: (0, b * nq + qt[t]))],
        out_specs=pl.BlockSpec((tq // LANE, tk, LANE), lambda b, t, qt, kt: (b * nq + qt[t], kt[t], 0)),
    )
    return pl.pallas_call(
        functools.partial(_indexer_prompt_kernel, tq=tq, tk=tk),
        grid_spec=grid_spec,
        out_shape=jax.ShapeDtypeStruct((n // LANE, s, LANE), F32),
        compiler_params=_cp(("parallel", "arbitrary")),
        name="indexer_prompt",
    )(qtab, ktab, qidx, kib, wt)


def _indexer_sample_kernel(pt_ref, q_ref, w_ref, kn_ref, *rest, pp, t_new):
    kt_refs = rest[:pp]
    sc_ref, scn_ref = rest[pp:]
    q = q_ref[0]
    w = w_ref[0]

    def weighted(d):
        acc = None
        for h in range(IDX_HEADS):
            t = d[h * t_new:(h + 1) * t_new] * w[:, h:h + 1]
            acc = t if acc is None else acc + t
        return acc

    for i in range(pp):
        sc_ref[0, :, i * PAGE:(i + 1) * PAGE] = weighted(jnp.maximum(_dot(q, kt_refs[i][...].astype(BF16)), 0.0))

    @pl.when(pl.program_id(1) == pl.num_programs(1) - 1)
    def _():
        sn = weighted(jnp.maximum(_dot_nt(q, kn_ref[0].astype(BF16)), 0.0))
        npad = sn.shape[1]
        t = lax.broadcasted_iota(I32, (t_new, npad), 0)
        u = lax.broadcasted_iota(I32, (t_new, npad), 1)
        scn_ref[0] = jnp.full((t_new, LANE), NEG, F32)
        scn_ref[0, :, :npad] = jnp.where(u <= t, sn, NEG)


def _indexer_sample(page_table, q, w, kidx_new, pool_kidxt, j, pp, t_new):
    db = q.shape[0]
    npg = page_table.shape[1]
    pt = page_table.reshape(-1)
    npad = kidx_new.shape[1]
    per_b = lambda r, c: pl.BlockSpec((1, r, c), lambda b, p, pt_: (b, 0, 0))
    pool = lambda i: pl.BlockSpec((None, None, IDX_DIM, PAGE), lambda b, p, pt_: (j, pt_[b * npg + p * pp + i], 0, 0))
    grid_spec = pltpu.PrefetchScalarGridSpec(
        num_scalar_prefetch=1,
        grid=(db, npg // pp),
        in_specs=[per_b(IDX_HEADS * t_new, IDX_DIM), per_b(t_new, IDX_HEADS), per_b(npad, IDX_DIM)]
        + [pool(i) for i in range(pp)],
        out_specs=[pl.BlockSpec((1, t_new, pp * PAGE), lambda b, p, pt_: (b, 0, p)), per_b(t_new, LANE)],
    )
    return pl.pallas_call(
        functools.partial(_indexer_sample_kernel, pp=pp, t_new=t_new),
        grid_spec=grid_spec,
        out_shape=[jax.ShapeDtypeStruct((db, t_new, npg * PAGE), F32), jax.ShapeDtypeStruct((db, t_new, LANE), F32)],
        compiler_params=_cp(("parallel", "arbitrary")),
        name="indexer_sample",
    )(pt, q, w, kidx_new, *([pool_kidxt] * pp))


def _select_rows_kernel(sc_ref, thr_ref, cut_ref, key_ref, *, topk, cw, tq, idx_bits):
    nch = sc_ref.shape[1] // cw

    def make_keys(c, carry):
        off = pl.multiple_of(c * cw, cw)
        key_ref[:, pl.ds(off, cw)] = _score_keys(sc_ref[:, pl.ds(off, cw)])
        return carry

    lax.fori_loop(0, nch, make_keys, 0)
    lane = lax.broadcasted_iota(I32, (tq, LANE), 1)

    def count(pred):
        def body(c, acc):
            off = pl.multiple_of(c * cw, cw)
            for u in range(cw // LANE):
                kk = key_ref[:, pl.ds(off + u * LANE, LANE)]
                acc = acc + jnp.where(pred(kk, off + u * LANE + lane), 1.0, 0.0)
            return acc
        acc = lax.fori_loop(0, nch, body, jnp.zeros((tq, LANE), F32))
        return jnp.sum(acc, axis=1, keepdims=True)

    thrk = _kth_key(count, (tq, 1), topk)
    cut = _tie_cut(count, thrk, (tq, 1), topk, idx_bits)
    thr_ref[...] = _key_to_score(thrk, (tq, LANE))
    cut_ref[...] = jnp.broadcast_to(cut, (tq, LANE))


def _kth_key(count, shape, topk):
    def value_bit(it, ans):
        cand = ans | jnp.left_shift(jnp.int32(1), 31 - it)
        return jnp.where(count(lambda kk, idx: kk >= (cand ^ MIN_I32)) >= topk, cand, ans)

    return lax.fori_loop(0, 32, value_bit, jnp.zeros(shape, I32)) ^ MIN_I32


def _tie_cut(count, thrk, shape, topk, idx_bits):
    n_gt = count(lambda kk, idx: kk > thrk)
    n_ge = count(lambda kk, idx: kk >= thrk)
    need = topk - n_gt

    def tie_bit(it, m):
        cand = m | jnp.left_shift(jnp.int32(1), idx_bits - 1 - it)
        cnt = count(lambda kk, idx: jnp.where(kk == thrk, idx, cand) < cand)
        return jnp.where(cnt < need, cand, m)

    return lax.cond(jnp.max(jnp.abs(n_ge - topk)) > 0.0,
                    lambda: lax.fori_loop(0, idx_bits, tie_bit, jnp.zeros(shape, I32)),
                    lambda: jnp.full(shape, 2 ** 30, I32))


def _select_rows(sc, topk, cw, tq):
    n, length = sc.shape
    idx_bits = max(1, (length - 1).bit_length())
    return pl.pallas_call(
        functools.partial(_select_rows_kernel, topk=topk, cw=cw, tq=tq, idx_bits=idx_bits),
        grid=(n // tq,),
        in_specs=[pl.BlockSpec((tq, length), lambda i: (i, 0))],
        out_specs=[pl.BlockSpec((tq, LANE), lambda i: (i, 0)), pl.BlockSpec((tq, LANE), lambda i: (i, 0))],
        out_shape=[jax.ShapeDtypeStruct((n, LANE), F32), jax.ShapeDtypeStruct((n, LANE), I32)],
        scratch_shapes=[pltpu.VMEM((tq, length), I32)],
        compiler_params=_cp(("parallel",)),
        name="topk_select_rows",
    )(sc)


def _select_cols_kernel(sc_ref, thr_ref, cut_ref, key_ref, hi_ref, lo_ref, *, topk, cw, idx_bits, nqt):
    qt = pl.program_id(0) % nqt
    nch = (jnp.maximum((qt + 1) * LANE, topk) + cw - 1) // cw
    pk = 2 * SUBLANE
    half = 2 ** 15

    def make_keys(c, carry):
        off = pl.multiple_of(c * cw, cw)
        k = _score_keys(sc_ref[pl.ds(off, cw), :])
        key_ref[pl.ds(off, cw), :] = k
        hi_ref[pl.ds(off, cw), :] = (k >> 16).astype(I16)
        lo_ref[pl.ds(off, cw), :] = ((k & 0xFFFF) - half).astype(I16)
        return carry

    lax.fori_loop(0, nch, make_keys, 0)
    sub = lax.broadcasted_iota(I32, (SUBLANE, LANE), 0)

    def count(pred):
        def body(c, accs):
            off = pl.multiple_of(c * cw, cw)
            accs = list(accs)
            blk = key_ref[pl.ds(off, cw), :]
            for u in range(cw // SUBLANE):
                kk = blk[u * SUBLANE:(u + 1) * SUBLANE]
                accs[u % N_ACC] = accs[u % N_ACC] + jnp.where(pred(kk, off + u * SUBLANE + sub), 1.0, 0.0)
            return tuple(accs)
        accs = lax.fori_loop(0, nch, body, tuple(jnp.zeros((SUBLANE, LANE), F32) for _ in range(N_ACC)))
        return jnp.sum(functools.reduce(lambda a, b: a + b, accs), axis=0, keepdims=True)

    def count16(ref, pred):
        def body(c, accs):
            off = pl.multiple_of(c * cw, cw)
            accs = list(accs)
            blk = ref[pl.ds(off, cw), :]
            for u in range(cw // pk):
                kk = blk[u * pk:(u + 1) * pk]
                accs[u % N_ACC] = accs[u % N_ACC] + jnp.where(pred(kk), jnp.int16(1), jnp.int16(0))
            return tuple(accs)
        accs = lax.fori_loop(0, nch, body, tuple(jnp.zeros((pk, LANE), I16) for _ in range(N_ACC)))
        tot = functools.reduce(lambda a, b: a + b, [a.astype(F32) for a in accs])
        return jnp.sum(tot, axis=0, keepdims=True)

    def packed(v):
        return jnp.broadcast_to(v - half, (pk, LANE)).astype(I16)

    def search16(ref, want):
        def bit(it, ans):
            cand = ans | jnp.left_shift(jnp.int32(1), 15 - it)
            cb = packed(cand)
            return jnp.where(count16(ref, lambda kk: kk >= cb) >= want, cand, ans)
        return lax.fori_loop(0, 16, bit, jnp.zeros((1, LANE), I32))

    hi = search16(hi_ref, topk)
    hb = packed(hi)
    want_lo = topk - count16(hi_ref, lambda kk: kk > hb)

    def mask_lo(c, carry):
        rows = pl.ds(pl.multiple_of(c * cw, cw), cw)
        hi_blk, lo_blk = hi_ref[rows, :], lo_ref[rows, :]
        lo_ref[rows, :] = jnp.concatenate(
            [jnp.where(hi_blk[u * pk:(u + 1) * pk] == hb, lo_blk[u * pk:(u + 1) * pk], jnp.int16(-half))
             for u in range(cw // pk)], axis=0)
        return carry

    lax.fori_loop(0, nch, mask_lo, 0)
    lo = search16(lo_ref, want_lo)
    thrk = (jnp.left_shift(hi, 16) | lo) ^ MIN_I32
    cut = _tie_cut(count, thrk, (1, LANE), topk, idx_bits)
    thr_ref[...] = _key_to_score(thrk, (SUBLANE, LANE))
    cut_ref[...] = jnp.broadcast_to(cut, (SUBLANE, LANE))


def _select_cols(sct, topk, cw):
    nt, s, _ = sct.shape
    nqt = s // LANE
    idx_bits = max(1, (s - 1).bit_length())
    out = pl.BlockSpec((SUBLANE, LANE), lambda i: (0, i))
    return pl.pallas_call(
        functools.partial(_select_cols_kernel, topk=topk, cw=cw, idx_bits=idx_bits, nqt=nqt),
        grid=(nt,),
        in_specs=[pl.BlockSpec((None, s, LANE), lambda i: (i, 0, 0))],
        out_specs=[out, out],
        out_shape=[jax.ShapeDtypeStruct((SUBLANE, nt * LANE), F32), jax.ShapeDtypeStruct((SUBLANE, nt * LANE), I32)],
        scratch_shapes=[pltpu.VMEM((s, LANE), I32), pltpu.VMEM((s, LANE), I16), pltpu.VMEM((s, LANE), I16)],
        compiler_params=_cp(("parallel",)),
        name="topk_select_cols",
    )(sct)


def _dsa_attn_kernel(qt_ref, kt_ref, q_ref, k_ref, vt_ref, sc_ref, thr_ref, cut_ref, o_ref, m_ref, l_ref, acc_ref,
                     *, tq, tk):
    qi = qt_ref[pl.program_id(1)]
    ki = kt_ref[pl.program_id(1)]
    last = ((qi + 1) * tq - 1) // tk

    @pl.when(ki == 0)
    def _():
        m_ref[...] = jnp.full(m_ref.shape, NEG, F32)
        l_ref[...] = jnp.zeros(l_ref.shape, F32)
        acc_ref[...] = jnp.zeros(acc_ref.shape, F32)

    key = ki * tk + lax.broadcasted_iota(I32, (tk, tq), 0)
    qry = qi * tq + lax.broadcasted_iota(I32, (tk, tq), 1)
    sc = jnp.concatenate([sc_ref[c] for c in range(tq // LANE)], axis=1)
    thr = thr_ref[0:1, :]
    cut = cut_ref[0:1, :]
    sel = jnp.where(sc > thr, 1.0, jnp.where(sc == thr, jnp.where(key <= cut, 1.0, 0.0), 0.0))
    keep = jnp.where(key <= qry, sel, 0.0)
    keep3 = jnp.concatenate([keep, keep, keep], axis=1)
    ss = [_dot_nt(k_ref[g // 2], q_ref[3 * g:3 * g + 3].reshape(3 * tq, LANE)) for g in range(KV_HEADS)]
    for g in range(KV_HEADS):
        s = jnp.where(keep3 > 0.0, ss[g], NEG)
        m_new, a, p, l_new = _softmax_step(s, m_ref[g], l_ref[g], axis=0)
        vt = jnp.concatenate([vt_ref[c, g * HEAD_DIM:(g + 1) * HEAD_DIM, :] for c in range(vt_ref.shape[0])], axis=1)
        acc_ref[g] = a * acc_ref[g] + _dot(vt, p.astype(BF16))
        m_ref[g] = m_new
        l_ref[g] = l_new

    @pl.when(ki == last)
    def _():
        for gp in range(KV_HEADS // 2):
            oe = acc_ref[2 * gp] / l_ref[2 * gp]
            oo = acc_ref[2 * gp + 1] / l_ref[2 * gp + 1]
            for r in range(3):
                blk = gp * 3 + r
                ot = jnp.concatenate([oe[:, r * tq:(r + 1) * tq], oo[:, r * tq:(r + 1) * tq]], axis=0)
                o_ref[:, blk * LANE:(blk + 1) * LANE] = ot.T.astype(o_ref.dtype)


def _dsa_attn(q, k, vt, sct, thr, cut, batch, tq, tk):
    n = q.shape[1]
    s = n // batch
    nq, nk = s // tq, s // tk
    qtab, ktab = _causal_steps(nq, tq, tk)
    qcol = pl.BlockSpec((SUBLANE, tq), lambda b, t, qt, kt: (0, b * nq + qt[t]))
    grid_spec = pltpu.PrefetchScalarGridSpec(
        num_scalar_prefetch=2,
        grid=(batch, qtab.shape[0]),
        in_specs=[pl.BlockSpec((N_HEADS, tq, LANE), lambda b, t, qt, kt: (0, b * nq + qt[t], 0)),
                  pl.BlockSpec((KV_HEADS // 2, tk, LANE), lambda b, t, qt, kt: (0, b * nk + kt[t], 0)),
                  pl.BlockSpec((tk // vt.shape[2], KV_HEADS * HEAD_DIM, vt.shape[2]),
                               lambda b, t, qt, kt: (b * nk + kt[t], 0, 0)),
                  pl.BlockSpec((tq // LANE, tk, LANE), lambda b, t, qt, kt: (b * nq + qt[t], kt[t], 0)),
                  qcol, qcol],
        out_specs=pl.BlockSpec((tq, N_HEADS * HEAD_DIM), lambda b, t, qt, kt: (b * nq + qt[t], 0)),
        scratch_shapes=[pltpu.VMEM((KV_HEADS, 1, 3 * tq), F32), pltpu.VMEM((KV_HEADS, 1, 3 * tq), F32),
                        pltpu.VMEM((KV_HEADS, HEAD_DIM, 3 * tq), F32)],
    )
    return pl.pallas_call(
        functools.partial(_dsa_attn_kernel, tq=tq, tk=tk),
        grid_spec=grid_spec,
        out_shape=jax.ShapeDtypeStruct((n, N_HEADS * HEAD_DIM), BF16),
        compiler_params=_cp(("parallel", "arbitrary"), 56),
        name="dsa_attn",
    )(qtab, ktab, q, k, vt, sct, thr, cut)


def _dsa_decode_kernel(pt_ref, q_ref, sc_ref, scn_ref, thr_ref, cut_ref, knt_ref, vn_ref, *rest, pp, ns, t_new, past):
    kt_refs, vt_refs = rest[:pp], rest[pp:2 * pp]
    o_ref, m_ref, l_ref, acc_ref = rest[2 * pp:]
    p_id = pl.program_id(1)
    rows = q_ref.shape[1]
    reps = rows // t_new
    per = pp // ns
    width = per * PAGE

    @pl.when(p_id == 0)
    def _():
        m_ref[...] = jnp.full(m_ref.shape, NEG, F32)
        l_ref[...] = jnp.zeros(l_ref.shape, F32)
        acc_ref[...] = jnp.zeros(acc_ref.shape, F32)

    thr = thr_ref[0][:, :1]
    cut = cut_ref[0][:, :1]
    q = q_ref[0]

    def keep_rows(sc, col, extra):
        sel = jnp.where(sc > thr, 1.0, jnp.where(sc == thr, jnp.where(col <= cut, 1.0, 0.0), 0.0))
        if extra is not None:
            sel = jnp.where(extra, sel, 0.0)
        return jnp.concatenate([sel] * reps, axis=0)

    for st in range(ns):
        pages = range(st * per, (st + 1) * per)
        col = p_id * (pp * PAGE) + st * width + lax.broadcasted_iota(I32, (t_new, width), 1)
        keep = keep_rows(sc_ref[0, :, st * width:(st + 1) * width], col, None)
        s = jnp.concatenate([_dot(q, kt_refs[i][...].astype(BF16)) for i in pages], axis=1)
        s = jnp.where(keep > 0.0, s, NEG)
        m_new, a, p, l_new = _softmax_step(s, m_ref[st], l_ref[st])
        pb = p.astype(BF16)
        pv = None
        for n_, i in enumerate(pages):
            t = _dot_nt(pb[:, n_ * PAGE:(n_ + 1) * PAGE], vt_refs[i][...].astype(BF16))
            pv = t if pv is None else pv + t
        acc_ref[st] = a * acc_ref[st] + pv
        m_ref[st] = m_new
        l_ref[st] = l_new

    @pl.when(p_id == pl.num_programs(1) - 1)
    def _():
        m, l, acc = _merge_streams([m_ref[st] for st in range(ns)], [l_ref[st] for st in range(ns)],
                                   [acc_ref[st] for st in range(ns)])
        npad = vn_ref.shape[1]
        t = lax.broadcasted_iota(I32, (t_new, npad), 0)
        u = lax.broadcasted_iota(I32, (t_new, npad), 1)
        keep_n = keep_rows(scn_ref[0][:, :npad], past + u, u <= t)
        s2 = jnp.where(keep_n > 0.0, _dot(q, knt_ref[0].astype(BF16)), NEG)
        m2, a2, p2, l2 = _softmax_step(s2, m, l)
        o_ref[0] = (a2 * acc + _dot(p2.astype(BF16), vn_ref[0].astype(BF16))) / l2


def _dsa_decode(page_table, q, sc, scn, thr, cut, knt, v_new, pool_kt, pool_vt, j, pp, ns, t_new):
    db, rows, wd = q.shape
    npg = page_table.shape[1]
    pt = page_table.reshape(-1)
    npad = v_new.shape[1]
    per_b = lambda r, c: pl.BlockSpec((1, r, c), lambda b, p, pt_: (b, 0, 0))
    pool = lambda i: pl.BlockSpec((None, None, wd, PAGE), lambda b, p, pt_: (j, pt_[b * npg + p * pp + i], 0, 0))
    grid_spec = pltpu.PrefetchScalarGridSpec(
        num_scalar_prefetch=1,
        grid=(db, npg // pp),
        in_specs=[per_b(rows, wd), pl.BlockSpec((1, t_new, pp * PAGE), lambda b, p, pt_: (b, 0, p)),
                  per_b(t_new, LANE), per_b(t_new, LANE), per_b(t_new, LANE), per_b(wd, npad), per_b(npad, wd)]
        + [pool(i) for i in range(pp)] + [pool(i) for i in range(pp)],
        out_specs=per_b(rows, wd),
        scratch_shapes=[pltpu.VMEM((ns, rows, 1), F32), pltpu.VMEM((ns, rows, 1), F32),
                        pltpu.VMEM((ns, rows, wd), F32)],
    )
    return pl.pallas_call(
        functools.partial(_dsa_decode_kernel, pp=pp, ns=ns, t_new=t_new, past=npg * PAGE),
        grid_spec=grid_spec,
        out_shape=jax.ShapeDtypeStruct((db, rows, wd), F32),
        compiler_params=_cp(("parallel", "arbitrary")),
        name="dsa_decode",
    )(pt, q, sc, scn, thr, cut, knt, v_new, *([pool_kt] * pp), *([pool_vt] * pp))


def _mem_kv_kernel(m_ref, w_ref, kf_ref, vf_ref, kb_ref, vb_ref):
    y = _dot(m_ref[...].astype(BF16), w_ref[...])
    w = MEM_HEADS * HEAD_DIM
    kf_ref[...] = y[:, :w]
    vf_ref[...] = y[:, w:]
    kb_ref[...] = y[:, :w].astype(BF16)
    vb_ref[...] = y[:, w:].astype(BF16)


def _mem_kv(mem2d, w):
    n = mem2d.shape[0]
    wd = MEM_HEADS * HEAD_DIM
    wb = w.astype(BF16)
    spec = pl.BlockSpec((n, wd), lambda i: (0, 0))
    return pl.pallas_call(
        _mem_kv_kernel,
        grid=(1,),
        in_specs=[pl.BlockSpec(mem2d.shape, lambda i: (0, 0)), pl.BlockSpec(wb.shape, lambda i: (0, 0))],
        out_specs=[spec] * 4,
        out_shape=[jax.ShapeDtypeStruct((n, wd), F32)] * 2 + [jax.ShapeDtypeStruct((n, wd), BF16)] * 2,
        compiler_params=_cp(("arbitrary",)),
        name="mem_kv",
    )(mem2d, wb)


def _mem_attend_kernel(q_ref, mk_ref, mv_ref, o_ref, *, bb, kv_t):
    tt = q_ref.shape[1]
    lane = lax.broadcasted_iota(I32, (tt, LANE), 1)
    low = lane < HEAD_DIM
    for b in range(bb):
        for blk in range(MEM_HEADS // 2):
            sl = slice(blk * LANE, (blk + 1) * LANE)
            qb = q_ref[b, :, sl]
            mkb = (mk_ref[b, sl, :] if kv_t else mk_ref[b, :, sl]).astype(BF16)
            mvb = (mv_ref[b, sl, :] if kv_t else mv_ref[b, :, sl]).astype(BF16)
            res = []
            for half in range(2):
                qh = jnp.where(low if half == 0 else jnp.logical_not(low), qb, jnp.zeros_like(qb))
                s = _dot(qh, mkb) if kv_t else _dot_nt(qh, mkb)
                p = jnp.exp(s - jnp.max(s, axis=1, keepdims=True))
                p = (p / jnp.sum(p, axis=1, keepdims=True)).astype(BF16)
                res.append(_dot_nt(p, mvb) if kv_t else _dot(p, mvb))
            o_ref[b, :, sl] = jnp.where(low, res[0], res[1]).astype(o_ref.dtype)


def _mem_attend(q, mk, mv, bb, tt, kv_t, layer=None):
    bm, t, wd = q.shape
    if layer is None:
        kv = pl.BlockSpec((bb,) + mk.shape[1:], lambda b, i: (b, 0, 0))
    else:
        kv = pl.BlockSpec((None, bb) + mk.shape[2:], lambda b, i: (layer, b, 0, 0))
    return pl.pallas_call(
        functools.partial(_mem_attend_kernel, bb=bb, kv_t=kv_t),
        grid=(bm // bb, t // tt),
        in_specs=[pl.BlockSpec((bb, tt, wd), lambda b, i: (b, i, 0)), kv, kv],
        out_specs=pl.BlockSpec((bb, tt, wd), lambda b, i: (b, i, 0)),
        out_shape=jax.ShapeDtypeStruct((bm, t, wd), BF16),
        compiler_params=_cp(("parallel", "parallel")),
        name="mem_attend",
    )(q, mk, mv)


def _split_bf16(x):
    hi = x.astype(BF16)
    return hi, (x - hi.astype(F32)).astype(BF16)


def _outproj_router_kernel(x_ref, mix_ref, mem_ref, wo1_ref, wo2_ref, g_ref, b_ref, wrh_ref, wrl_ref, rb_ref,
                           x1_ref, gate_ref):
    y = ALPHA * x_ref[...] + _dot(mix_ref[...].astype(BF16), wo1_ref[...]) + _dot(mem_ref[...], wo2_ref[...])
    x1 = _ln(y, g_ref[...], b_ref[...])
    x1_ref[...] = x1
    xh, xl = _split_bf16(x1)
    logit = _dot_nt(wrh_ref[...], xh) + _dot_nt(wrh_ref[...], xl) + _dot_nt(wrl_ref[...], xh)
    s = 1.0 / (1.0 + jnp.exp(-logit))
    sb = s + rb_ref[...]
    epg = N_EXPERTS // N_GROUPS
    rows = [sb[e:e + 1] for e in range(N_EXPERTS)]
    gscore = []
    for g in range(N_GROUPS):
        v = rows[g * epg:(g + 1) * epg]
        best = None
        for a in range(epg):
            for b in range(a + 1, epg):
                pr = v[a] + v[b]
                best = pr if best is None else jnp.maximum(best, pr)
        gscore.append(best)
    gmax = functools.reduce(jnp.maximum, gscore)
    taken = jnp.zeros_like(gmax)
    gsel = []
    for g in range(N_GROUPS):
        pick = jnp.where(gscore[g] == gmax, 1.0, 0.0) * (1.0 - taken)
        taken = taken + pick
        gsel.append(pick)
    val = [jnp.where(gsel[e // epg] > 0.0, rows[e], NEG) for e in range(N_EXPERTS)]
    chosen = [jnp.zeros_like(gmax) for _ in range(N_EXPERTS)]
    for _ in range(2):
        vmax = functools.reduce(jnp.maximum, val)
        taken = jnp.zeros_like(gmax)
        for e in range(N_EXPERTS):
            pick = jnp.where(val[e] == vmax, 1.0, 0.0) * (1.0 - taken)
            taken = taken + pick
            chosen[e] = chosen[e] + pick
            val[e] = jnp.where(pick > 0.0, 2.0 * NEG, val[e])
    wsel = [jnp.where(chosen[e] > 0.0, s[e:e + 1], 0.0) for e in range(N_EXPERTS)]
    wsum = functools.reduce(lambda a, b: a + b, wsel)
    gate_ref[...] = jnp.concatenate([w / wsum for w in wsel], axis=0)


def _outproj_router(x, mix, mem, wo1, wo2, g, b, wrh, wrl, rb, tm):
    n = x.shape[0]
    full = lambda a: pl.BlockSpec(a.shape, lambda i: (0,) * a.ndim)
    rows = lambda w: pl.BlockSpec((tm, w), lambda i: (i, 0))
    return pl.pallas_call(
        _outproj_router_kernel,
        grid=(n // tm,),
        in_specs=[rows(D_MODEL), rows(mix.shape[1]), rows(mem.shape[1]), full(wo1), full(wo2), full(g), full(b),
                  full(wrh), full(wrl), full(rb)],
        out_specs=[rows(D_MODEL), pl.BlockSpec((N_EXPERTS, tm), lambda i: (0, i))],
        out_shape=[jax.ShapeDtypeStruct((n, D_MODEL), F32), jax.ShapeDtypeStruct((N_EXPERTS, n), F32)],
        compiler_params=_cp(("parallel",)),
        name="outproj_router",
    )(x, mix, mem, wo1, wo2, g, b, wrh, wrl, rb)


def _moe_kernel(x_ref, gate_ref, wgu_ref, wd_ref, g_ref, b_ref, o_ref, acc_ref, xb_ref):
    e = pl.program_id(1)

    @pl.when(e == 0)
    def _():
        acc_ref[...] = jnp.zeros(acc_ref.shape, F32)
        xb_ref[...] = x_ref[...].astype(BF16)

    gate = gate_ref[...]
    lane = lax.broadcasted_iota(I32, gate.shape, 1)
    gcol = jnp.sum(jnp.where(lane == e, gate, 0.0), axis=1, keepdims=True)
    hgu = _dot(xb_ref[...], wgu_ref[...])
    hg, hu = hgu[:, :D_EXPERT], hgu[:, D_EXPERT:]
    h = hg * (1.0 / (1.0 + jnp.exp(-hg))) * hu * gcol
    acc_ref[...] += _dot(h.astype(BF16), wd_ref[...])

    @pl.when(e == pl.num_programs(1) - 1)
    def _():
        o_ref[...] = _ln(ALPHA * x_ref[...] + acc_ref[...], g_ref[...], b_ref[...])


def _moe(x, gate, wgu, wd, g, b, tm):
    n = x.shape[0]
    full = lambda a: pl.BlockSpec(a.shape, lambda i, e: (0,) * a.ndim)
    return pl.pallas_call(
        _moe_kernel,
        grid=(n // tm, N_EXPERTS),
        in_specs=[pl.BlockSpec((tm, D_MODEL), lambda i, e: (i, 0)), pl.BlockSpec((tm, N_EXPERTS), lambda i, e: (i, 0)),
                  pl.BlockSpec((None, D_MODEL, 2 * D_EXPERT), lambda i, e: (e, 0, 0)),
                  pl.BlockSpec((None, D_EXPERT, D_MODEL), lambda i, e: (e, 0, 0)), full(g), full(b)],
        out_specs=pl.BlockSpec((tm, D_MODEL), lambda i, e: (i, 0)),
        out_shape=jax.ShapeDtypeStruct((n, D_MODEL), F32),
        scratch_shapes=[pltpu.VMEM((tm, D_MODEL), F32), pltpu.VMEM((tm, D_MODEL), BF16)],
        compiler_params=_cp(("parallel", "arbitrary")),
        name="moe",
    )(x, gate, wgu, wd, g, b)


def _tile(n, pref):
    t = min(n, pref)
    while n % t:
        t //= 2
    return t


def kernel(x_prompt, x_sample, cache_mla_ckv, cache_mla_krope, cache_dsa_k, cache_dsa_v, cache_dsa_kidx, cache_mem_k, cache_mem_v, page_table, mem_prompt, w_in_mla, mla_q_norm, mla_kv_norm, w_uq, w_uk, w_uv, w_in_dsa, idx_k_norm_g, idx_k_norm_b, w_mem_kv, w_out, ln1_g, ln1_b, ln2_g, ln2_b, w_router, router_bias, w_gate, w_up, w_down):
    B, S, _ = x_prompt.shape
    DB, T, _ = x_sample.shape
    n_mem = mem_prompt.shape[1]
    npg = page_table.shape[1]
    past = npg * PAGE
    n_pool = cache_mla_ckv.shape[1]
    NP, NS = B * S, DB * T
    TPAD = 16
    pp_dsa = pp_idx = 32 if npg % 32 == 0 else npg
    pp_mla = 16 if npg % 16 == 0 else npg
    ns_dsa = 4 if pp_dsa % 4 == 0 else 1
    ns_mla = 2 if pp_mla % 2 == 0 else 1
    tm_p, tm_s = _tile(NP, 512), _tile(NS, 512)
    t_att = _tile(S, 512)
    KVW = KV_HEADS * HEAD_DIM

    pos_p = jnp.arange(S, dtype=jnp.int32)
    pos_s = jnp.tile(past + jnp.arange(T, dtype=jnp.int32), DB)
    cos_mp, sin_mp = _rope_tables(pos_p, QK_ROPE, LANE, QK_NOPE, LANE)
    cos_ms, sin_ms = _rope_tables(pos_s, QK_ROPE, LANE, QK_NOPE, LANE)
    cos_mr, sin_mr = _rope_tables(pos_s, QK_ROPE, QK_ROPE, 0, N_HEADS * QK_ROPE)
    cos_dp, sin_dp = _rope_tables(pos_p, ROT_DIM, HEAD_DIM, 0, LANE)
    cos_ds, sin_ds = _rope_tables(pos_s, ROT_DIM, HEAD_DIM, 0, LANE)

    wrt = w_router.T
    wrh = wrt.astype(BF16)
    wrl = (wrt - wrh.astype(F32)).astype(BF16)
    rb = router_bias.reshape(N_EXPERTS, 1).astype(F32)
    perm = jnp.asarray(DSA_PERM)
    pool_kt = jnp.transpose(cache_dsa_k, (0, 1, 3, 4, 2)).reshape(-1, n_pool, KVW, PAGE)
    pool_vt = jnp.transpose(cache_dsa_v, (0, 1, 3, 4, 2)).reshape(-1, n_pool, KVW, PAGE)
    pool_kidxt = jnp.transpose(cache_dsa_kidx, (0, 1, 3, 2))
    pool_krt = jnp.transpose(cache_mla_krope, (0, 1, 3, 2))
    mem_kt = jnp.transpose(cache_mem_k, (0, 1, 3, 4, 2)).reshape(DEPTH, DB, MEM_HEADS * HEAD_DIM, n_mem)
    mem_vt = jnp.transpose(cache_mem_v, (0, 1, 3, 4, 2)).reshape(DEPTH, DB, MEM_HEADS * HEAD_DIM, n_mem)
    mem2d = mem_prompt.reshape(B * n_mem, D_MODEL)

    def pad_new(a):
        a = a.reshape(DB, T, a.shape[-1])
        return jnp.concatenate([a, jnp.zeros((DB, TPAD - T, a.shape[-1]), a.dtype)], axis=1)

    def post(x, mix, mem, i, wo_rows, tm):
        wo = w_out[i]
        wo1 = wo[:N_HEADS * HEAD_DIM]
        if wo_rows is not None:
            wo1 = wo1.reshape(N_HEADS, HEAD_DIM, D_MODEL)[wo_rows].reshape(N_HEADS * HEAD_DIM, D_MODEL)
        x1, gate_t = _outproj_router(x, mix, mem, wo1.astype(BF16), wo[N_HEADS * HEAD_DIM:].astype(BF16),
                                     ln1_g[i].reshape(1, -1), ln1_b[i].reshape(1, -1), wrh, wrl, rb, tm)
        wgu = jnp.concatenate([w_gate[i], w_up[i]], axis=-1).astype(BF16)
        return _moe(x1, gate_t.T, wgu, w_down[i].astype(BF16), ln2_g[i].reshape(1, -1), ln2_b[i].reshape(1, -1),
                    _tile(x.shape[0], 1024))

    xp = x_prompt.reshape(NP, D_MODEL)
    xs = x_sample.reshape(NS, D_MODEL)
    ckv_p, kr_p, ckv_s, kr_s = [], [], [], []
    k_p, v_p, ki_p, k_s, v_s, ki_s = [], [], [], [], [], []
    mk_list, mv_list = [], []
    for i in range(DEPTH):
        j = i // 2
        mkf, mvf, mkb, mvb = _mem_kv(mem2d, w_mem_kv[i])
        mk_list.append(mkf.reshape(B, n_mem, MEM_HEADS, HEAD_DIM))
        mv_list.append(mvf.reshape(B, n_mem, MEM_HEADS, HEAD_DIM))
        if i % 2 == 0:
            q, k, vt, ckv, krpad, qm_p = _mla_proj_prompt(xp, w_in_mla[j], mla_q_norm[j], mla_kv_norm[j], w_uq[j],
                                                          w_uk[j], w_uv[j], cos_mp, sin_mp, tm_p)
            mix_p = _mla_flash(q, k, vt, B, t_att, _tile(S, 1024))
            ckv_p.append(ckv.reshape(B, S, KV_LORA))
            kr_p.append(krpad[:, QK_NOPE:QK_NOPE + QK_ROPE].reshape(B, S, QK_ROPE))
            qlat, qr, ckv, krpad, qm_s = _mla_proj_sample(xs, w_in_mla[j], mla_q_norm[j], mla_kv_norm[j], w_uq[j],
                                                          w_uk[j], cos_ms, sin_ms, cos_mr, sin_mr, tm_s)
            kr = krpad[:, QK_NOPE:QK_NOPE + QK_ROPE]
            ckv_s.append(ckv.reshape(DB, T, KV_LORA))
            kr_s.append(kr.reshape(DB, T, QK_ROPE))
            qlat = qlat.reshape(DB, T, N_HEADS, KV_LORA).transpose(0, 2, 1, 3).reshape(DB, N_HEADS * T, KV_LORA)
            qr = qr.reshape(DB, T, N_HEADS, QK_ROPE).transpose(0, 2, 1, 3).reshape(DB, N_HEADS * T, QK_ROPE)
            o_lat = _mla_decode(page_table, qlat, qr, pad_new(ckv), pad_new(kr), cache_mla_ckv, pool_krt,
                                j, pp_mla, ns_mla, T)
            o_lat = o_lat.reshape(DB, N_HEADS, T, KV_LORA).transpose(1, 0, 2, 3).reshape(N_HEADS, NS, KV_LORA)
            o = _head_matmul(o_lat, jnp.transpose(w_uv[j], (1, 0, 2)).astype(BF16))
            mix_s = o.transpose(1, 0, 2).reshape(NS, N_HEADS * HEAD_DIM)
            wo_rows = None
        else:
            q, kb, kf, vf, vt, wt, qidx, kw, kib, qm_p = _dsa_proj(xp, w_in_dsa[j], idx_k_norm_g[j],
                                                                   idx_k_norm_b[j], cos_dp, sin_dp, tm_p)
            k_p.append(kf.reshape(B, S, KV_HEADS, HEAD_DIM))
            v_p.append(vf.reshape(B, S, KV_HEADS, HEAD_DIM))
            ki_p.append(kw[:, :IDX_DIM].reshape(B, S, IDX_DIM))
            sct = _indexer_prompt(qidx, kib, wt, B, t_att, t_att)
            thr, cut = _select_cols(sct, min(TOPK_MAX, S // 4), _tile(S, 512))
            mix_p = _dsa_attn(q, kb, vt, sct, thr, cut, B, t_att, t_att)

            q, kb, kf, vf, vt, wt, qidx, kw, kib, qm_s = _dsa_proj(xs, w_in_dsa[j], idx_k_norm_g[j],
                                                                   idx_k_norm_b[j], cos_ds, sin_ds, tm_s)
            k_s.append(kf.reshape(DB, T, KV_HEADS, HEAD_DIM))
            v_s.append(vf.reshape(DB, T, KV_HEADS, HEAD_DIM))
            kidx_new = kw[:, :IDX_DIM]
            ki_s.append(kidx_new.reshape(DB, T, IDX_DIM))
            qc = jnp.stack([qidx[h, :, (h % 2) * HEAD_DIM:(h % 2 + 1) * HEAD_DIM] for h in range(IDX_HEADS)])
            qc = qc.reshape(IDX_HEADS, DB, T, IDX_DIM).transpose(1, 0, 2, 3).reshape(DB, IDX_HEADS * T, IDX_DIM)
            w_s = kw[:, HEAD_DIM:HEAD_DIM + IDX_HEADS].reshape(DB, T, IDX_HEADS)
            sc_past, sc_new = _indexer_sample(page_table, qc, w_s, pad_new(kidx_new), pool_kidxt, j, pp_idx, T)
            topk_s = min(TOPK_MAX, (past + T) // 4)
            cw_s = 5 * LANE if (past + LANE) % (5 * LANE) == 0 else LANE
            sc_all = jnp.concatenate([sc_past, sc_new], axis=-1).reshape(NS, past + LANE)
            thr, cut = _select_rows(sc_all, topk_s, cw_s, _tile(NS, 128))
            qh = jnp.stack([q[h, :, ((h // 3) % 2) * HEAD_DIM:((h // 3) % 2 + 1) * HEAD_DIM] for h in range(N_HEADS)])
            qh = qh.reshape(KV_HEADS, 3, DB, T, HEAD_DIM).transpose(2, 0, 1, 3, 4)
            eye = jnp.eye(KV_HEADS, dtype=BF16)
            qbd = (qh[:, :, :, :, None, :] * eye[None, :, None, None, :, None]).reshape(DB, N_HEADS * T, KVW)
            o = _dsa_decode(page_table, qbd, sc_past, sc_new, thr.reshape(DB, T, LANE), cut.reshape(DB, T, LANE),
                            pad_new(kf).transpose(0, 2, 1), pad_new(vf), pool_kt, pool_vt, j, pp_dsa, ns_dsa, T)
            o = o.reshape(DB, KV_HEADS, 3, T, KV_HEADS, HEAD_DIM)
            o = jnp.stack([o[:, g, :, :, g, :] for g in range(KV_HEADS)], axis=1)
            o = o.reshape(DB, N_HEADS, T, HEAD_DIM)[:, perm]
            mix_s = o.transpose(0, 2, 1, 3).reshape(NS, N_HEADS * HEAD_DIM)
            wo_rows = perm
        memo_p = _mem_attend(qm_p.reshape(B, S, -1), mkb.reshape(B, n_mem, -1), mvb.reshape(B, n_mem, -1), 1,
                             _tile(S, 512), False)
        qm_s3 = qm_s.reshape(DB, T, -1)
        qm_s3 = jnp.concatenate([qm_s3, jnp.zeros((DB, TPAD - T, qm_s3.shape[-1]), BF16)], axis=1)
        memo_s = _mem_attend(qm_s3, mem_kt, mem_vt, _tile(DB, 8), TPAD, True, layer=i)[:, :T]
        xp = post(xp, mix_p, memo_p.reshape(NP, -1), i, wo_rows, tm_p)
        xs = post(xs, mix_s, memo_s.reshape(NS, -1), i, wo_rows, tm_s)
    return (xp.reshape(B, S, D_MODEL), xs.reshape(DB, T, D_MODEL),
            jnp.stack(ckv_p), jnp.stack(kr_p), jnp.stack(k_p), jnp.stack(v_p), jnp.stack(ki_p),
            jnp.stack(mk_list), jnp.stack(mv_list),
            jnp.stack(ckv_s), jnp.stack(kr_s), jnp.stack(k_s), jnp.stack(v_s), jnp.stack(ki_s))
```

```python
import functools

import jax
import jax.numpy as jnp
from jax import lax
from jax.experimental import pallas as pl
from jax.experimental.pallas import tpu as pltpu

F32 = jnp.float32
BF16 = jnp.bfloat16
I32 = jnp.int32
I16 = jnp.int16

D_MODEL = 1024
DEPTH = 4
PAGE = 128
HEAD_DIM = 64
N_HEADS = 12
MEM_HEADS = 4
Q_LORA = 384
KV_LORA = 256
QK_NOPE = 64
QK_ROPE = 32
MLA_SCALE = (QK_NOPE + QK_ROPE) ** -0.5
HEAD_SCALE = HEAD_DIM ** -0.5
LOG2E = 1.4426950408889634
MLA_QSCALE = MLA_SCALE * LOG2E
DSA_QSCALE = HEAD_SCALE * LOG2E
KV_HEADS = 4
IDX_HEADS = 8
IDX_DIM = 64
TOPK_MAX = 256
ROT_DIM = HEAD_DIM // 4
ROPE_THETA = 500000.0
N_EXPERTS = 16
N_GROUPS = 4
D_EXPERT = 256
ALPHA = (2 * DEPTH) ** 0.25
LN_EPS = 1e-5
RMS_EPS = 1e-6
NEG = -1e30
LANE = 128
SUBLANE = 8
MIN_I32 = -2 ** 31
N_ACC = 8
DSA_PERM = (0, 3, 1, 4, 2, 5, 6, 9, 7, 10, 8, 11)

NT = (((1,), (1,)), ((), ()))


def _cp(sem, vmem_mb=48):
    return pltpu.CompilerParams(dimension_semantics=sem, vmem_limit_bytes=vmem_mb * 1024 * 1024)


def _dot(a, b):
    return jnp.dot(a, b, preferred_element_type=F32)


def _dot_nt(a, b):
    return lax.dot_general(a, b, NT, preferred_element_type=F32)


def _rms(x, g):
    return x * lax.rsqrt(jnp.mean(x * x, axis=-1, keepdims=True) + RMS_EPS) * g


def _ln(x, g, b):
    mu = jnp.mean(x, axis=-1, keepdims=True)
    xc = x - mu
    var = jnp.mean(xc * xc, axis=-1, keepdims=True)
    return xc * lax.rsqrt(var + LN_EPS) * g + b


def _softmax_step(s, m_prev, l_prev, axis=1):
    m_new = jnp.maximum(m_prev, jnp.max(s, axis=axis, keepdims=True))
    a = jnp.exp2(m_prev - m_new)
    p = jnp.exp2(s - m_new)
    return m_new, a, p, a * l_prev + jnp.sum(p, axis=axis, keepdims=True)


def _merge_streams(ms, ls, accs):
    m = functools.reduce(jnp.maximum, ms)
    ws = [jnp.exp2(mi - m) for mi in ms]
    l = functools.reduce(lambda a, b: a + b, [w * li for w, li in zip(ws, ls)])
    acc = functools.reduce(lambda a, b: a + b, [w * ai for w, ai in zip(ws, accs)])
    return m, l, acc


def _score_keys(s):
    b = lax.bitcast_convert_type(s, I32)
    return jnp.where(s == 0.0, 0, b ^ ((b >> 31) & 0x7FFFFFFF))


def _key_to_score(k, shape):
    return lax.bitcast_convert_type(jnp.broadcast_to(k ^ ((k >> 31) & 0x7FFFFFFF), shape), F32)


def _rope_tables(pos, rot_dim, period, offset, width):
    half = rot_dim // 2
    inv = ROPE_THETA ** (-jnp.arange(half, dtype=jnp.float32) / half)
    ang = pos.astype(jnp.float32)[:, None] * inv
    c, s = jnp.cos(ang), jnp.sin(ang)
    t = pos.shape[0]
    cg = jnp.ones((t, period), F32).at[:, offset:offset + rot_dim].set(jnp.concatenate([c, c], 1))
    sg = jnp.zeros((t, period), F32).at[:, offset:offset + rot_dim].set(jnp.concatenate([s, s], 1))
    return jnp.tile(cg, (1, width // period)), jnp.tile(sg, (1, width // period))


def _rot_cols(w, half):
    return jnp.concatenate([-w[..., half:2 * half], w[..., :half]], axis=-1)


def _rot_heads(w, heads, dim, rot):
    w3 = w.reshape(w.shape[0], heads, dim)
    r = jnp.concatenate([_rot_cols(w3[..., :rot], rot // 2), jnp.zeros_like(w3[..., rot:])], axis=-1)
    return r.reshape(w.shape[0], heads * dim)


def _mla_front(x_ref, w1_ref, w2_ref, qn_ref, kvn_ref, cos_ref, sin_ref):
    xb = x_ref[...].astype(BF16)
    y = _dot(xb, w1_ref[...])
    cqn = _rms(y[:, :Q_LORA], qn_ref[...]).astype(BF16)
    ckv = _rms(y[:, Q_LORA:Q_LORA + KV_LORA], kvn_ref[...])
    qmem = y[:, 640:896]
    cos = cos_ref[...]
    sin = sin_ref[...]
    krp = y[:, 896:1024] * cos + y[:, 1024:1152] * sin
    q2 = _dot(cqn, w2_ref[...])
    return q2, ckv, qmem, krp, cos, sin


def _mla_proj_prompt_kernel(x_ref, w1_ref, w2_ref, wk_ref, wvt_ref, qn_ref, kvn_ref, cos_ref, sin_ref,
                            q_ref, k_ref, vt_ref, ckv_ref, kr_ref, qm_ref):
    q2, ckv, qmem, krp, cos, sin = _mla_front(x_ref, w1_ref, w2_ref, qn_ref, kvn_ref, cos_ref, sin_ref)
    hw = N_HEADS * LANE
    for h in range(N_HEADS):
        qh = q2[:, h * LANE:(h + 1) * LANE] * cos + q2[:, hw + h * LANE:hw + (h + 1) * LANE] * sin
        q_ref[h] = (qh * MLA_QSCALE).astype(BF16)
    ckv_ref[...] = ckv
    kr_ref[...] = krp
    qm_ref[...] = (qmem * HEAD_SCALE).astype(BF16)
    cb = ckv.astype(BF16)
    kn = _dot(cb, wk_ref[...])
    for h in range(N_HEADS):
        k_ref[h] = (kn[:, h * LANE:(h + 1) * LANE] + krp).astype(BF16)
    vt_ref[...] = _dot_nt(wvt_ref[...], cb).astype(BF16)


def _mla_proj_sample_kernel(x_ref, w1_ref, w2_ref, wukt_ref, qn_ref, kvn_ref, cos_ref, sin_ref,
                            cosr_ref, sinr_ref, qlat_ref, qr_ref, ckv_ref, kr_ref, qm_ref):
    q2, ckv, qmem, krp, _, _ = _mla_front(x_ref, w1_ref, w2_ref, qn_ref, kvn_ref, cos_ref, sin_ref)
    hw = N_HEADS * LANE
    rw = N_HEADS * QK_ROPE
    for h in range(N_HEADS):
        qh = (q2[:, h * LANE:(h + 1) * LANE] * MLA_QSCALE).astype(BF16)
        qlat_ref[:, h * KV_LORA:(h + 1) * KV_LORA] = _dot(qh, wukt_ref[h]).astype(BF16)
    qr = q2[:, hw:hw + rw] * cosr_ref[...] + q2[:, hw + rw:hw + 2 * rw] * sinr_ref[...]
    qr_ref[...] = (qr * MLA_QSCALE).astype(BF16)
    ckv_ref[...] = ckv
    kr_ref[...] = krp
    qm_ref[...] = (qmem * HEAD_SCALE).astype(BF16)


def _mla_w1(w_in):
    w_cq, w_ckv = w_in[:, :Q_LORA], w_in[:, Q_LORA:Q_LORA + KV_LORA]
    w_kr = w_in[:, Q_LORA + KV_LORA:Q_LORA + KV_LORA + QK_ROPE]
    w_qm = w_in[:, Q_LORA + KV_LORA + QK_ROPE:]
    z = lambda n: jnp.zeros((D_MODEL, n), F32)
    kr_pad = jnp.concatenate([z(QK_NOPE), w_kr, z(LANE - QK_NOPE - QK_ROPE)], 1)
    kr_rot = jnp.concatenate([z(QK_NOPE), _rot_cols(w_kr, QK_ROPE // 2), z(LANE - QK_NOPE - QK_ROPE)], 1)
    return jnp.concatenate([w_cq, w_ckv, w_qm, kr_pad, kr_rot], 1).astype(BF16)


def _mla_proj_prompt(x, w_in, q_norm, kv_norm, w_uq, w_uk, w_uv, cos, sin, tm):
    n = x.shape[0]
    w1 = _mla_w1(w_in)
    zq = jnp.zeros((Q_LORA, N_HEADS, LANE - QK_NOPE - QK_ROPE), F32)
    uq_pad = jnp.concatenate([w_uq, zq], -1).reshape(Q_LORA, N_HEADS * LANE)
    uq_rot = jnp.concatenate([jnp.zeros((Q_LORA, N_HEADS, QK_NOPE), F32),
                              _rot_cols(w_uq[..., QK_NOPE:], QK_ROPE // 2), zq], -1).reshape(Q_LORA, N_HEADS * LANE)
    w2 = jnp.concatenate([uq_pad, uq_rot], 1).astype(BF16)
    wk = jnp.concatenate([w_uk, jnp.zeros((KV_LORA, N_HEADS, LANE - QK_NOPE), F32)], -1)
    wk = wk.reshape(KV_LORA, N_HEADS * LANE).astype(BF16)
    wvt = w_uv.reshape(KV_LORA, N_HEADS * HEAD_DIM).T.astype(BF16)
    nt = cos.shape[0] // tm
    full = lambda a: pl.BlockSpec(a.shape, lambda i: (0,) * a.ndim)
    rows = lambda w: pl.BlockSpec((tm, w), lambda i: (i, 0))
    heads = pl.BlockSpec((N_HEADS, tm, LANE), lambda i: (0, i, 0))
    tab = pl.BlockSpec((tm, LANE), lambda i: (i % nt, 0))
    qn, kvn = q_norm.reshape(1, -1), kv_norm.reshape(1, -1)
    return pl.pallas_call(
        _mla_proj_prompt_kernel,
        grid=(n // tm,),
        in_specs=[rows(D_MODEL), full(w1), full(w2), full(wk), full(wvt), full(qn), full(kvn), tab, tab],
        out_specs=[heads, heads, pl.BlockSpec((None, N_HEADS * HEAD_DIM, tm), lambda i: (i, 0, 0)), rows(KV_LORA),
                   rows(LANE), rows(MEM_HEADS * HEAD_DIM)],
        out_shape=[jax.ShapeDtypeStruct((N_HEADS, n, LANE), BF16), jax.ShapeDtypeStruct((N_HEADS, n, LANE), BF16),
                   jax.ShapeDtypeStruct((n // tm, N_HEADS * HEAD_DIM, tm), BF16), jax.ShapeDtypeStruct((n, KV_LORA), F32),
                   jax.ShapeDtypeStruct((n, LANE), F32), jax.ShapeDtypeStruct((n, MEM_HEADS * HEAD_DIM), BF16)],
        compiler_params=_cp(("parallel",)),
        name="mla_proj_prompt",
    )(x, w1, w2, wk, wvt, qn, kvn, cos, sin)


def _mla_proj_sample(x, w_in, q_norm, kv_norm, w_uq, w_uk, cos, sin, cosr, sinr, tm):
    n = x.shape[0]
    w1 = _mla_w1(w_in)
    zq = jnp.zeros((Q_LORA, N_HEADS, LANE - QK_NOPE - QK_ROPE), F32)
    uq_pad = jnp.concatenate([w_uq, zq], -1).reshape(Q_LORA, N_HEADS * LANE)
    w_qr = w_uq[..., QK_NOPE:]
    w2 = jnp.concatenate([uq_pad, w_qr.reshape(Q_LORA, -1),
                          _rot_cols(w_qr, QK_ROPE // 2).reshape(Q_LORA, -1)], 1).astype(BF16)
    wukt = jnp.transpose(w_uk, (1, 2, 0))
    wukt = jnp.concatenate([wukt, jnp.zeros((N_HEADS, LANE - QK_NOPE, KV_LORA), F32)], 1).astype(BF16)
    full = lambda a: pl.BlockSpec(a.shape, lambda i: (0,) * a.ndim)
    rows = lambda w: pl.BlockSpec((tm, w), lambda i: (i, 0))
    qn, kvn = q_norm.reshape(1, -1), kv_norm.reshape(1, -1)
    return pl.pallas_call(
        _mla_proj_sample_kernel,
        grid=(n // tm,),
        in_specs=[rows(D_MODEL), full(w1), full(w2), full(wukt), full(qn), full(kvn), rows(LANE), rows(LANE),
                  rows(N_HEADS * QK_ROPE), rows(N_HEADS * QK_ROPE)],
        out_specs=[rows(N_HEADS * KV_LORA), rows(N_HEADS * QK_ROPE), rows(KV_LORA), rows(LANE),
                   rows(MEM_HEADS * HEAD_DIM)],
        out_shape=[jax.ShapeDtypeStruct((n, N_HEADS * KV_LORA), BF16),
                   jax.ShapeDtypeStruct((n, N_HEADS * QK_ROPE), BF16), jax.ShapeDtypeStruct((n, KV_LORA), F32),
                   jax.ShapeDtypeStruct((n, LANE), F32), jax.ShapeDtypeStruct((n, MEM_HEADS * HEAD_DIM), BF16)],
        compiler_params=_cp(("parallel",)),
        name="mla_proj_sample",
    )(x, w1, w2, wukt, qn, kvn, cos, sin, cosr, sinr)


def _causal_steps(nq, tq, tk):
    qs, ks = [], []
    for qi in range(nq):
        for ki in range(((qi + 1) * tq - 1) // tk + 1):
            qs.append(qi)
            ks.append(ki)
    return jnp.asarray(qs, I32), jnp.asarray(ks, I32)


def _mla_flash_kernel(qt_ref, kt_ref, q_ref, k_ref, vt_ref, o_ref, m_ref, l_ref, acc_ref, *, tq, tk):
    qi = qt_ref[pl.program_id(2)]
    ki = kt_ref[pl.program_id(2)]
    last = ((qi + 1) * tq - 1) // tk

    @pl.when(ki == 0)
    def _():
        m_ref[...] = jnp.full(m_ref.shape, NEG, F32)
        l_ref[...] = jnp.zeros(l_ref.shape, F32)
        acc_ref[...] = jnp.zeros(acc_ref.shape, F32)

    def step(masked):
        ss = [_dot_nt(k_ref[hh], q_ref[hh]) for hh in range(2)]
        if masked:
            key = ki * tk + lax.broadcasted_iota(I32, (tk, tq), 0)
            qry = qi * tq + lax.broadcasted_iota(I32, (tk, tq), 1)
            causal = key <= qry
        for hh in range(2):
            s = jnp.where(causal, ss[hh], NEG) if masked else ss[hh]
            m_new, a, p, l_new = _softmax_step(s, m_ref[hh], l_ref[hh], axis=0)
            vt = jnp.concatenate([vt_ref[c, hh * HEAD_DIM:(hh + 1) * HEAD_DIM, :] for c in range(vt_ref.shape[0])],
                                 axis=1)
            acc_ref[hh] = a * acc_ref[hh] + _dot(vt, p.astype(BF16))
            m_ref[hh] = m_new
            l_ref[hh] = l_new

    first_masked = (qi * tq + 1) // tk

    @pl.when(ki < first_masked)
    def _():
        step(False)

    @pl.when(ki >= first_masked)
    def _():
        step(True)

    @pl.when(ki == last)
    def _():
        ot = jnp.concatenate([acc_ref[0] / l_ref[0], acc_ref[1] / l_ref[1]], axis=0)
        o_ref[...] = ot.T.astype(o_ref.dtype)


def _mla_flash(q, k, vt, batch, tq, tk):
    n = q.shape[1]
    s = n // batch
    nq, nk = s // tq, s // tk
    tv = vt.shape[2]
    qtab, ktab = _causal_steps(nq, tq, tk)
    grid_spec = pltpu.PrefetchScalarGridSpec(
        num_scalar_prefetch=2,
        grid=(batch, N_HEADS // 2, qtab.shape[0]),
        in_specs=[pl.BlockSpec((2, tq, LANE), lambda b, hp, t, qt, kt: (hp, b * nq + qt[t], 0)),
                  pl.BlockSpec((2, tk, LANE), lambda b, hp, t, qt, kt: (hp, b * nk + kt[t], 0)),
                  pl.BlockSpec((tk // tv, 2 * HEAD_DIM, tv), lambda b, hp, t, qt, kt: (b * nk + kt[t], hp, 0))],
        out_specs=pl.BlockSpec((tq, LANE), lambda b, hp, t, qt, kt: (b * nq + qt[t], hp)),
        scratch_shapes=[pltpu.VMEM((2, 1, tq), F32), pltpu.VMEM((2, 1, tq), F32), pltpu.VMEM((2, HEAD_DIM, tq), F32)],
    )
    return pl.pallas_call(
        functools.partial(_mla_flash_kernel, tq=tq, tk=tk),
        grid_spec=grid_spec,
        out_shape=jax.ShapeDtypeStruct((n, N_HEADS * HEAD_DIM), BF16),
        compiler_params=_cp(("parallel", "parallel", "arbitrary")),
        name="mla_flash",
    )(qtab, ktab, q, k, vt)


def _mla_decode_kernel(pt_ref, qlat_ref, qr_ref, cn_ref, kn_ref, *rest, pp, ns, t_new):
    ck_refs, krt_refs = rest[:pp], rest[pp:2 * pp]
    o_ref, m_ref, l_ref, acc_ref = rest[2 * pp:]
    p_id = pl.program_id(1)
    rows = qlat_ref.shape[1]
    per = pp // ns

    @pl.when(p_id == 0)
    def _():
        m_ref[...] = jnp.full(m_ref.shape, NEG, F32)
        l_ref[...] = jnp.zeros(l_ref.shape, F32)
        acc_ref[...] = jnp.zeros(acc_ref.shape, F32)

    qlat = qlat_ref[0]
    qr = qr_ref[0]
    for st in range(ns):
        pages = range(st * per, (st + 1) * per)
        cks = [ck_refs[i][...].astype(BF16) for i in pages]
        s = jnp.concatenate([_dot_nt(qlat, ck) + _dot(qr, krt_refs[i][...].astype(BF16))
                             for ck, i in zip(cks, pages)], axis=1)
        m_new, a, p, l_new = _softmax_step(s, m_ref[st], l_ref[st])
        pb = p.astype(BF16)
        pv = _dot(pb[:, :PAGE], cks[0])
        for i in range(1, per):
            pv = pv + _dot(pb[:, i * PAGE:(i + 1) * PAGE], cks[i])
        acc_ref[st] = a * acc_ref[st] + pv
        m_ref[st] = m_new
        l_ref[st] = l_new

    @pl.when(p_id == pl.num_programs(1) - 1)
    def _():
        m, l, acc = _merge_streams([m_ref[st] for st in range(ns)], [l_ref[st] for st in range(ns)],
                                   [acc_ref[st] for st in range(ns)])
        cn = cn_ref[0].astype(BF16)
        kn = kn_ref[0].astype(BF16)
        npad = cn.shape[0]
        t = lax.broadcasted_iota(I32, (rows, npad), 0) % t_new
        u = lax.broadcasted_iota(I32, (rows, npad), 1)
        s2 = jnp.where(u <= t, _dot_nt(qlat, cn) + _dot_nt(qr, kn), NEG)
        m2, a2, p2, l2 = _softmax_step(s2, m, l)
        o_ref[0] = (a2 * acc + _dot(p2.astype(BF16), cn)) / l2


def _mla_decode(page_table, qlat, qr, ckv_new, kr_new, pool_ckv, pool_krt, j, pp, ns, t_new):
    db, rows, _ = qlat.shape
    npg = page_table.shape[1]
    pt = page_table.reshape(-1)
    npad = ckv_new.shape[1]
    page = lambda i: (lambda b, p, pt_: (j, pt_[b * npg + p * pp + i], 0, 0))
    per_b = lambda r, w: pl.BlockSpec((1, r, w), lambda b, p, pt_: (b, 0, 0))
    grid_spec = pltpu.PrefetchScalarGridSpec(
        num_scalar_prefetch=1,
        grid=(db, npg // pp),
        in_specs=[per_b(rows, KV_LORA), per_b(rows, QK_ROPE), per_b(npad, KV_LORA), per_b(npad, QK_ROPE)]
        + [pl.BlockSpec((None, None, PAGE, KV_LORA), page(i)) for i in range(pp)]
        + [pl.BlockSpec((None, None, QK_ROPE, PAGE), page(i)) for i in range(pp)],
        out_specs=per_b(rows, KV_LORA),
        scratch_shapes=[pltpu.VMEM((ns, rows, 1), F32), pltpu.VMEM((ns, rows, 1), F32),
                        pltpu.VMEM((ns, rows, KV_LORA), F32)],
    )
    return pl.pallas_call(
        functools.partial(_mla_decode_kernel, pp=pp, ns=ns, t_new=t_new),
        grid_spec=grid_spec,
        out_shape=jax.ShapeDtypeStruct((db, rows, KV_LORA), F32),
        compiler_params=_cp(("parallel", "arbitrary")),
        name="mla_decode",
    )(pt, qlat, qr, ckv_new, kr_new, *([pool_ckv] * pp), *([pool_krt] * pp))


def _head_matmul_kernel(x_ref, w_ref, o_ref):
    o_ref[...] = _dot(x_ref[...].astype(BF16), w_ref[...]).astype(o_ref.dtype)


def _head_matmul(x, w):
    h, n, k = x.shape
    m = w.shape[2]
    return pl.pallas_call(
        _head_matmul_kernel,
        grid=(h,),
        in_specs=[pl.BlockSpec((None, n, k), lambda i: (i, 0, 0)), pl.BlockSpec((None, k, m), lambda i: (i, 0, 0))],
        out_specs=pl.BlockSpec((None, n, m), lambda i: (i, 0, 0)),
        out_shape=jax.ShapeDtypeStruct((h, n, m), BF16),
        compiler_params=_cp(("parallel",)),
        name="head_matmul",
    )(x, w)


def _dsa_proj_kernel(x_ref, wq_ref, wk_ref, wv_ref, wvt_ref, wwt_ref, wqi_ref, wm_ref, wki_ref, lng_ref, lnb_ref,
                     sgn_ref, gperm_ref, brot_ref, cos_ref, sin_ref,
                     q_ref, k_ref, kf_ref, vf_ref, vt_ref, wt_ref, qi_ref, kw_ref, kib_ref, qm_ref):
    tm = x_ref.shape[0]
    xb = x_ref[...].astype(BF16)
    cos = cos_ref[...]
    sin = sin_ref[...]
    lane = lax.broadcasted_iota(I32, (tm, LANE), 1)
    low = lane < HEAD_DIM

    def roped(w_ref, nblk):
        y = _dot(xb, w_ref[...])
        return [y[:, b * LANE:(b + 1) * LANE] * cos + y[:, (nblk + b) * LANE:(nblk + b + 1) * LANE] * sin
                for b in range(nblk)]

    qb = roped(wq_ref, N_HEADS // 2)
    for h in range(N_HEADS):
        g, r = h // 3, h % 3
        blk = qb[(g // 2) * 3 + r] * DSA_QSCALE
        q_ref[h] = jnp.where(low if g % 2 == 0 else jnp.logical_not(low), blk, 0.0).astype(BF16)
    kb = roped(wk_ref, KV_HEADS // 2)
    for b in range(KV_HEADS // 2):
        kf_ref[:, b * LANE:(b + 1) * LANE] = kb[b]
        k_ref[b] = kb[b].astype(BF16)
    vf_ref[...] = _dot(xb, wv_ref[...])
    vt_ref[...] = _dot_nt(wvt_ref[...], xb).astype(BF16)
    wt_ref[...] = _dot_nt(wwt_ref[...], xb)
    qib = roped(wqi_ref, IDX_HEADS // 2)
    for h in range(IDX_HEADS):
        qi_ref[h] = jnp.where(low if h % 2 == 0 else jnp.logical_not(low), qib[h // 2], 0.0).astype(BF16)
    qm_ref[...] = (_dot(xb, wm_ref[...]) * HEAD_SCALE).astype(BF16)
    yk = _dot(xb, wki_ref[...])
    c, c_rot, wpad = yk[:, :LANE], yk[:, LANE:2 * LANE], yk[:, 2 * LANE:3 * LANE]
    mu = jnp.sum(jnp.where(low, c, 0.0), axis=1, keepdims=True) * (1.0 / IDX_DIM)
    xc = c - mu
    var = jnp.sum(jnp.where(low, xc * xc, 0.0), axis=1, keepdims=True) * (1.0 / IDX_DIM)
    rstd = lax.rsqrt(var + LN_EPS)
    ki = xc * rstd * lng_ref[...] + lnb_ref[...]
    ki_rot = (c_rot - sgn_ref[...] * mu) * rstd * gperm_ref[...] + brot_ref[...]
    kir = ki * cos + ki_rot * sin
    kib_ref[...] = kir.astype(BF16)
    kw_ref[...] = jnp.where(low, kir, wpad)


def _dsa_proj(x, w_in, ln_g, ln_b, cos, sin, tm):
    n = x.shape[0]
    o = 0
    cols = []
    for wdt in (N_HEADS * HEAD_DIM, KV_HEADS * HEAD_DIM, KV_HEADS * HEAD_DIM, IDX_HEADS * IDX_DIM, IDX_DIM, IDX_HEADS,
                MEM_HEADS * HEAD_DIM):
        cols.append(w_in[:, o:o + wdt])
        o += wdt
    w_q, w_k, w_v, w_qi, w_ki, w_w, w_qm = cols
    perm = jnp.asarray(DSA_PERM)
    w_qp = w_q.reshape(D_MODEL, N_HEADS, HEAD_DIM)[:, perm].reshape(D_MODEL, -1)
    wq = jnp.concatenate([w_qp, _rot_heads(w_qp, N_HEADS, HEAD_DIM, ROT_DIM)], 1).astype(BF16)
    wk = jnp.concatenate([w_k, _rot_heads(w_k, KV_HEADS, HEAD_DIM, ROT_DIM)], 1).astype(BF16)
    wqi = jnp.concatenate([w_qi, _rot_heads(w_qi, IDX_HEADS, IDX_DIM, ROT_DIM)], 1).astype(BF16)
    w_ki_rot = _rot_heads(w_ki, 1, IDX_DIM, ROT_DIM)
    w_wpad = jnp.concatenate([jnp.zeros((D_MODEL, HEAD_DIM), F32), w_w,
                              jnp.zeros((D_MODEL, LANE - HEAD_DIM - IDX_HEADS), F32)], 1)
    wki = jnp.concatenate([w_ki, w_ki, w_ki_rot, w_ki_rot, w_wpad], 1).astype(BF16)
    wv, wm = w_v.astype(BF16), w_qm.astype(BF16)
    wvt = w_v.T.astype(BF16)
    wwt = jnp.concatenate([w_w.T, jnp.zeros((2 * SUBLANE - IDX_HEADS, D_MODEL), F32)], 0).astype(BF16)
    half = ROT_DIM // 2
    zr = jnp.zeros((IDX_DIM - ROT_DIM,), F32)
    dup = lambda a: jnp.concatenate([a, a]).reshape(1, LANE)
    sgn = dup(jnp.concatenate([-jnp.ones((half,), F32), jnp.ones((half,), F32), zr]))
    gperm = dup(jnp.concatenate([ln_g[half:ROT_DIM], ln_g[:half], zr]))
    brot = dup(jnp.concatenate([-ln_b[half:ROT_DIM], ln_b[:half], zr]))
    lng, lnb = dup(ln_g), dup(ln_b)
    nt = cos.shape[0] // tm
    full = lambda a: pl.BlockSpec(a.shape, lambda i: (0,) * a.ndim)
    rows = lambda w: pl.BlockSpec((tm, w), lambda i: (i, 0))
    heads = lambda h: pl.BlockSpec((h, tm, LANE), lambda i: (0, i, 0))
    colsp = lambda r: pl.BlockSpec((r, tm), lambda i: (0, i))
    tab = pl.BlockSpec((tm, LANE), lambda i: (i % nt, 0))
    sds = jax.ShapeDtypeStruct
    return pl.pallas_call(
        _dsa_proj_kernel,
        grid=(n // tm,),
        in_specs=[rows(D_MODEL), full(wq), full(wk), full(wv), full(wvt), full(wwt), full(wqi), full(wm), full(wki),
                  full(lng), full(lnb), full(sgn), full(gperm), full(brot), tab, tab],
        out_specs=[heads(N_HEADS), heads(KV_HEADS // 2), rows(2 * LANE), rows(2 * LANE),
                   pl.BlockSpec((None, 2 * LANE, tm), lambda i: (i, 0, 0)),
                   colsp(2 * SUBLANE), heads(IDX_HEADS), rows(LANE), rows(LANE), rows(2 * LANE)],
        out_shape=[sds((N_HEADS, n, LANE), BF16), sds((KV_HEADS // 2, n, LANE), BF16), sds((n, 2 * LANE), F32),
                   sds((n, 2 * LANE), F32), sds((n // tm, 2 * LANE, tm), BF16), sds((2 * SUBLANE, n), F32),
                   sds((IDX_HEADS, n, LANE), BF16), sds((n, LANE), F32), sds((n, LANE), BF16),
                   sds((n, 2 * LANE), BF16)],
        compiler_params=_cp(("parallel",), 56),
        name="dsa_proj",
    )(x, wq, wk, wv, wvt, wwt, wqi, wm, wki, lng, lnb, sgn, gperm, brot, cos, sin)


def _indexer_prompt_kernel(qt_ref, kt_ref, qi_ref, ki_ref, wt_ref, sc_ref, *, tq, tk):
    qi = qt_ref[pl.program_id(1)]
    ki = kt_ref[pl.program_id(1)]
    kb = ki_ref[...]
    w = wt_ref[...]
    acc = None
    for h in range(IDX_HEADS):
        d = jnp.maximum(_dot_nt(kb, qi_ref[h]), 0.0) * w[h:h + 1]
        acc = d if acc is None else acc + d
    key = ki * tk + lax.broadcasted_iota(I32, (tk, tq), 0)
    qry = qi * tq + lax.broadcasted_iota(I32, (tk, tq), 1)
    res = jnp.where(key <= qry, acc, NEG)
    for c in range(tq // LANE):
        sc_ref[c] = res[:, c * LANE:(c + 1) * LANE]


def _indexer_prompt(qidx, kib, wt, batch, tq, tk):
    n = kib.shape[0]
    s = n // batch
    nq, nk = s // tq, s // tk
    qtab, ktab = _causal_steps(nq, tq, tk)
    grid_spec = pltpu.PrefetchScalarGridSpec(
        num_scalar_prefetch=2,
        grid=(batch, qtab.shape[0]),
        in_specs=[pl.BlockSpec((IDX_HEADS, tq, LANE), lambda b, t, qt, kt: (0, b * nq + qt[t], 0)),
                  pl.BlockSpec((tk, LANE), lambda b, t, qt, kt: (b * nk + kt[t], 0)),
                  pl.BlockSpec((2 * SUBLANE, tq), lambda b, t, qt, kt: (0, b * nq + qt[t]))],
        out_specs=pl.BlockSpec((tq // LANE, tk, LANE), lambda b, t, qt, kt: (b * nq + qt[t], kt[t], 0)),
    )
    return pl.pallas_call(
        functools.partial(_indexer_prompt_kernel, tq=tq, tk=tk),
        grid_spec=grid_spec,
        out_shape=jax.ShapeDtypeStruct((n // LANE, s, LANE), F32),
        compiler_params=_cp(("parallel", "arbitrary")),
        name="indexer_prompt",
    )(qtab, ktab, qidx, kib, wt)


def _indexer_sample_kernel(pt_ref, q_ref, w_ref, kn_ref, *rest, pp, t_new):
    kt_refs = rest[:pp]
    sc_ref, scn_ref = rest[pp:]
    q = q_ref[0]
    w = w_ref[0]

    def weighted(d):
        acc = None
        for h in range(IDX_HEADS):
            t = d[h * t_new:(h + 1) * t_new] * w[:, h:h + 1]
            acc = t if acc is None else acc + t
        return acc

    for i in range(pp):
        sc_ref[0, :, i * PAGE:(i + 1) * PAGE] = weighted(jnp.maximum(_dot(q, kt_refs[i][...].astype(BF16)), 0.0))

    @pl.when(pl.program_id(1) == pl.num_programs(1) - 1)
    def _():
        sn = weighted(jnp.maximum(_dot_nt(q, kn_ref[0].astype(BF16)), 0.0))
        npad = sn.shape[1]
        t = lax.broadcasted_iota(I32, (t_new, npad), 0)
        u = lax.broadcasted_iota(I32, (t_new, npad), 1)
        scn_ref[0] = jnp.full((t_new, LANE), NEG, F32)
        scn_ref[0, :, :npad] = jnp.where(u <= t, sn, NEG)


def _indexer_sample(page_table, q, w, kidx_new, pool_kidxt, j, pp, t_new):
    db = q.shape[0]
    npg = page_table.shape[1]
    pt = page_table.reshape(-1)
    npad = kidx_new.shape[1]
    per_b = lambda r, c: pl.BlockSpec((1, r, c), lambda b, p, pt_: (b, 0, 0))
    pool = lambda i: pl.BlockSpec((None, None, IDX_DIM, PAGE), lambda b, p, pt_: (j, pt_[b * npg + p * pp + i], 0, 0))
    grid_spec = pltpu.PrefetchScalarGridSpec(
        num_scalar_prefetch=1,
        grid=(db, npg // pp),
        in_specs=[per_b(IDX_HEADS * t_new, IDX_DIM), per_b(t_new, IDX_HEADS), per_b(npad, IDX_DIM)]
        + [pool(i) for i in range(pp)],
        out_specs=[pl.BlockSpec((1, t_new, pp * PAGE), lambda b, p, pt_: (b, 0, p)), per_b(t_new, LANE)],
    )
    return pl.pallas_call(
        functools.partial(_indexer_sample_kernel, pp=pp, t_new=t_new),
        grid_spec=grid_spec,
        out_shape=[jax.ShapeDtypeStruct((db, t_new, npg * PAGE), F32), jax.ShapeDtypeStruct((db, t_new, LANE), F32)],
        compiler_params=_cp(("parallel", "arbitrary")),
        name="indexer_sample",
    )(pt, q, w, kidx_new, *([pool_kidxt] * pp))


def _select_rows_kernel(sc_ref, thr_ref, cut_ref, key_ref, *, topk, cw, tq, idx_bits):
    nch = sc_ref.shape[1] // cw

    def make_keys(c, carry):
        off = pl.multiple_of(c * cw, cw)
        key_ref[:, pl.ds(off, cw)] = _score_keys(sc_ref[:, pl.ds(off, cw)])
        return carry

    lax.fori_loop(0, nch, make_keys, 0)
    lane = lax.broadcasted_iota(I32, (tq, LANE), 1)

    def count(pred):
        def body(c, acc):
            off = pl.multiple_of(c * cw, cw)
            for u in range(cw // LANE):
                kk = key_ref[:, pl.ds(off + u * LANE, LANE)]
                acc = acc + jnp.where(pred(kk, off + u * LANE + lane), 1.0, 0.0)
            return acc
        acc = lax.fori_loop(0, nch, body, jnp.zeros((tq, LANE), F32))
        return jnp.sum(acc, axis=1, keepdims=True)

    thrk = _kth_key(count, (tq, 1), topk)
    cut = _tie_cut(count, thrk, (tq, 1), topk, idx_bits)
    thr_ref[...] = _key_to_score(thrk, (tq, LANE))
    cut_ref[...] = jnp.broadcast_to(cut, (tq, LANE))


def _kth_key(count, shape, topk):
    def value_bit(it, ans):
        cand = ans | jnp.left_shift(jnp.int32(1), 31 - it)
        return jnp.where(count(lambda kk, idx: kk >= (cand ^ MIN_I32)) >= topk, cand, ans)

    return lax.fori_loop(0, 32, value_bit, jnp.zeros(shape, I32)) ^ MIN_I32


def _tie_cut(count, thrk, shape, topk, idx_bits):
    n_gt = count(lambda kk, idx: kk > thrk)
    n_ge = count(lambda kk, idx: kk >= thrk)
    need = topk - n_gt

    def tie_bit(it, m):
        cand = m | jnp.left_shift(jnp.int32(1), idx_bits - 1 - it)
        cnt = count(lambda kk, idx: jnp.where(kk == thrk, idx, cand) < cand)
        return jnp.where(cnt < need, cand, m)

    return lax.cond(jnp.max(jnp.abs(n_ge - topk)) > 0.0,
                    lambda: lax.fori_loop(0, idx_bits, tie_bit, jnp.zeros(shape, I32)),
                    lambda: jnp.full(shape, 2 ** 30, I32))


def _select_rows(sc, topk, cw, tq):
    n, length = sc.shape
    idx_bits = max(1, (length - 1).bit_length())
    return pl.pallas_call(
        functools.partial(_select_rows_kernel, topk=topk, cw=cw, tq=tq, idx_bits=idx_bits),
        grid=(n // tq,),
        in_specs=[pl.BlockSpec((tq, length), lambda i: (i, 0))],
        out_specs=[pl.BlockSpec((tq, LANE), lambda i: (i, 0)), pl.BlockSpec((tq, LANE), lambda i: (i, 0))],
        out_shape=[jax.ShapeDtypeStruct((n, LANE), F32), jax.ShapeDtypeStruct((n, LANE), I32)],
        scratch_shapes=[pltpu.VMEM((tq, length), I32)],
        compiler_params=_cp(("parallel",)),
        name="topk_select_rows",
    )(sc)


def _select_cols_kernel(sc_ref, thr_ref, cut_ref, key_ref, hi_ref, lo_ref, *, topk, cw, idx_bits, nqt, grp):
    qt_last = (pl.program_id(0) % (nqt // grp)) * grp + grp - 1
    nch = (jnp.maximum((qt_last + 1) * LANE, topk) + cw - 1) // cw
    gs = range(grp)
    pk = 2 * SUBLANE
    half = 2 ** 15

    def make_keys(c, carry):
        rows = pl.ds(pl.multiple_of(c * cw, cw), cw)
        for g in gs:
            k = _score_keys(sc_ref[g, rows, :])
            key_ref[g, rows, :] = k
            hi_ref[g, rows, :] = (k >> 16).astype(I16)
            lo_ref[g, rows, :] = ((k & 0xFFFF) - half).astype(I16)
        return carry

    lax.fori_loop(0, nch, make_keys, 0)
    sub = lax.broadcasted_iota(I32, (SUBLANE, LANE), 0)

    def sweep(ref, nrow, dtype, hit):
        one, zero = jnp.ones((), dtype), jnp.zeros((), dtype)

        def body(c, accs):
            off = pl.multiple_of(c * cw, cw)
            accs = [list(a) for a in accs]
            for g in gs:
                blk = ref[g, pl.ds(off, cw), :]
                for u in range(cw // nrow):
                    accs[g][u % N_ACC] = accs[g][u % N_ACC] + jnp.where(
                        hit(g, blk[u * nrow:(u + 1) * nrow], off + u * nrow), one, zero)
            return tuple(tuple(a) for a in accs)

        init = tuple(tuple(jnp.zeros((nrow, LANE), dtype) for _ in range(N_ACC)) for _ in gs)
        accs = lax.fori_loop(0, nch, body, init)
        return [jnp.sum(functools.reduce(lambda a, b: a + b, [x.astype(F32) for x in a]), axis=0, keepdims=True)
                for a in accs]

    def count(preds):
        return sweep(key_ref, SUBLANE, F32, lambda g, kk, off: preds[g](kk, off + sub))

    def count16(ref, preds):
        return sweep(ref, pk, I16, lambda g, kk, off: preds[g](kk))

    def packed(v):
        return jnp.broadcast_to(v - half, (pk, LANE)).astype(I16)

    def search16(ref, want):
        def bit(it, ans):
            cands = [a | jnp.left_shift(jnp.int32(1), 15 - it) for a in ans]
            cbs = [packed(c) for c in cands]
            cnt = count16(ref, [lambda kk, cb=cb: kk >= cb for cb in cbs])
            return tuple(jnp.where(cnt[g] >= want[g], cands[g], ans[g]) for g in gs)
        return lax.fori_loop(0, 16, bit, tuple(jnp.zeros((1, LANE), I32) for _ in gs))

    hi = search16(hi_ref, [topk] * grp)
    hbs = [packed(h) for h in hi]
    above = count16(hi_ref, [lambda kk, hb=hb: kk > hb for hb in hbs])
    want_lo = [topk - a for a in above]

    def mask_lo(c, carry):
        rows = pl.ds(pl.multiple_of(c * cw, cw), cw)
        for g in gs:
            hi_blk, lo_blk = hi_ref[g, rows, :], lo_ref[g, rows, :]
            lo_ref[g, rows, :] = jnp.concatenate(
                [jnp.where(hi_blk[u * pk:(u + 1) * pk] == hbs[g], lo_blk[u * pk:(u + 1) * pk], jnp.int16(-half))
                 for u in range(cw // pk)], axis=0)
        return carry

    lax.fori_loop(0, nch, mask_lo, 0)
    lo = search16(lo_ref, want_lo)
    thrk = [(jnp.left_shift(hi[g], 16) | lo[g]) ^ MIN_I32 for g in gs]

    n_gt = count([lambda kk, idx, t=t: kk > t for t in thrk])
    n_ge = count([lambda kk, idx, t=t: kk >= t for t in thrk])
    need = [topk - x for x in n_gt]

    def tie_bit(it, ms):
        cands = [m | jnp.left_shift(jnp.int32(1), idx_bits - 1 - it) for m in ms]
        cnt = count([lambda kk, idx, t=thrk[g], c=cands[g]: jnp.where(kk == t, idx, c) < c for g in gs])
        return tuple(jnp.where(cnt[g] < need[g], cands[g], ms[g]) for g in gs)

    any_tie = functools.reduce(jnp.maximum, [jnp.max(jnp.abs(x - topk)) for x in n_ge]) > 0.0
    cut = lax.cond(any_tie,
                   lambda: lax.fori_loop(0, idx_bits, tie_bit, tuple(jnp.zeros((1, LANE), I32) for _ in gs)),
                   lambda: tuple(jnp.full((1, LANE), 2 ** 30, I32) for _ in gs))
    for g in gs:
        thr_ref[:, g * LANE:(g + 1) * LANE] = _key_to_score(thrk[g], (SUBLANE, LANE))
        cut_ref[:, g * LANE:(g + 1) * LANE] = jnp.broadcast_to(cut[g], (SUBLANE, LANE))


def _select_cols(sct, topk, cw):
    nt, s, _ = sct.shape
    nqt = s // LANE
    grp = 2 if nqt % 2 == 0 else 1
    idx_bits = max(1, (s - 1).bit_length())
    out = pl.BlockSpec((SUBLANE, grp * LANE), lambda i: (0, i))
    return pl.pallas_call(
        functools.partial(_select_cols_kernel, topk=topk, cw=cw, idx_bits=idx_bits, nqt=nqt, grp=grp),
        grid=(nt // grp,),
        in_specs=[pl.BlockSpec((grp, s, LANE), lambda i: (i, 0, 0))],
        out_specs=[out, out],
        out_shape=[jax.ShapeDtypeStruct((SUBLANE, nt * LANE), F32), jax.ShapeDtypeStruct((SUBLANE, nt * LANE), I32)],
        scratch_shapes=[pltpu.VMEM((grp, s, LANE), I32), pltpu.VMEM((grp, s, LANE), I16),
                        pltpu.VMEM((grp, s, LANE), I16)],
        compiler_params=_cp(("parallel",)),
        name="topk_select_cols",
    )(sct)


def _dsa_attn_kernel(qt_ref, kt_ref, q_ref, k_ref, vt_ref, sc_ref, thr_ref, cut_ref, o_ref, m_ref, l_ref, acc_ref,
                     *, tq, tk):
    qi = qt_ref[pl.program_id(1)]
    ki = kt_ref[pl.program_id(1)]
    last = ((qi + 1) * tq - 1) // tk

    @pl.when(ki == 0)
    def _():
        m_ref[...] = jnp.full(m_ref.shape, NEG, F32)
        l_ref[...] = jnp.zeros(l_ref.shape, F32)
        acc_ref[...] = jnp.zeros(acc_ref.shape, F32)

    key = ki * tk + lax.broadcasted_iota(I32, (tk, tq), 0)
    qry = qi * tq + lax.broadcasted_iota(I32, (tk, tq), 1)
    sc = jnp.concatenate([sc_ref[c] for c in range(tq // LANE)], axis=1)
    thr = thr_ref[0:1, :]
    cut = cut_ref[0:1, :]
    sel = jnp.where(sc > thr, 1.0, jnp.where(sc == thr, jnp.where(key <= cut, 1.0, 0.0), 0.0))
    keep = jnp.where(key <= qry, sel, 0.0)
    keep3 = jnp.concatenate([keep, keep, keep], axis=1)
    ss = [_dot_nt(k_ref[g // 2], q_ref[3 * g:3 * g + 3].reshape(3 * tq, LANE)) for g in range(KV_HEADS)]
    for g in range(KV_HEADS):
        s = jnp.where(keep3 > 0.0, ss[g], NEG)
        m_new, a, p, l_new = _softmax_step(s, m_ref[g], l_ref[g], axis=0)
        vt = jnp.concatenate([vt_ref[c, g * HEAD_DIM:(g + 1) * HEAD_DIM, :] for c in range(vt_ref.shape[0])], axis=1)
        acc_ref[g] = a * acc_ref[g] + _dot(vt, p.astype(BF16))
        m_ref[g] = m_new
        l_ref[g] = l_new

    @pl.when(ki == last)
    def _():
        for gp in range(KV_HEADS // 2):
            oe = acc_ref[2 * gp] / l_ref[2 * gp]
            oo = acc_ref[2 * gp + 1] / l_ref[2 * gp + 1]
            for r in range(3):
                blk = gp * 3 + r
                ot = jnp.concatenate([oe[:, r * tq:(r + 1) * tq], oo[:, r * tq:(r + 1) * tq]], axis=0)
                o_ref[:, blk * LANE:(blk + 1) * LANE] = ot.T.astype(o_ref.dtype)


def _dsa_attn(q, k, vt, sct, thr, cut, batch, tq, tk):
    n = q.shape[1]
    s = n // batch
    nq, nk = s // tq, s // tk
    qtab, ktab = _causal_steps(nq, tq, tk)
    qcol = pl.BlockSpec((SUBLANE, tq), lambda b, t, qt, kt: (0, b * nq + qt[t]))
    grid_spec = pltpu.PrefetchScalarGridSpec(
        num_scalar_prefetch=2,
        grid=(batch, qtab.shape[0]),
        in_specs=[pl.BlockSpec((N_HEADS, tq, LANE), lambda b, t, qt, kt: (0, b * nq + qt[t], 0)),
                  pl.BlockSpec((KV_HEADS // 2, tk, LANE), lambda b, t, qt, kt: (0, b * nk + kt[t], 0)),
                  pl.BlockSpec((tk // vt.shape[2], KV_HEADS * HEAD_DIM, vt.shape[2]),
                               lambda b, t, qt, kt: (b * nk + kt[t], 0, 0)),
                  pl.BlockSpec((tq // LANE, tk, LANE), lambda b, t, qt, kt: (b * nq + qt[t], kt[t], 0)),
                  qcol, qcol],
        out_specs=pl.BlockSpec((tq, N_HEADS * HEAD_DIM), lambda b, t, qt, kt: (b * nq + qt[t], 0)),
        scratch_shapes=[pltpu.VMEM((KV_HEADS, 1, 3 * tq), F32), pltpu.VMEM((KV_HEADS, 1, 3 * tq), F32),
                        pltpu.VMEM((KV_HEADS, HEAD_DIM, 3 * tq), F32)],
    )
    return pl.pallas_call(
        functools.partial(_dsa_attn_kernel, tq=tq, tk=tk),
        grid_spec=grid_spec,
        out_shape=jax.ShapeDtypeStruct((n, N_HEADS * HEAD_DIM), BF16),
        compiler_params=_cp(("parallel", "arbitrary"), 56),
        name="dsa_attn",
    )(qtab, ktab, q, k, vt, sct, thr, cut)


def _dsa_decode_kernel(pt_ref, q_ref, sc_ref, scn_ref, thr_ref, cut_ref, knt_ref, vn_ref, *rest, pp, ns, t_new, past):
    kt_refs, vt_refs = rest[:pp], rest[pp:2 * pp]
    o_ref, m_ref, l_ref, acc_ref = rest[2 * pp:]
    p_id = pl.program_id(1)
    rows = q_ref.shape[1]
    reps = rows // t_new
    per = pp // ns
    width = per * PAGE

    @pl.when(p_id == 0)
    def _():
        m_ref[...] = jnp.full(m_ref.shape, NEG, F32)
        l_ref[...] = jnp.zeros(l_ref.shape, F32)
        acc_ref[...] = jnp.zeros(acc_ref.shape, F32)

    thr = thr_ref[0][:, :1]
    cut = cut_ref[0][:, :1]
    q = q_ref[0]

    def keep_rows(sc, col, extra):
        sel = jnp.where(sc > thr, 1.0, jnp.where(sc == thr, jnp.where(col <= cut, 1.0, 0.0), 0.0))
        if extra is not None:
            sel = jnp.where(extra, sel, 0.0)
        return jnp.concatenate([sel] * reps, axis=0)

    for st in range(ns):
        pages = range(st * per, (st + 1) * per)
        col = p_id * (pp * PAGE) + st * width + lax.broadcasted_iota(I32, (t_new, width), 1)
        keep = keep_rows(sc_ref[0, :, st * width:(st + 1) * width], col, None)
        s = jnp.concatenate([_dot(q, kt_refs[i][...].astype(BF16)) for i in pages], axis=1)
        s = jnp.where(keep > 0.0, s, NEG)
        m_new, a, p, l_new = _softmax_step(s, m_ref[st], l_ref[st])
        pb = p.astype(BF16)
        pv = None
        for n_, i in enumerate(pages):
            t = _dot_nt(pb[:, n_ * PAGE:(n_ + 1) * PAGE], vt_refs[i][...].astype(BF16))
            pv = t if pv is None else pv + t
        acc_ref[st] = a * acc_ref[st] + pv
        m_ref[st] = m_new
        l_ref[st] = l_new

    @pl.when(p_id == pl.num_programs(1) - 1)
    def _():
        m, l, acc = _merge_streams([m_ref[st] for st in range(ns)], [l_ref[st] for st in range(ns)],
                                   [acc_ref[st] for st in range(ns)])
        npad = vn_ref.shape[1]
        t = lax.broadcasted_iota(I32, (t_new, npad), 0)
        u = lax.broadcasted_iota(I32, (t_new, npad), 1)
        keep_n = keep_rows(scn_ref[0][:, :npad], past + u, u <= t)
        s2 = jnp.where(keep_n > 0.0, _dot(q, knt_ref[0].astype(BF16)), NEG)
        m2, a2, p2, l2 = _softmax_step(s2, m, l)
        o_ref[0] = (a2 * acc + _dot(p2.astype(BF16), vn_ref[0].astype(BF16))) / l2


def _dsa_decode(page_table, q, sc, scn, thr, cut, knt, v_new, pool_kt, pool_vt, j, pp, ns, t_new):
    db, rows, wd = q.shape
    npg = page_table.shape[1]
    pt = page_table.reshape(-1)
    npad = v_new.shape[1]
    per_b = lambda r, c: pl.BlockSpec((1, r, c), lambda b, p, pt_: (b, 0, 0))
    pool = lambda i: pl.BlockSpec((None, None, wd, PAGE), lambda b, p, pt_: (j, pt_[b * npg + p * pp + i], 0, 0))
    grid_spec = pltpu.PrefetchScalarGridSpec(
        num_scalar_prefetch=1,
        grid=(db, npg // pp),
        in_specs=[per_b(rows, wd), pl.BlockSpec((1, t_new, pp * PAGE), lambda b, p, pt_: (b, 0, p)),
                  per_b(t_new, LANE), per_b(t_new, LANE), per_b(t_new, LANE), per_b(wd, npad), per_b(npad, wd)]
        + [pool(i) for i in range(pp)] + [pool(i) for i in range(pp)],
        out_specs=per_b(rows, wd),
        scratch_shapes=[pltpu.VMEM((ns, rows, 1), F32), pltpu.VMEM((ns, rows, 1), F32),
                        pltpu.VMEM((ns, rows, wd), F32)],
    )
    return pl.pallas_call(
        functools.partial(_dsa_decode_kernel, pp=pp, ns=ns, t_new=t_new, past=npg * PAGE),
        grid_spec=grid_spec,
        out_shape=jax.ShapeDtypeStruct((db, rows, wd), F32),
        compiler_params=_cp(("parallel", "arbitrary")),
        name="dsa_decode",
    )(pt, q, sc, scn, thr, cut, knt, v_new, *([pool_kt] * pp), *([pool_vt] * pp))


def _mem_kv_kernel(m_ref, w_ref, kf_ref, vf_ref, kb_ref, vb_ref):
    y = _dot(m_ref[...].astype(BF16), w_ref[...])
    w = MEM_HEADS * HEAD_DIM
    kf_ref[...] = y[:, :w]
    vf_ref[...] = y[:, w:]
    kb_ref[...] = y[:, :w].astype(BF16)
    vb_ref[...] = y[:, w:].astype(BF16)


def _mem_kv(mem2d, w):
    n = mem2d.shape[0]
    wd = MEM_HEADS * HEAD_DIM
    wb = w.astype(BF16)
    spec = pl.BlockSpec((n, wd), lambda i: (0, 0))
    return pl.pallas_call(
        _mem_kv_kernel,
        grid=(1,),
        in_specs=[pl.BlockSpec(mem2d.shape, lambda i: (0, 0)), pl.BlockSpec(wb.shape, lambda i: (0, 0))],
        out_specs=[spec] * 4,
        out_shape=[jax.ShapeDtypeStruct((n, wd), F32)] * 2 + [jax.ShapeDtypeStruct((n, wd), BF16)] * 2,
        compiler_params=_cp(("arbitrary",)),
        name="mem_kv",
    )(mem2d, wb)


def _mem_attend_kernel(q_ref, mk_ref, mv_ref, o_ref, *, bb, kv_t):
    tt = q_ref.shape[1]
    lane = lax.broadcasted_iota(I32, (tt, LANE), 1)
    low = lane < HEAD_DIM
    for b in range(bb):
        for blk in range(MEM_HEADS // 2):
            sl = slice(blk * LANE, (blk + 1) * LANE)
            qb = q_ref[b, :, sl]
            mkb = (mk_ref[b, sl, :] if kv_t else mk_ref[b, :, sl]).astype(BF16)
            mvb = (mv_ref[b, sl, :] if kv_t else mv_ref[b, :, sl]).astype(BF16)
            res = []
            for half in range(2):
                qh = jnp.where(low if half == 0 else jnp.logical_not(low), qb, jnp.zeros_like(qb))
                s = _dot(qh, mkb) if kv_t else _dot_nt(qh, mkb)
                p = jnp.exp(s - jnp.max(s, axis=1, keepdims=True))
                p = (p / jnp.sum(p, axis=1, keepdims=True)).astype(BF16)
                res.append(_dot_nt(p, mvb) if kv_t else _dot(p, mvb))
            o_ref[b, :, sl] = jnp.where(low, res[0], res[1]).astype(o_ref.dtype)


def _mem_attend(q, mk, mv, bb, tt, kv_t, layer=None):
    bm, t, wd = q.shape
    if layer is None:
        kv = pl.BlockSpec((bb,) + mk.shape[1:], lambda b, i: (b, 0, 0))
    else:
        kv = pl.BlockSpec((None, bb) + mk.shape[2:], lambda b, i: (layer, b, 0, 0))
    return pl.pallas_call(
        functools.partial(_mem_attend_kernel, bb=bb, kv_t=kv_t),
        grid=(bm // bb, t // tt),
        in_specs=[pl.BlockSpec((bb, tt, wd), lambda b, i: (b, i, 0)), kv, kv],
        out_specs=pl.BlockSpec((bb, tt, wd), lambda b, i: (b, i, 0)),
        out_shape=jax.ShapeDtypeStruct((bm, t, wd), BF16),
        compiler_params=_cp(("parallel", "parallel")),
        name="mem_attend",
    )(q, mk, mv)


def _split_bf16(x):
    hi = x.astype(BF16)
    return hi, (x - hi.astype(F32)).astype(BF16)


def _outproj_router_kernel(x_ref, mix_ref, mem_ref, wo1_ref, wo2_ref, g_ref, b_ref, wrh_ref, wrl_ref, rb_ref,
                           x1_ref, gate_ref):
    y = ALPHA * x_ref[...] + _dot(mix_ref[...].astype(BF16), wo1_ref[...]) + _dot(mem_ref[...], wo2_ref[...])
    x1 = _ln(y, g_ref[...], b_ref[...])
    x1_ref[...] = x1
    xh, xl = _split_bf16(x1)
    logit = _dot_nt(wrh_ref[...], xh) + _dot_nt(wrh_ref[...], xl) + _dot_nt(wrl_ref[...], xh)
    s = 1.0 / (1.0 + jnp.exp(-logit))
    sb = s + rb_ref[...]
    epg = N_EXPERTS // N_GROUPS
    rows = [sb[e:e + 1] for e in range(N_EXPERTS)]
    gscore = []
    for g in range(N_GROUPS):
        v = rows[g * epg:(g + 1) * epg]
        best = None
        for a in range(epg):
            for b in range(a + 1, epg):
                pr = v[a] + v[b]
                best = pr if best is None else jnp.maximum(best, pr)
        gscore.append(best)
    gmax = functools.reduce(jnp.maximum, gscore)
    taken = jnp.zeros_like(gmax)
    gsel = []
    for g in range(N_GROUPS):
        pick = jnp.where(gscore[g] == gmax, 1.0, 0.0) * (1.0 - taken)
        taken = taken + pick
        gsel.append(pick)
    val = [jnp.where(gsel[e // epg] > 0.0, rows[e], NEG) for e in range(N_EXPERTS)]
    chosen = [jnp.zeros_like(gmax) for _ in range(N_EXPERTS)]
    for _ in range(2):
        vmax = functools.reduce(jnp.maximum, val)
        taken = jnp.zeros_like(gmax)
        for e in range(N_EXPERTS):
            pick = jnp.where(val[e] == vmax, 1.0, 0.0) * (1.0 - taken)
            taken = taken + pick
            chosen[e] = chosen[e] + pick
            val[e] = jnp.where(pick > 0.0, 2.0 * NEG, val[e])
    wsel = [jnp.where(chosen[e] > 0.0, s[e:e + 1], 0.0) for e in range(N_EXPERTS)]
    wsum = functools.reduce(lambda a, b: a + b, wsel)
    gate_ref[...] = jnp.concatenate([w / wsum for w in wsel], axis=0)


def _outproj_router(x, mix, mem, wo1, wo2, g, b, wrh, wrl, rb, tm):
    n = x.shape[0]
    full = lambda a: pl.BlockSpec(a.shape, lambda i: (0,) * a.ndim)
    rows = lambda w: pl.BlockSpec((tm, w), lambda i: (i, 0))
    return pl.pallas_call(
        _outproj_router_kernel,
        grid=(n // tm,),
        in_specs=[rows(D_MODEL), rows(mix.shape[1]), rows(mem.shape[1]), full(wo1), full(wo2), full(g), full(b),
                  full(wrh), full(wrl), full(rb)],
        out_specs=[rows(D_MODEL), pl.BlockSpec((N_EXPERTS, tm), lambda i: (0, i))],
        out_shape=[jax.ShapeDtypeStruct((n, D_MODEL), F32), jax.ShapeDtypeStruct((N_EXPERTS, n), F32)],
        compiler_params=_cp(("parallel",)),
        name="outproj_router",
    )(x, mix, mem, wo1, wo2, g, b, wrh, wrl, rb)


def _moe_kernel(x_ref, gate_ref, wgu_ref, wd_ref, g_ref, b_ref, o_ref, acc_ref, xb_ref):
    e = pl.program_id(1)

    @pl.when(e == 0)
    def _():
        acc_ref[...] = jnp.zeros(acc_ref.shape, F32)
        xb_ref[...] = x_ref[...].astype(BF16)

    gate = gate_ref[...]
    lane = lax.broadcasted_iota(I32, gate.shape, 1)
    gcol = jnp.sum(jnp.where(lane == e, gate, 0.0), axis=1, keepdims=True)
    hgu = _dot(xb_ref[...], wgu_ref[...])
    hg, hu = hgu[:, :D_EXPERT], hgu[:, D_EXPERT:]
    h = hg * (1.0 / (1.0 + jnp.exp(-hg))) * hu * gcol
    acc_ref[...] += _dot(h.astype(BF16), wd_ref[...])

    @pl.when(e == pl.num_programs(1) - 1)
    def _():
        o_ref[...] = _ln(ALPHA * x_ref[...] + acc_ref[...], g_ref[...], b_ref[...])


def _moe(x, gate, wgu, wd, g, b, tm):
    n = x.shape[0]
    full = lambda a: pl.BlockSpec(a.shape, lambda i, e: (0,) * a.ndim)
    return pl.pallas_call(
        _moe_kernel,
        grid=(n // tm, N_EXPERTS),
        in_specs=[pl.BlockSpec((tm, D_MODEL), lambda i, e: (i, 0)), pl.BlockSpec((tm, N_EXPERTS), lambda i, e: (i, 0)),
                  pl.BlockSpec((None, D_MODEL, 2 * D_EXPERT), lambda i, e: (e, 0, 0)),
                  pl.BlockSpec((None, D_EXPERT, D_MODEL), lambda i, e: (e, 0, 0)), full(g), full(b)],
        out_specs=pl.BlockSpec((tm, D_MODEL), lambda i, e: (i, 0)),
        out_shape=jax.ShapeDtypeStruct((n, D_MODEL), F32),
        scratch_shapes=[pltpu.VMEM((tm, D_MODEL), F32), pltpu.VMEM((tm, D_MODEL), BF16)],
        compiler_params=_cp(("parallel", "arbitrary")),
        name="moe",
    )(x, gate, wgu, wd, g, b)


def _tile(n, pref):
    t = min(n, pref)
    while n % t:
        t //= 2
    return t


def kernel(x_prompt, x_sample, cache_mla_ckv, cache_mla_krope, cache_dsa_k, cache_dsa_v, cache_dsa_kidx, cache_mem_k, cache_mem_v, page_table, mem_prompt, w_in_mla, mla_q_norm, mla_kv_norm, w_uq, w_uk, w_uv, w_in_dsa, idx_k_norm_g, idx_k_norm_b, w_mem_kv, w_out, ln1_g, ln1_b, ln2_g, ln2_b, w_router, router_bias, w_gate, w_up, w_down):
    B, S, _ = x_prompt.shape
    DB, T, _ = x_sample.shape
    n_mem = mem_prompt.shape[1]
    npg = page_table.shape[1]
    past = npg * PAGE
    n_pool = cache_mla_ckv.shape[1]
    NP, NS = B * S, DB * T
    TPAD = 16
    pp_dsa = pp_idx = 32 if npg % 32 == 0 else npg
    pp_mla = 16 if npg % 16 == 0 else npg
    ns_dsa = 4 if pp_dsa % 4 == 0 else 1
    ns_mla = 2 if pp_mla % 2 == 0 else 1
    tm_p, tm_s = _tile(NP, 512), _tile(NS, 512)
    t_att = _tile(S, 512)
    KVW = KV_HEADS * HEAD_DIM

    pos_p = jnp.arange(S, dtype=jnp.int32)
    pos_s = jnp.tile(past + jnp.arange(T, dtype=jnp.int32), DB)
    cos_mp, sin_mp = _rope_tables(pos_p, QK_ROPE, LANE, QK_NOPE, LANE)
    cos_ms, sin_ms = _rope_tables(pos_s, QK_ROPE, LANE, QK_NOPE, LANE)
    cos_mr, sin_mr = _rope_tables(pos_s, QK_ROPE, QK_ROPE, 0, N_HEADS * QK_ROPE)
    cos_dp, sin_dp = _rope_tables(pos_p, ROT_DIM, HEAD_DIM, 0, LANE)
    cos_ds, sin_ds = _rope_tables(pos_s, ROT_DIM, HEAD_DIM, 0, LANE)

    wrt = w_router.T
    wrh = wrt.astype(BF16)
    wrl = (wrt - wrh.astype(F32)).astype(BF16)
    rb = router_bias.reshape(N_EXPERTS, 1).astype(F32)
    perm = jnp.asarray(DSA_PERM)
    pool_kt = jnp.transpose(cache_dsa_k, (0, 1, 3, 4, 2)).reshape(-1, n_pool, KVW, PAGE)
    pool_vt = jnp.transpose(cache_dsa_v, (0, 1, 3, 4, 2)).reshape(-1, n_pool, KVW, PAGE)
    pool_kidxt = jnp.transpose(cache_dsa_kidx, (0, 1, 3, 2))
    pool_krt = jnp.transpose(cache_mla_krope, (0, 1, 3, 2))
    mem_kt = jnp.transpose(cache_mem_k, (0, 1, 3, 4, 2)).reshape(DEPTH, DB, MEM_HEADS * HEAD_DIM, n_mem)
    mem_vt = jnp.transpose(cache_mem_v, (0, 1, 3, 4, 2)).reshape(DEPTH, DB, MEM_HEADS * HEAD_DIM, n_mem)
    mem2d = mem_prompt.reshape(B * n_mem, D_MODEL)

    def pad_new(a):
        a = a.reshape(DB, T, a.shape[-1])
        return jnp.concatenate([a, jnp.zeros((DB, TPAD - T, a.shape[-1]), a.dtype)], axis=1)

    def post(x, mix, mem, i, wo_rows, tm):
        wo = w_out[i]
        wo1 = wo[:N_HEADS * HEAD_DIM]
        if wo_rows is not None:
            wo1 = wo1.reshape(N_HEADS, HEAD_DIM, D_MODEL)[wo_rows].reshape(N_HEADS * HEAD_DIM, D_MODEL)
        x1, gate_t = _outproj_router(x, mix, mem, wo1.astype(BF16), wo[N_HEADS * HEAD_DIM:].astype(BF16),
                                     ln1_g[i].reshape(1, -1), ln1_b[i].reshape(1, -1), wrh, wrl, rb, tm)
        wgu = jnp.concatenate([w_gate[i], w_up[i]], axis=-1).astype(BF16)
        return _moe(x1, gate_t.T, wgu, w_down[i].astype(BF16), ln2_g[i].reshape(1, -1), ln2_b[i].reshape(1, -1),
                    _tile(x.shape[0], 1024))

    xp = x_prompt.reshape(NP, D_MODEL)
    xs = x_sample.reshape(NS, D_MODEL)
    ckv_p, kr_p, ckv_s, kr_s = [], [], [], []
    k_p, v_p, ki_p, k_s, v_s, ki_s = [], [], [], [], [], []
    mk_list, mv_list = [], []
    for i in range(DEPTH):
        j = i // 2
        mkf, mvf, mkb, mvb = _mem_kv(mem2d, w_mem_kv[i])
        mk_list.append(mkf.reshape(B, n_mem, MEM_HEADS, HEAD_DIM))
        mv_list.append(mvf.reshape(B, n_mem, MEM_HEADS, HEAD_DIM))
        if i % 2 == 0:
            q, k, vt, ckv, krpad, qm_p = _mla_proj_prompt(xp, w_in_mla[j], mla_q_norm[j], mla_kv_norm[j], w_uq[j],
                                                          w_uk[j], w_uv[j], cos_mp, sin_mp, tm_p)
            mix_p = _mla_flash(q, k, vt, B, t_att, _tile(S, 1024))
            ckv_p.append(ckv.reshape(B, S, KV_LORA))
            kr_p.append(krpad[:, QK_NOPE:QK_NOPE + QK_ROPE].reshape(B, S, QK_ROPE))
            qlat, qr, ckv, krpad, qm_s = _mla_proj_sample(xs, w_in_mla[j], mla_q_norm[j], mla_kv_norm[j], w_uq[j],
                                                          w_uk[j], cos_ms, sin_ms, cos_mr, sin_mr, tm_s)
            kr = krpad[:, QK_NOPE:QK_NOPE + QK_ROPE]
            ckv_s.append(ckv.reshape(DB, T, KV_LORA))
            kr_s.append(kr.reshape(DB, T, QK_ROPE))
            qlat = qlat.reshape(DB, T, N_HEADS, KV_LORA).transpose(0, 2, 1, 3).reshape(DB, N_HEADS * T, KV_LORA)
            qr = qr.reshape(DB, T, N_HEADS, QK_ROPE).transpose(0, 2, 1, 3).reshape(DB, N_HEADS * T, QK_ROPE)
            o_lat = _mla_decode(page_table, qlat, qr, pad_new(ckv), pad_new(kr), cache_mla_ckv, pool_krt,
                                j, pp_mla, ns_mla, T)
            o_lat = o_lat.reshape(DB, N_HEADS, T, KV_LORA).transpose(1, 0, 2, 3).reshape(N_HEADS, NS, KV_LORA)
            o = _head_matmul(o_lat, jnp.transpose(w_uv[j], (1, 0, 2)).astype(BF16))
            mix_s = o.transpose(1, 0, 2).reshape(NS, N_HEADS * HEAD_DIM)
            wo_rows = None
        else:
            q, kb, kf, vf, vt, wt, qidx, kw, kib, qm_p = _dsa_proj(xp, w_in_dsa[j], idx_k_norm_g[j],
                                                                   idx_k_norm_b[j], cos_dp, sin_dp, tm_p)
            k_p.append(kf.reshape(B, S, KV_HEADS, HEAD_DIM))
            v_p.append(vf.reshape(B, S, KV_HEADS, HEAD_DIM))
            ki_p.append(kw[:, :IDX_DIM].reshape(B, S, IDX_DIM))
            sct = _indexer_prompt(qidx, kib, wt, B, t_att, t_att)
            thr, cut = _select_cols(sct, min(TOPK_MAX, S // 4), _tile(S, 512))
            mix_p = _dsa_attn(q, kb, vt, sct, thr, cut, B, t_att, t_att)

            q, kb, kf, vf, vt, wt, qidx, kw, kib, qm_s = _dsa_proj(xs, w_in_dsa[j], idx_k_norm_g[j],
                                                                   idx_k_norm_b[j], cos_ds, sin_ds, tm_s)
            k_s.append(kf.reshape(DB, T, KV_HEADS, HEAD_DIM))
            v_s.append(vf.reshape(DB, T, KV_HEADS, HEAD_DIM))
            kidx_new = kw[:, :IDX_DIM]
            ki_s.append(kidx_new.reshape(DB, T, IDX_DIM))
            qc = jnp.stack([qidx[h, :, (h % 2) * HEAD_DIM:(h % 2 + 1) * HEAD_DIM] for h in range(IDX_HEADS)])
            qc = qc.reshape(IDX_HEADS, DB, T, IDX_DIM).transpose(1, 0, 2, 3).reshape(DB, IDX_HEADS * T, IDX_DIM)
            w_s = kw[:, HEAD_DIM:HEAD_DIM + IDX_HEADS].reshape(DB, T, IDX_HEADS)
            sc_past, sc_new = _indexer_sample(page_table, qc, w_s, pad_new(kidx_new), pool_kidxt, j, pp_idx, T)
            topk_s = min(TOPK_MAX, (past + T) // 4)
            cw_s = 5 * LANE if (past + LANE) % (5 * LANE) == 0 else LANE
            sc_all = jnp.concatenate([sc_past, sc_new], axis=-1).reshape(NS, past + LANE)
            thr, cut = _select_rows(sc_all, topk_s, cw_s, _tile(NS, 128))
            qh = jnp.stack([q[h, :, ((h // 3) % 2) * HEAD_DIM:((h // 3) % 2 + 1) * HEAD_DIM] for h in range(N_HEADS)])
            qh = qh.reshape(KV_HEADS, 3, DB, T, HEAD_DIM).transpose(2, 0, 1, 3, 4)
            eye = jnp.eye(KV_HEADS, dtype=BF16)
            qbd = (qh[:, :, :, :, None, :] * eye[None, :, None, None, :, None]).reshape(DB, N_HEADS * T, KVW)
            o = _dsa_decode(page_table, qbd, sc_past, sc_new, thr.reshape(DB, T, LANE), cut.reshape(DB, T, LANE),
                            pad_new(kf).transpose(0, 2, 1), pad_new(vf), pool_kt, pool_vt, j, pp_dsa, ns_dsa, T)
            o = o.reshape(DB, KV_HEADS, 3, T, KV_HEADS, HEAD_DIM)
            o = jnp.stack([o[:, g, :, :, g, :] for g in range(KV_HEADS)], axis=1)
            o = o.reshape(DB, N_HEADS, T, HEAD_DIM)[:, perm]
            mix_s = o.transpose(0, 2, 1, 3).reshape(NS, N_HEADS * HEAD_DIM)
            wo_rows = perm
        memo_p = _mem_attend(qm_p.reshape(B, S, -1), mkb.reshape(B, n_mem, -1), mvb.reshape(B, n_mem, -1), 1,
                             _tile(S, 512), False)
        qm_s3 = qm_s.reshape(DB, T, -1)
        qm_s3 = jnp.concatenate([qm_s3, jnp.zeros((DB, TPAD - T, qm_s3.shape[-1]), BF16)], axis=1)
        memo_s = _mem_attend(qm_s3, mem_kt, mem_vt, _tile(DB, 8), TPAD, True, layer=i)[:, :T]
        xp = post(xp, mix_p, memo_p.reshape(NP, -1), i, wo_rows, tm_p)
        xs = post(xs, mix_s, memo_s.reshape(NS, -1), i, wo_rows, tm_s)
    return (xp.reshape(B, S, D_MODEL), xs.reshape(DB, T, D_MODEL),
            jnp.stack(ckv_p), jnp.stack(kr_p), jnp.stack(k_p), jnp.stack(v_p), jnp.stack(ki_p),
            jnp.stack(mk_list), jnp.stack(mv_list),
            jnp.stack(ckv_s), jnp.stack(kr_s), jnp.stack(k_s), jnp.stack(v_s), jnp.stack(ki_s))
```

```python
import functools

import jax
import jax.numpy as jnp
from jax import lax
from jax.experimental import pallas as pl
from jax.experimental.pallas import tpu as pltpu

F32 = jnp.float32
BF16 = jnp.bfloat16
I32 = jnp.int32
I16 = jnp.int16

D_MODEL = 1024
DEPTH = 4
PAGE = 128
HEAD_DIM = 64
N_HEADS = 12
MEM_HEADS = 4
Q_LORA = 384
KV_LORA = 256
QK_NOPE = 64
QK_ROPE = 32
MLA_SCALE = (QK_NOPE + QK_ROPE) ** -0.5
HEAD_SCALE = HEAD_DIM ** -0.5
LOG2E = 1.4426950408889634
MLA_QSCALE = MLA_SCALE * LOG2E
DSA_QSCALE = HEAD_SCALE * LOG2E
KV_HEADS = 4
IDX_HEADS = 8
IDX_DIM = 64
TOPK_MAX = 256
ROT_DIM = HEAD_DIM // 4
ROPE_THETA = 500000.0
N_EXPERTS = 16
N_GROUPS = 4
D_EXPERT = 256
ALPHA = (2 * DEPTH) ** 0.25
LN_EPS = 1e-5
RMS_EPS = 1e-6
NEG = -1e30
LANE = 128
SUBLANE = 8
MIN_I32 = -2 ** 31
N_ACC = 8
DSA_PERM = (0, 3, 1, 4, 2, 5, 6, 9, 7, 10, 8, 11)

NT = (((1,), (1,)), ((), ()))


def _cp(sem, vmem_mb=48):
    return pltpu.CompilerParams(dimension_semantics=sem, vmem_limit_bytes=vmem_mb * 1024 * 1024)


def _dot(a, b):
    return jnp.dot(a, b, preferred_element_type=F32)


def _dot_nt(a, b):
    return lax.dot_general(a, b, NT, preferred_element_type=F32)


def _rms(x, g):
    return x * lax.rsqrt(jnp.mean(x * x, axis=-1, keepdims=True) + RMS_EPS) * g


def _ln(x, g, b):
    mu = jnp.mean(x, axis=-1, keepdims=True)
    xc = x - mu
    var = jnp.mean(xc * xc, axis=-1, keepdims=True)
    return xc * lax.rsqrt(var + LN_EPS) * g + b


def _softmax_step(s, m_prev, l_prev, axis=1):
    m_new = jnp.maximum(m_prev, jnp.max(s, axis=axis, keepdims=True))
    a = jnp.exp2(m_prev - m_new)
    p = jnp.exp2(s - m_new)
    return m_new, a, p, a * l_prev + jnp.sum(p, axis=axis, keepdims=True)


def _merge_streams(ms, ls, accs):
    m = functools.reduce(jnp.maximum, ms)
    ws = [jnp.exp2(mi - m) for mi in ms]
    l = functools.reduce(lambda a, b: a + b, [w * li for w, li in zip(ws, ls)])
    acc = functools.reduce(lambda a, b: a + b, [w * ai for w, ai in zip(ws, accs)])
    return m, l, acc


def _score_keys(s):
    b = lax.bitcast_convert_type(s, I32)
    return jnp.where(s == 0.0, 0, b ^ ((b >> 31) & 0x7FFFFFFF))


def _key_to_score(k, shape):
    return lax.bitcast_convert_type(jnp.broadcast_to(k ^ ((k >> 31) & 0x7FFFFFFF), shape), F32)


def _rope_tables(pos, rot_dim, period, offset, width):
    half = rot_dim // 2
    inv = ROPE_THETA ** (-jnp.arange(half, dtype=jnp.float32) / half)
    ang = pos.astype(jnp.float32)[:, None] * inv
    c, s = jnp.cos(ang), jnp.sin(ang)
    t = pos.shape[0]
    cg = jnp.ones((t, period), F32).at[:, offset:offset + rot_dim].set(jnp.concatenate([c, c], 1))
    sg = jnp.zeros((t, period), F32).at[:, offset:offset + rot_dim].set(jnp.concatenate([s, s], 1))
    return jnp.tile(cg, (1, width // period)), jnp.tile(sg, (1, width // period))


def _rot_cols(w, half):
    return jnp.concatenate([-w[..., half:2 * half], w[..., :half]], axis=-1)


def _rot_heads(w, heads, dim, rot):
    w3 = w.reshape(w.shape[0], heads, dim)
    r = jnp.concatenate([_rot_cols(w3[..., :rot], rot // 2), jnp.zeros_like(w3[..., rot:])], axis=-1)
    return r.reshape(w.shape[0], heads * dim)


def _mla_front(x_ref, w1_ref, w2_ref, qn_ref, kvn_ref, cos_ref, sin_ref):
    xb = x_ref[...].astype(BF16)
    y = _dot(xb, w1_ref[...])
    cqn = _rms(y[:, :Q_LORA], qn_ref[...]).astype(BF16)
    ckv = _rms(y[:, Q_LORA:Q_LORA + KV_LORA], kvn_ref[...])
    qmem = y[:, 640:896]
    cos = cos_ref[...]
    sin = sin_ref[...]
    krp = y[:, 896:1024] * cos + y[:, 1024:1152] * sin
    q2 = _dot(cqn, w2_ref[...])
    return q2, ckv, qmem, krp, cos, sin


def _mla_proj_prompt_kernel(x_ref, w1_ref, w2_ref, wk_ref, wvt_ref, qn_ref, kvn_ref, cos_ref, sin_ref,
                            q_ref, k_ref, vt_ref, ckv_ref, kr_ref, qm_ref):
    q2, ckv, qmem, krp, cos, sin = _mla_front(x_ref, w1_ref, w2_ref, qn_ref, kvn_ref, cos_ref, sin_ref)
    hw = N_HEADS * LANE
    for h in range(N_HEADS):
        qh = q2[:, h * LANE:(h + 1) * LANE] * cos + q2[:, hw + h * LANE:hw + (h + 1) * LANE] * sin
        q_ref[h] = (qh * MLA_QSCALE).astype(BF16)
    ckv_ref[...] = ckv
    kr_ref[...] = krp
    qm_ref[...] = (qmem * HEAD_SCALE).astype(BF16)
    cb = ckv.astype(BF16)
    kn = _dot(cb, wk_ref[...])
    for h in range(N_HEADS):
        k_ref[h] = (kn[:, h * LANE:(h + 1) * LANE] + krp).astype(BF16)
    vt_ref[...] = _dot_nt(wvt_ref[...], cb).astype(BF16)


def _mla_proj_sample_kernel(x_ref, w1_ref, w2_ref, wukt_ref, qn_ref, kvn_ref, cos_ref, sin_ref,
                            cosr_ref, sinr_ref, qlat_ref, qr_ref, ckv_ref, kr_ref, qm_ref):
    q2, ckv, qmem, krp, _, _ = _mla_front(x_ref, w1_ref, w2_ref, qn_ref, kvn_ref, cos_ref, sin_ref)
    hw = N_HEADS * LANE
    rw = N_HEADS * QK_ROPE
    for h in range(N_HEADS):
        qh = (q2[:, h * LANE:(h + 1) * LANE] * MLA_QSCALE).astype(BF16)
        qlat_ref[:, h * KV_LORA:(h + 1) * KV_LORA] = _dot(qh, wukt_ref[h]).astype(BF16)
    qr = q2[:, hw:hw + rw] * cosr_ref[...] + q2[:, hw + rw:hw + 2 * rw] * sinr_ref[...]
    qr_ref[...] = (qr * MLA_QSCALE).astype(BF16)
    ckv_ref[...] = ckv
    kr_ref[...] = krp
    qm_ref[...] = (qmem * HEAD_SCALE).astype(BF16)


def _mla_w1(w_in):
    w_cq, w_ckv = w_in[:, :Q_LORA], w_in[:, Q_LORA:Q_LORA + KV_LORA]
    w_kr = w_in[:, Q_LORA + KV_LORA:Q_LORA + KV_LORA + QK_ROPE]
    w_qm = w_in[:, Q_LORA + KV_LORA + QK_ROPE:]
    z = lambda n: jnp.zeros((D_MODEL, n), F32)
    kr_pad = jnp.concatenate([z(QK_NOPE), w_kr, z(LANE - QK_NOPE - QK_ROPE)], 1)
    kr_rot = jnp.concatenate([z(QK_NOPE), _rot_cols(w_kr, QK_ROPE // 2), z(LANE - QK_NOPE - QK_ROPE)], 1)
    return jnp.concatenate([w_cq, w_ckv, w_qm, kr_pad, kr_rot], 1).astype(BF16)


def _mla_proj_prompt(x, w_in, q_norm, kv_norm, w_uq, w_uk, w_uv, cos, sin, tm):
    n = x.shape[0]
    w1 = _mla_w1(w_in)
    zq = jnp.zeros((Q_LORA, N_HEADS, LANE - QK_NOPE - QK_ROPE), F32)
    uq_pad = jnp.concatenate([w_uq, zq], -1).reshape(Q_LORA, N_HEADS * LANE)
    uq_rot = jnp.concatenate([jnp.zeros((Q_LORA, N_HEADS, QK_NOPE), F32),
                              _rot_cols(w_uq[..., QK_NOPE:], QK_ROPE // 2), zq], -1).reshape(Q_LORA, N_HEADS * LANE)
    w2 = jnp.concatenate([uq_pad, uq_rot], 1).astype(BF16)
    wk = jnp.concatenate([w_uk, jnp.zeros((KV_LORA, N_HEADS, LANE - QK_NOPE), F32)], -1)
    wk = wk.reshape(KV_LORA, N_HEADS * LANE).astype(BF16)
    wvt = w_uv.reshape(KV_LORA, N_HEADS * HEAD_DIM).T.astype(BF16)
    nt = cos.shape[0] // tm
    full = lambda a: pl.BlockSpec(a.shape, lambda i: (0,) * a.ndim)
    rows = lambda w: pl.BlockSpec((tm, w), lambda i: (i, 0))
    heads = pl.BlockSpec((N_HEADS, tm, LANE), lambda i: (0, i, 0))
    tab = pl.BlockSpec((tm, LANE), lambda i: (i % nt, 0))
    qn, kvn = q_norm.reshape(1, -1), kv_norm.reshape(1, -1)
    return pl.pallas_call(
        _mla_proj_prompt_kernel,
        grid=(n // tm,),
        in_specs=[rows(D_MODEL), full(w1), full(w2), full(wk), full(wvt), full(qn), full(kvn), tab, tab],
        out_specs=[heads, heads, pl.BlockSpec((None, N_HEADS * HEAD_DIM, tm), lambda i: (i, 0, 0)), rows(KV_LORA),
                   rows(LANE), rows(MEM_HEADS * HEAD_DIM)],
        out_shape=[jax.ShapeDtypeStruct((N_HEADS, n, LANE), BF16), jax.ShapeDtypeStruct((N_HEADS, n, LANE), BF16),
                   jax.ShapeDtypeStruct((n // tm, N_HEADS * HEAD_DIM, tm), BF16), jax.ShapeDtypeStruct((n, KV_LORA), F32),
                   jax.ShapeDtypeStruct((n, LANE), F32), jax.ShapeDtypeStruct((n, MEM_HEADS * HEAD_DIM), BF16)],
        compiler_params=_cp(("parallel",)),
        name="mla_proj_prompt",
    )(x, w1, w2, wk, wvt, qn, kvn, cos, sin)


def _mla_proj_sample(x, w_in, q_norm, kv_norm, w_uq, w_uk, cos, sin, cosr, sinr, tm):
    n = x.shape[0]
    w1 = _mla_w1(w_in)
    zq = jnp.zeros((Q_LORA, N_HEADS, LANE - QK_NOPE - QK_ROPE), F32)
    uq_pad = jnp.concatenate([w_uq, zq], -1).reshape(Q_LORA, N_HEADS * LANE)
    w_qr = w_uq[..., QK_NOPE:]
    w2 = jnp.concatenate([uq_pad, w_qr.reshape(Q_LORA, -1),
                          _rot_cols(w_qr, QK_ROPE // 2).reshape(Q_LORA, -1)], 1).astype(BF16)
    wukt = jnp.transpose(w_uk, (1, 2, 0))
    wukt = jnp.concatenate([wukt, jnp.zeros((N_HEADS, LANE - QK_NOPE, KV_LORA), F32)], 1).astype(BF16)
    full = lambda a: pl.BlockSpec(a.shape, lambda i: (0,) * a.ndim)
    rows = lambda w: pl.BlockSpec((tm, w), lambda i: (i, 0))
    qn, kvn = q_norm.reshape(1, -1), kv_norm.reshape(1, -1)
    return pl.pallas_call(
        _mla_proj_sample_kernel,
        grid=(n // tm,),
        in_specs=[rows(D_MODEL), full(w1), full(w2), full(wukt), full(qn), full(kvn), rows(LANE), rows(LANE),
                  rows(N_HEADS * QK_ROPE), rows(N_HEADS * QK_ROPE)],
        out_specs=[rows(N_HEADS * KV_LORA), rows(N_HEADS * QK_ROPE), rows(KV_LORA), rows(LANE),
                   rows(MEM_HEADS * HEAD_DIM)],
        out_shape=[jax.ShapeDtypeStruct((n, N_HEADS * KV_LORA), BF16),
                   jax.ShapeDtypeStruct((n, N_HEADS * QK_ROPE), BF16), jax.ShapeDtypeStruct((n, KV_LORA), F32),
                   jax.ShapeDtypeStruct((n, LANE), F32), jax.ShapeDtypeStruct((n, MEM_HEADS * HEAD_DIM), BF16)],
        compiler_params=_cp(("parallel",)),
        name="mla_proj_sample",
    )(x, w1, w2, wukt, qn, kvn, cos, sin, cosr, sinr)


def _causal_steps(nq, tq, tk):
    qs, ks = [], []
    for qi in range(nq):
        for ki in range(((qi + 1) * tq - 1) // tk + 1):
            qs.append(qi)
            ks.append(ki)
    return jnp.asarray(qs, I32), jnp.asarray(ks, I32)


def _mla_flash_kernel(qt_ref, kt_ref, q_ref, k_ref, vt_ref, o_ref, m_ref, l_ref, acc_ref, *, tq, tk):
    qi = qt_ref[pl.program_id(2)]
    ki = kt_ref[pl.program_id(2)]
    last = ((qi + 1) * tq - 1) // tk

    @pl.when(ki == 0)
    def _():
        m_ref[...] = jnp.full(m_ref.shape, NEG, F32)
        l_ref[...] = jnp.zeros(l_ref.shape, F32)
        acc_ref[...] = jnp.zeros(acc_ref.shape, F32)

    def step(masked):
        ss = [_dot_nt(k_ref[hh], q_ref[hh]) for hh in range(2)]
        if masked:
            key = ki * tk + lax.broadcasted_iota(I32, (tk, tq), 0)
            qry = qi * tq + lax.broadcasted_iota(I32, (tk, tq), 1)
            causal = key <= qry
        for hh in range(2):
            s = jnp.where(causal, ss[hh], NEG) if masked else ss[hh]
            m_new, a, p, l_new = _softmax_step(s, m_ref[hh], l_ref[hh], axis=0)
            vt = jnp.concatenate([vt_ref[c, hh * HEAD_DIM:(hh + 1) * HEAD_DIM, :] for c in range(vt_ref.shape[0])],
                                 axis=1)
            acc_ref[hh] = a * acc_ref[hh] + _dot(vt, p.astype(BF16))
            m_ref[hh] = m_new
            l_ref[hh] = l_new

    first_masked = (qi * tq + 1) // tk

    @pl.when(ki < first_masked)
    def _():
        step(False)

    @pl.when(ki >= first_masked)
    def _():
        step(True)

    @pl.when(ki == last)
    def _():
        ot = jnp.concatenate([acc_ref[0] / l_ref[0], acc_ref[1] / l_ref[1]], axis=0)
        o_ref[...] = ot.T.astype(o_ref.dtype)


def _mla_flash(q, k, vt, batch, tq, tk):
    n = q.shape[1]
    s = n // batch
    nq, nk = s // tq, s // tk
    tv = vt.shape[2]
    qtab, ktab = _causal_steps(nq, tq, tk)
    grid_spec = pltpu.PrefetchScalarGridSpec(
        num_scalar_prefetch=2,
        grid=(batch, N_HEADS // 2, qtab.shape[0]),
        in_specs=[pl.BlockSpec((2, tq, LANE), lambda b, hp, t, qt, kt: (hp, b * nq + qt[t], 0)),
                  pl.BlockSpec((2, tk, LANE), lambda b, hp, t, qt, kt: (hp, b * nk + kt[t], 0)),
                  pl.BlockSpec((tk // tv, 2 * HEAD_DIM, tv), lambda b, hp, t, qt, kt: (b * nk + kt[t], hp, 0))],
        out_specs=pl.BlockSpec((tq, LANE), lambda b, hp, t, qt, kt: (b * nq + qt[t], hp)),
        scratch_shapes=[pltpu.VMEM((2, 1, tq), F32), pltpu.VMEM((2, 1, tq), F32), pltpu.VMEM((2, HEAD_DIM, tq), F32)],
    )
    return pl.pallas_call(
        functools.partial(_mla_flash_kernel, tq=tq, tk=tk),
        grid_spec=grid_spec,
        out_shape=jax.ShapeDtypeStruct((n, N_HEADS * HEAD_DIM), BF16),
        compiler_params=_cp(("parallel", "parallel", "arbitrary")),
        name="mla_flash",
    )(qtab, ktab, q, k, vt)


def _mla_decode_kernel(pt_ref, qlat_ref, qr_ref, cn_ref, kn_ref, *rest, pp, ns, t_new):
    ck_refs, krt_refs = rest[:pp], rest[pp:2 * pp]
    o_ref, m_ref, l_ref, acc_ref = rest[2 * pp:]
    p_id = pl.program_id(1)
    rows = qlat_ref.shape[1]
    per = pp // ns

    @pl.when(p_id == 0)
    def _():
        m_ref[...] = jnp.full(m_ref.shape, NEG, F32)
        l_ref[...] = jnp.zeros(l_ref.shape, F32)
        acc_ref[...] = jnp.zeros(acc_ref.shape, F32)

    qlat = qlat_ref[0]
    qr = qr_ref[0]
    for st in range(ns):
        pages = range(st * per, (st + 1) * per)
        cks = [ck_refs[i][...].astype(BF16) for i in pages]
        s = jnp.concatenate([_dot_nt(qlat, ck) + _dot(qr, krt_refs[i][...].astype(BF16))
                             for ck, i in zip(cks, pages)], axis=1)
        m_new, a, p, l_new = _softmax_step(s, m_ref[st], l_ref[st])
        pb = p.astype(BF16)
        pv = _dot(pb[:, :PAGE], cks[0])
        for i in range(1, per):
            pv = pv + _dot(pb[:, i * PAGE:(i + 1) * PAGE], cks[i])
        acc_ref[st] = a * acc_ref[st] + pv
        m_ref[st] = m_new
        l_ref[st] = l_new

    @pl.when(p_id == pl.num_programs(1) - 1)
    def _():
        m, l, acc = _merge_streams([m_ref[st] for st in range(ns)], [l_ref[st] for st in range(ns)],
                                   [acc_ref[st] for st in range(ns)])
        cn = cn_ref[0].astype(BF16)
        kn = kn_ref[0].astype(BF16)
        npad = cn.shape[0]
        t = lax.broadcasted_iota(I32, (rows, npad), 0) % t_new
        u = lax.broadcasted_iota(I32, (rows, npad), 1)
        s2 = jnp.where(u <= t, _dot_nt(qlat, cn) + _dot_nt(qr, kn), NEG)
        m2, a2, p2, l2 = _softmax_step(s2, m, l)
        o_ref[0] = (a2 * acc + _dot(p2.astype(BF16), cn)) / l2


def _mla_decode(page_table, qlat, qr, ckv_new, kr_new, pool_ckv, pool_krt, j, pp, ns, t_new):
    db, rows, _ = qlat.shape
    npg = page_table.shape[1]
    pt = page_table.reshape(-1)
    npad = ckv_new.shape[1]
    page = lambda i: (lambda b, p, pt_: (j, pt_[b * npg + p * pp + i], 0, 0))
    per_b = lambda r, w: pl.BlockSpec((1, r, w), lambda b, p, pt_: (b, 0, 0))
    grid_spec = pltpu.PrefetchScalarGridSpec(
        num_scalar_prefetch=1,
        grid=(db, npg // pp),
        in_specs=[per_b(rows, KV_LORA), per_b(rows, QK_ROPE), per_b(npad, KV_LORA), per_b(npad, QK_ROPE)]
        + [pl.BlockSpec((None, None, PAGE, KV_LORA), page(i)) for i in range(pp)]
        + [pl.BlockSpec((None, None, QK_ROPE, PAGE), page(i)) for i in range(pp)],
        out_specs=per_b(rows, KV_LORA),
        scratch_shapes=[pltpu.VMEM((ns, rows, 1), F32), pltpu.VMEM((ns, rows, 1), F32),
                        pltpu.VMEM((ns, rows, KV_LORA), F32)],
    )
    return pl.pallas_call(
        functools.partial(_mla_decode_kernel, pp=pp, ns=ns, t_new=t_new),
        grid_spec=grid_spec,
        out_shape=jax.ShapeDtypeStruct((db, rows, KV_LORA), F32),
        compiler_params=_cp(("parallel", "arbitrary")),
        name="mla_decode",
    )(pt, qlat, qr, ckv_new, kr_new, *([pool_ckv] * pp), *([pool_krt] * pp))


def _head_matmul_kernel(x_ref, w_ref, o_ref):
    o_ref[...] = _dot(x_ref[...].astype(BF16), w_ref[...]).astype(o_ref.dtype)


def _head_matmul(x, w):
    h, n, k = x.shape
    m = w.shape[2]
    return pl.pallas_call(
        _head_matmul_kernel,
        grid=(h,),
        in_specs=[pl.BlockSpec((None, n, k), lambda i: (i, 0, 0)), pl.BlockSpec((None, k, m), lambda i: (i, 0, 0))],
        out_specs=pl.BlockSpec((None, n, m), lambda i: (i, 0, 0)),
        out_shape=jax.ShapeDtypeStruct((h, n, m), BF16),
        compiler_params=_cp(("parallel",)),
        name="head_matmul",
    )(x, w)


def _dsa_proj_kernel(x_ref, wq_ref, wk_ref, wv_ref, wvt_ref, wwt_ref, wqi_ref, wm_ref, wki_ref, lng_ref, lnb_ref,
                     sgn_ref, gperm_ref, brot_ref, cos_ref, sin_ref,
                     q_ref, k_ref, kf_ref, vf_ref, vt_ref, wt_ref, qi_ref, kw_ref, kib_ref, qm_ref):
    tm = x_ref.shape[0]
    xb = x_ref[...].astype(BF16)
    cos = cos_ref[...]
    sin = sin_ref[...]
    lane = lax.broadcasted_iota(I32, (tm, LANE), 1)
    low = lane < HEAD_DIM

    def roped(w_ref, nblk):
        y = _dot(xb, w_ref[...])
        return [y[:, b * LANE:(b + 1) * LANE] * cos + y[:, (nblk + b) * LANE:(nblk + b + 1) * LANE] * sin
                for b in range(nblk)]

    qb = roped(wq_ref, N_HEADS // 2)
    for h in range(N_HEADS):
        g, r = h // 3, h % 3
        blk = qb[(g // 2) * 3 + r] * DSA_QSCALE
        q_ref[h] = jnp.where(low if g % 2 == 0 else jnp.logical_not(low), blk, 0.0).astype(BF16)
    kb = roped(wk_ref, KV_HEADS // 2)
    for b in range(KV_HEADS // 2):
        kf_ref[:, b * LANE:(b + 1) * LANE] = kb[b]
        k_ref[b] = kb[b].astype(BF16)
    vf_ref[...] = _dot(xb, wv_ref[...])
    vt_ref[...] = _dot_nt(wvt_ref[...], xb).astype(BF16)
    wt_ref[...] = _dot_nt(wwt_ref[...], xb)
    qib = roped(wqi_ref, IDX_HEADS // 2)
    for h in range(IDX_HEADS):
        qi_ref[h] = jnp.where(low if h % 2 == 0 else jnp.logical_not(low), qib[h // 2], 0.0).astype(BF16)
    qm_ref[...] = (_dot(xb, wm_ref[...]) * HEAD_SCALE).astype(BF16)
    yk = _dot(xb, wki_ref[...])
    c, c_rot, wpad = yk[:, :LANE], yk[:, LANE:2 * LANE], yk[:, 2 * LANE:3 * LANE]
    mu = jnp.sum(jnp.where(low, c, 0.0), axis=1, keepdims=True) * (1.0 / IDX_DIM)
    xc = c - mu
    var = jnp.sum(jnp.where(low, xc * xc, 0.0), axis=1, keepdims=True) * (1.0 / IDX_DIM)
    rstd = lax.rsqrt(var + LN_EPS)
    ki = xc * rstd * lng_ref[...] + lnb_ref[...]
    ki_rot = (c_rot - sgn_ref[...] * mu) * rstd * gperm_ref[...] + brot_ref[...]
    kir = ki * cos + ki_rot * sin
    kib_ref[...] = kir.astype(BF16)
    kw_ref[...] = jnp.where(low, kir, wpad)


def _dsa_proj(x, w_in, ln_g, ln_b, cos, sin, tm):
    n = x.shape[0]
    o = 0
    cols = []
    for wdt in (N_HEADS * HEAD_DIM, KV_HEADS * HEAD_DIM, KV_HEADS * HEAD_DIM, IDX_HEADS * IDX_DIM, IDX_DIM, IDX_HEADS,
                MEM_HEADS * HEAD_DIM):
        cols.append(w_in[:, o:o + wdt])
        o += wdt
    w_q, w_k, w_v, w_qi, w_ki, w_w, w_qm = cols
    perm = jnp.asarray(DSA_PERM)
    w_qp = w_q.reshape(D_MODEL, N_HEADS, HEAD_DIM)[:, perm].reshape(D_MODEL, -1)
    wq = jnp.concatenate([w_qp, _rot_heads(w_qp, N_HEADS, HEAD_DIM, ROT_DIM)], 1).astype(BF16)
    wk = jnp.concatenate([w_k, _rot_heads(w_k, KV_HEADS, HEAD_DIM, ROT_DIM)], 1).astype(BF16)
    wqi = jnp.concatenate([w_qi, _rot_heads(w_qi, IDX_HEADS, IDX_DIM, ROT_DIM)], 1).astype(BF16)
    w_ki_rot = _rot_heads(w_ki, 1, IDX_DIM, ROT_DIM)
    w_wpad = jnp.concatenate([jnp.zeros((D_MODEL, HEAD_DIM), F32), w_w,
                              jnp.zeros((D_MODEL, LANE - HEAD_DIM - IDX_HEADS), F32)], 1)
    wki = jnp.concatenate([w_ki, w_ki, w_ki_rot, w_ki_rot, w_wpad], 1).astype(BF16)
    wv, wm = w_v.astype(BF16), w_qm.astype(BF16)
    wvt = w_v.T.astype(BF16)
    wwt = jnp.concatenate([w_w.T, jnp.zeros((2 * SUBLANE - IDX_HEADS, D_MODEL), F32)], 0).astype(BF16)
    half = ROT_DIM // 2
    zr = jnp.zeros((IDX_DIM - ROT_DIM,), F32)
    dup = lambda a: jnp.concatenate([a, a]).reshape(1, LANE)
    sgn = dup(jnp.concatenate([-jnp.ones((half,), F32), jnp.ones((half,), F32), zr]))
    gperm = dup(jnp.concatenate([ln_g[half:ROT_DIM], ln_g[:half], zr]))
    brot = dup(jnp.concatenate([-ln_b[half:ROT_DIM], ln_b[:half], zr]))
    lng, lnb = dup(ln_g), dup(ln_b)
    nt = cos.shape[0] // tm
    full = lambda a: pl.BlockSpec(a.shape, lambda i: (0,) * a.ndim)
    rows = lambda w: pl.BlockSpec((tm, w), lambda i: (i, 0))
    heads = lambda h: pl.BlockSpec((h, tm, LANE), lambda i: (0, i, 0))
    colsp = lambda r: pl.BlockSpec((r, tm), lambda i: (0, i))
    tab = pl.BlockSpec((tm, LANE), lambda i: (i % nt, 0))
    sds = jax.ShapeDtypeStruct
    return pl.pallas_call(
        _dsa_proj_kernel,
        grid=(n // tm,),
        in_specs=[rows(D_MODEL), full(wq), full(wk), full(wv), full(wvt), full(wwt), full(wqi), full(wm), full(wki),
                  full(lng), full(lnb), full(sgn), full(gperm), full(brot), tab, tab],
        out_specs=[heads(N_HEADS), heads(KV_HEADS // 2), rows(2 * LANE), rows(2 * LANE),
                   pl.BlockSpec((None, 2 * LANE, tm), lambda i: (i, 0, 0)),
                   colsp(2 * SUBLANE), heads(IDX_HEADS), rows(LANE), rows(LANE), rows(2 * LANE)],
        out_shape=[sds((N_HEADS, n, LANE), BF16), sds((KV_HEADS // 2, n, LANE), BF16), sds((n, 2 * LANE), F32),
                   sds((n, 2 * LANE), F32), sds((n // tm, 2 * LANE, tm), BF16), sds((2 * SUBLANE, n), F32),
                   sds((IDX_HEADS, n, LANE), BF16), sds((n, LANE), F32), sds((n, LANE), BF16),
                   sds((n, 2 * LANE), BF16)],
        compiler_params=_cp(("parallel",), 56),
        name="dsa_proj",
    )(x, wq, wk, wv, wvt, wwt, wqi, wm, wki, lng, lnb, sgn, gperm, brot, cos, sin)


def _indexer_prompt_kernel(qt_ref, kt_ref, qi_ref, ki_ref, wt_ref, sc_ref, *, tq, tk):
    qi = qt_ref[pl.program_id(1)]
    ki = kt_ref[pl.program_id(1)]
    kb = ki_ref[...]
    w = wt_ref[...]
    acc = None
    for h in range(IDX_HEADS):
        d = jnp.maximum(_dot_nt(kb, qi_ref[h]), 0.0) * w[h:h + 1]
        acc = d if acc is None else acc + d
    key = ki * tk + lax.broadcasted_iota(I32, (tk, tq), 0)
    qry = qi * tq + lax.broadcasted_iota(I32, (tk, tq), 1)
    res = jnp.where(key <= qry, acc, NEG)
    for c in range(tq // LANE):
        sc_ref[c] = res[:, c * LANE:(c + 1) * LANE]


def _indexer_prompt(qidx, kib, wt, batch, tq, tk):
    n = kib.shape[0]
    s = n // batch
    nq, nk = s // tq, s // tk
    qtab, ktab = _causal_steps(nq, tq, tk)
    grid_spec = pltpu.PrefetchScalarGridSpec(
        num_scalar_prefetch=2,
        grid=(batch, qtab.shape[0]),
        in_specs=[pl.BlockSpec((IDX_HEADS, tq, LANE), lambda b, t, qt, kt: (0, b * nq + qt[t], 0)),
                  pl.BlockSpec((tk, LANE), lambda b, t, qt, kt: (b * nk + kt[t], 0)),
                  pl.BlockSpec((2 * SUBLANE, tq), lambda b, t, qt, kt: (0, b * nq + qt[t]))],
        out_specs=pl.BlockSpec((tq // LANE, tk, LANE), lambda b, t, qt, kt: (b * nq + qt[t], kt[t], 0)),
    )
    return pl.pallas_call(
        functools.partial(_indexer_prompt_kernel, tq=tq, tk=tk),
        grid_spec=grid_spec,
        out_shape=jax.ShapeDtypeStruct((n // LANE, s, LANE), F32),
        compiler_params=_cp(("parallel", "arbitrary")),
        name="indexer_prompt",
    )(qtab, ktab, qidx, kib, wt)


def _indexer_sample_kernel(pt_ref, q_ref, w_ref, kn_ref, *rest, pp, t_new):
    kt_refs = rest[:pp]
    sc_ref, scn_ref = rest[pp:]
    q = q_ref[0]
    w = w_ref[0]

    def weighted(d):
        acc = None
        for h in range(IDX_HEADS):
            t = d[h * t_new:(h + 1) * t_new] * w[:, h:h + 1]
            acc = t if acc is None else acc + t
        return acc

    for i in range(pp):
        sc_ref[0, :, i * PAGE:(i + 1) * PAGE] = weighted(jnp.maximum(_dot(q, kt_refs[i][...].astype(BF16)), 0.0))

    @pl.when(pl.program_id(1) == pl.num_programs(1) - 1)
    def _():
        sn = weighted(jnp.maximum(_dot_nt(q, kn_ref[0].astype(BF16)), 0.0))
        npad = sn.shape[1]
        t = lax.broadcasted_iota(I32, (t_new, npad), 0)
        u = lax.broadcasted_iota(I32, (t_new, npad), 1)
        scn_ref[0] = jnp.full((t_new, LANE), NEG, F32)
        scn_ref[0, :, :npad] = jnp.where(u <= t, sn, NEG)


def _indexer_sample(page_table, q, w, kidx_new, pool_kidxt, j, pp, t_new):
    db = q.shape[0]
    npg = page_table.shape[1]
    pt = page_table.reshape(-1)
    npad = kidx_new.shape[1]
    per_b = lambda r, c: pl.BlockSpec((1, r, c), lambda b, p, pt_: (b, 0, 0))
    pool = lambda i: pl.BlockSpec((None, None, IDX_DIM, PAGE), lambda b, p, pt_: (j, pt_[b * npg + p * pp + i], 0, 0))
    grid_spec = pltpu.PrefetchScalarGridSpec(
        num_scalar_prefetch=1,
        grid=(db, npg // pp),
        in_specs=[per_b(IDX_HEADS * t_new, IDX_DIM), per_b(t_new, IDX_HEADS), per_b(npad, IDX_DIM)]
        + [pool(i) for i in range(pp)],
        out_specs=[pl.BlockSpec((1, t_new, pp * PAGE), lambda b, p, pt_: (b, 0, p)), per_b(t_new, LANE)],
    )
    return pl.pallas_call(
        functools.partial(_indexer_sample_kernel, pp=pp, t_new=t_new),
        grid_spec=grid_spec,
        out_shape=[jax.ShapeDtypeStruct((db, t_new, npg * PAGE), F32), jax.ShapeDtypeStruct((db, t_new, LANE), F32)],
        compiler_params=_cp(("parallel", "arbitrary")),
        name="indexer_sample",
    )(pt, q, w, kidx_new, *([pool_kidxt] * pp))


def _select_rows_kernel(sc_ref, thr_ref, cut_ref, key_ref, *, topk, cw, tq, idx_bits):
    nch = sc_ref.shape[1] // cw

    def make_keys(c, carry):
        off = pl.multiple_of(c * cw, cw)
        key_ref[:, pl.ds(off, cw)] = _score_keys(sc_ref[:, pl.ds(off, cw)])
        return carry

    lax.fori_loop(0, nch, make_keys, 0)
    lane = lax.broadcasted_iota(I32, (tq, LANE), 1)

    def count(pred):
        def body(c, acc):
            off = pl.multiple_of(c * cw, cw)
            for u in range(cw // LANE):
                kk = key_ref[:, pl.ds(off + u * LANE, LANE)]
                acc = acc + jnp.where(pred(kk, off + u * LANE + lane), 1.0, 0.0)
            return acc
        acc = lax.fori_loop(0, nch, body, jnp.zeros((tq, LANE), F32))
        return jnp.sum(acc, axis=1, keepdims=True)

    thrk = _kth_key(count, (tq, 1), topk)
    cut = _tie_cut(count, thrk, (tq, 1), topk, idx_bits)
    thr_ref[...] = _key_to_score(thrk, (tq, LANE))
    cut_ref[...] = jnp.broadcast_to(cut, (tq, LANE))


def _kth_key(count, shape, topk):
    def value_bit(it, ans):
        cand = ans | jnp.left_shift(jnp.int32(1), 31 - it)
        return jnp.where(count(lambda kk, idx: kk >= (cand ^ MIN_I32)) >= topk, cand, ans)

    return lax.fori_loop(0, 32, value_bit, jnp.zeros(shape, I32)) ^ MIN_I32


def _tie_cut(count, thrk, shape, topk, idx_bits):
    n_gt = count(lambda kk, idx: kk > thrk)
    n_ge = count(lambda kk, idx: kk >= thrk)
    need = topk - n_gt

    def tie_bit(it, m):
        cand = m | jnp.left_shift(jnp.int32(1), idx_bits - 1 - it)
        cnt = count(lambda kk, idx: jnp.where(kk == thrk, idx, cand) < cand)
        return jnp.where(cnt < need, cand, m)

    return lax.cond(jnp.max(jnp.abs(n_ge - topk)) > 0.0,
                    lambda: lax.fori_loop(0, idx_bits, tie_bit, jnp.zeros(shape, I32)),
                    lambda: jnp.full(shape, 2 ** 30, I32))


def _select_rows(sc, topk, cw, tq):
    n, length = sc.shape
    idx_bits = max(1, (length - 1).bit_length())
    return pl.pallas_call(
        functools.partial(_select_rows_kernel, topk=topk, cw=cw, tq=tq, idx_bits=idx_bits),
        grid=(n // tq,),
        in_specs=[pl.BlockSpec((tq, length), lambda i: (i, 0))],
        out_specs=[pl.BlockSpec((tq, LANE), lambda i: (i, 0)), pl.BlockSpec((tq, LANE), lambda i: (i, 0))],
        out_shape=[jax.ShapeDtypeStruct((n, LANE), F32), jax.ShapeDtypeStruct((n, LANE), I32)],
        scratch_shapes=[pltpu.VMEM((tq, length), I32)],
        compiler_params=_cp(("parallel",)),
        name="topk_select_rows",
    )(sc)


def _select_cols_kernel(sc_ref, thr_ref, cut_ref, key_ref, hi_ref, lo_ref, *, topk, cw, idx_bits, nqt, grp):
    qt_last = (pl.program_id(0) % (nqt // grp)) * grp + grp - 1
    nch = (jnp.maximum((qt_last + 1) * LANE, topk) + cw - 1) // cw
    gs = range(grp)
    pk = 2 * SUBLANE
    half = 2 ** 15

    def make_keys(c, carry):
        rows = pl.ds(pl.multiple_of(c * cw, cw), cw)
        for g in gs:
            k = _score_keys(sc_ref[g, rows, :])
            key_ref[g, rows, :] = k
            hi_ref[g, rows, :] = (k >> 16).astype(I16)
            lo_ref[g, rows, :] = ((k & 0xFFFF) - half).astype(I16)
        return carry

    lax.fori_loop(0, nch, make_keys, 0)
    sub = lax.broadcasted_iota(I32, (SUBLANE, LANE), 0)

    def sweep(ref, nrow, dtype, hit):
        one, zero = jnp.ones((), dtype), jnp.zeros((), dtype)

        def body(c, accs):
            off = pl.multiple_of(c * cw, cw)
            accs = [list(a) for a in accs]
            for g in gs:
                blk = ref[g, pl.ds(off, cw), :]
                for u in range(cw // nrow):
                    accs[g][u % N_ACC] = accs[g][u % N_ACC] + jnp.where(
                        hit(g, blk[u * nrow:(u + 1) * nrow], off + u * nrow), one, zero)
            return tuple(tuple(a) for a in accs)

        init = tuple(tuple(jnp.zeros((nrow, LANE), dtype) for _ in range(N_ACC)) for _ in gs)
        accs = lax.fori_loop(0, nch, body, init)
        return [jnp.sum(functools.reduce(lambda a, b: a + b, [x.astype(F32) for x in a]), axis=0, keepdims=True)
                for a in accs]

    def count(preds):
        return sweep(key_ref, SUBLANE, F32, lambda g, kk, off: preds[g](kk, off + sub))

    def count16(ref, preds):
        return sweep(ref, pk, I16, lambda g, kk, off: preds[g](kk))

    def packed(v):
        return jnp.broadcast_to(v - half, (pk, LANE)).astype(I16)

    def search16(ref, want):
        def bit(it, ans):
            cands = [a | jnp.left_shift(jnp.int32(1), 15 - it) for a in ans]
            cbs = [packed(c) for c in cands]
            cnt = count16(ref, [lambda kk, cb=cb: kk >= cb for cb in cbs])
            return tuple(jnp.where(cnt[g] >= want[g], cands[g], ans[g]) for g in gs)
        return lax.fori_loop(0, 16, bit, tuple(jnp.zeros((1, LANE), I32) for _ in gs))

    hi = search16(hi_ref, [topk] * grp)
    hbs = [packed(h) for h in hi]
    above = count16(hi_ref, [lambda kk, hb=hb: kk > hb for hb in hbs])
    want_lo = [topk - a for a in above]

    def mask_lo(c, carry):
        rows = pl.ds(pl.multiple_of(c * cw, cw), cw)
        for g in gs:
            hi_blk, lo_blk = hi_ref[g, rows, :], lo_ref[g, rows, :]
            lo_ref[g, rows, :] = jnp.concatenate(
                [jnp.where(hi_blk[u * pk:(u + 1) * pk] == hbs[g], lo_blk[u * pk:(u + 1) * pk], jnp.int16(-half))
                 for u in range(cw // pk)], axis=0)
        return carry

    lax.fori_loop(0, nch, mask_lo, 0)
    lo = search16(lo_ref, want_lo)
    thrk = [(jnp.left_shift(hi[g], 16) | lo[g]) ^ MIN_I32 for g in gs]

    n_gt = count([lambda kk, idx, t=t: kk > t for t in thrk])
    n_ge = count([lambda kk, idx, t=t: kk >= t for t in thrk])
    need = [topk - x for x in n_gt]

    def tie_bit(it, ms):
        cands = [m | jnp.left_shift(jnp.int32(1), idx_bits - 1 - it) for m in ms]
        cnt = count([lambda kk, idx, t=thrk[g], c=cands[g]: jnp.where(kk == t, idx, c) < c for g in gs])
        return tuple(jnp.where(cnt[g] < need[g], cands[g], ms[g]) for g in gs)

    any_tie = functools.reduce(jnp.maximum, [jnp.max(jnp.abs(x - topk)) for x in n_ge]) > 0.0
    cut = lax.cond(any_tie,
                   lambda: lax.fori_loop(0, idx_bits, tie_bit, tuple(jnp.zeros((1, LANE), I32) for _ in gs)),
                   lambda: tuple(jnp.full((1, LANE), 2 ** 30, I32) for _ in gs))
    for g in gs:
        thr_ref[:, g * LANE:(g + 1) * LANE] = _key_to_score(thrk[g], (SUBLANE, LANE))
        cut_ref[:, g * LANE:(g + 1) * LANE] = jnp.broadcast_to(cut[g], (SUBLANE, LANE))


def _select_cols(sct, topk, cw):
    nt, s, _ = sct.shape
    nqt = s // LANE
    grp = 2 if nqt % 2 == 0 else 1
    idx_bits = max(1, (s - 1).bit_length())
    out = pl.BlockSpec((SUBLANE, grp * LANE), lambda i: (0, i))
    return pl.pallas_call(
        functools.partial(_select_cols_kernel, topk=topk, cw=cw, idx_bits=idx_bits, nqt=nqt, grp=grp),
        grid=(nt // grp,),
        in_specs=[pl.BlockSpec((grp, s, LANE), lambda i: (i, 0, 0))],
        out_specs=[out, out],
        out_shape=[jax.ShapeDtypeStruct((SUBLANE, nt * LANE), F32), jax.ShapeDtypeStruct((SUBLANE, nt * LANE), I32)],
        scratch_shapes=[pltpu.VMEM((grp, s, LANE), I32), pltpu.VMEM((grp, s, LANE), I16),
                        pltpu.VMEM((grp, s, LANE), I16)],
        compiler_params=_cp(("parallel",)),
        name="topk_select_cols",
    )(sct)


def _dsa_attn_kernel(qt_ref, kt_ref, q_ref, k_ref, vt_ref, sc_ref, thr_ref, cut_ref, o_ref, m_ref, l_ref, acc_ref,
                     *, tq, tk):
    qi = qt_ref[pl.program_id(1)]
    ki = kt_ref[pl.program_id(1)]
    last = ((qi + 1) * tq - 1) // tk

    @pl.when(ki == 0)
    def _():
        m_ref[...] = jnp.full(m_ref.shape, NEG, F32)
        l_ref[...] = jnp.zeros(l_ref.shape, F32)
        acc_ref[...] = jnp.zeros(acc_ref.shape, F32)

    key = ki * tk + lax.broadcasted_iota(I32, (tk, tq), 0)
    qry = qi * tq + lax.broadcasted_iota(I32, (tk, tq), 1)
    sc = jnp.concatenate([sc_ref[c] for c in range(tq // LANE)], axis=1)
    thr = thr_ref[0:1, :]
    cut = cut_ref[0:1, :]
    sel = jnp.where(sc > thr, 1.0, jnp.where(sc == thr, jnp.where(key <= cut, 1.0, 0.0), 0.0))
    keep = jnp.where(key <= qry, sel, 0.0)
    keep3 = jnp.concatenate([keep, keep, keep], axis=1)
    ss = [_dot_nt(k_ref[g // 2], q_ref[3 * g:3 * g + 3].reshape(3 * tq, LANE)) for g in range(KV_HEADS)]
    for g in range(KV_HEADS):
        s = jnp.where(keep3 > 0.0, ss[g], NEG)
        m_new, a, p, l_new = _softmax_step(s, m_ref[g], l_ref[g], axis=0)
        vt = jnp.concatenate([vt_ref[c, g * HEAD_DIM:(g + 1) * HEAD_DIM, :] for c in range(vt_ref.shape[0])], axis=1)
        acc_ref[g] = a * acc_ref[g] + _dot(vt, p.astype(BF16))
        m_ref[g] = m_new
        l_ref[g] = l_new

    @pl.when(ki == last)
    def _():
        for gp in range(KV_HEADS // 2):
            oe = acc_ref[2 * gp] / l_ref[2 * gp]
            oo = acc_ref[2 * gp + 1] / l_ref[2 * gp + 1]
            for r in range(3):
                blk = gp * 3 + r
                ot = jnp.concatenate([oe[:, r * tq:(r + 1) * tq], oo[:, r * tq:(r + 1) * tq]], axis=0)
                o_ref[:, blk * LANE:(blk + 1) * LANE] = ot.T.astype(o_ref.dtype)


def _dsa_attn(q, k, vt, sct, thr, cut, batch, tq, tk):
    n = q.shape[1]
    s = n // batch
    nq, nk = s // tq, s // tk
    qtab, ktab = _causal_steps(nq, tq, tk)
    qcol = pl.BlockSpec((SUBLANE, tq), lambda b, t, qt, kt: (0, b * nq + qt[t]))
    grid_spec = pltpu.PrefetchScalarGridSpec(
        num_scalar_prefetch=2,
        grid=(batch, qtab.shape[0]),
        in_specs=[pl.BlockSpec((N_HEADS, tq, LANE), lambda b, t, qt, kt: (0, b * nq + qt[t], 0)),
                  pl.BlockSpec((KV_HEADS // 2, tk, LANE), lambda b, t, qt, kt: (0, b * nk + kt[t], 0)),
                  pl.BlockSpec((tk // vt.shape[2], KV_HEADS * HEAD_DIM, vt.shape[2]),
                               lambda b, t, qt, kt: (b * nk + kt[t], 0, 0)),
                  pl.BlockSpec((tq // LANE, tk, LANE), lambda b, t, qt, kt: (b * nq + qt[t], kt[t], 0)),
                  qcol, qcol],
        out_specs=pl.BlockSpec((tq, N_HEADS * HEAD_DIM), lambda b, t, qt, kt: (b * nq + qt[t], 0)),
        scratch_shapes=[pltpu.VMEM((KV_HEADS, 1, 3 * tq), F32), pltpu.VMEM((KV_HEADS, 1, 3 * tq), F32),
                        pltpu.VMEM((KV_HEADS, HEAD_DIM, 3 * tq), F32)],
    )
    return pl.pallas_call(
        functools.partial(_dsa_attn_kernel, tq=tq, tk=tk),
        grid_spec=grid_spec,
        out_shape=jax.ShapeDtypeStruct((n, N_HEADS * HEAD_DIM), BF16),
        compiler_params=_cp(("parallel", "arbitrary"), 56),
        name="dsa_attn",
    )(qtab, ktab, q, k, vt, sct, thr, cut)


def _dsa_decode_kernel(pt_ref, q_ref, sc_ref, scn_ref, thr_ref, cut_ref, knt_ref, vn_ref, *rest, pp, ns, t_new, past):
    kt_refs, vt_refs = rest[:pp], rest[pp:2 * pp]
    o_ref, m_ref, l_ref, acc_ref = rest[2 * pp:]
    p_id = pl.program_id(1)
    rows = q_ref.shape[1]
    reps = rows // t_new
    per = pp // ns
    width = per * PAGE

    @pl.when(p_id == 0)
    def _():
        m_ref[...] = jnp.full(m_ref.shape, NEG, F32)
        l_ref[...] = jnp.zeros(l_ref.shape, F32)
        acc_ref[...] = jnp.zeros(acc_ref.shape, F32)

    thr = thr_ref[0][:, :1]
    cut = cut_ref[0][:, :1]
    q = q_ref[0]

    def keep_rows(sc, col, extra):
        sel = jnp.where(sc > thr, 1.0, jnp.where(sc == thr, jnp.where(col <= cut, 1.0, 0.0), 0.0))
        if extra is not None:
            sel = jnp.where(extra, sel, 0.0)
        return jnp.concatenate([sel] * reps, axis=0)

    for st in range(ns):
        pages = range(st * per, (st + 1) * per)
        col = p_id * (pp * PAGE) + st * width + lax.broadcasted_iota(I32, (t_new, width), 1)
        keep = keep_rows(sc_ref[0, :, st * width:(st + 1) * width], col, None)
        s = jnp.concatenate([_dot(q, kt_refs[i][...].astype(BF16)) for i in pages], axis=1)
        s = jnp.where(keep > 0.0, s, NEG)
        m_new, a, p, l_new = _softmax_step(s, m_ref[st], l_ref[st])
        pb = p.astype(BF16)
        pv = None
        for n_, i in enumerate(pages):
            t = _dot_nt(pb[:, n_ * PAGE:(n_ + 1) * PAGE], vt_refs[i][...].astype(BF16))
            pv = t if pv is None else pv + t
        acc_ref[st] = a * acc_ref[st] + pv
        m_ref[st] = m_new
        l_ref[st] = l_new

    @pl.when(p_id == pl.num_programs(1) - 1)
    def _():
        m, l, acc = _merge_streams([m_ref[st] for st in range(ns)], [l_ref[st] for st in range(ns)],
                                   [acc_ref[st] for st in range(ns)])
        npad = vn_ref.shape[1]
        t = lax.broadcasted_iota(I32, (t_new, npad), 0)
        u = lax.broadcasted_iota(I32, (t_new, npad), 1)
        keep_n = keep_rows(scn_ref[0][:, :npad], past + u, u <= t)
        s2 = jnp.where(keep_n > 0.0, _dot(q, knt_ref[0].astype(BF16)), NEG)
        m2, a2, p2, l2 = _softmax_step(s2, m, l)
        o_ref[0] = (a2 * acc + _dot(p2.astype(BF16), vn_ref[0].astype(BF16))) / l2


def _dsa_decode(page_table, q, sc, scn, thr, cut, knt, v_new, pool_kt, pool_vt, j, pp, ns, t_new):
    db, rows, wd = q.shape
    npg = page_table.shape[1]
    pt = page_table.reshape(-1)
    npad = v_new.shape[1]
    per_b = lambda r, c: pl.BlockSpec((1, r, c), lambda b, p, pt_: (b, 0, 0))
    pool = lambda i: pl.BlockSpec((None, None, wd, PAGE), lambda b, p, pt_: (j, pt_[b * npg + p * pp + i], 0, 0))
    grid_spec = pltpu.PrefetchScalarGridSpec(
        num_scalar_prefetch=1,
        grid=(db, npg // pp),
        in_specs=[per_b(rows, wd), pl.BlockSpec((1, t_new, pp * PAGE), lambda b, p, pt_: (b, 0, p)),
                  per_b(t_new, LANE), per_b(t_new, LANE), per_b(t_new, LANE), per_b(wd, npad), per_b(npad, wd)]
        + [pool(i) for i in range(pp)] + [pool(i) for i in range(pp)],
        out_specs=per_b(rows, wd),
        scratch_shapes=[pltpu.VMEM((ns, rows, 1), F32), pltpu.VMEM((ns, rows, 1), F32),
                        pltpu.VMEM((ns, rows, wd), F32)],
    )
    return pl.pallas_call(
        functools.partial(_dsa_decode_kernel, pp=pp, ns=ns, t_new=t_new, past=npg * PAGE),
        grid_spec=grid_spec,
        out_shape=jax.ShapeDtypeStruct((db, rows, wd), F32),
        compiler_params=_cp(("parallel", "arbitrary")),
        name="dsa_decode",
    )(pt, q, sc, scn, thr, cut, knt, v_new, *([pool_kt] * pp), *([pool_vt] * pp))


def _mem_kv_kernel(m_ref, w_ref, kf_ref, vf_ref, kb_ref, vb_ref):
    y = _dot(m_ref[...].astype(BF16), w_ref[...])
    w = MEM_HEADS * HEAD_DIM
    kf_ref[...] = y[:, :w]
    vf_ref[...] = y[:, w:]
    kb_ref[...] = y[:, :w].astype(BF16)
    vb_ref[...] = y[:, w:].astype(BF16)


def _mem_kv(mem2d, w):
    n = mem2d.shape[0]
    wd = MEM_HEADS * HEAD_DIM
    wb = w.astype(BF16)
    spec = pl.BlockSpec((n, wd), lambda i: (0, 0))
    return pl.pallas_call(
        _mem_kv_kernel,
        grid=(1,),
        in_specs=[pl.BlockSpec(mem2d.shape, lambda i: (0, 0)), pl.BlockSpec(wb.shape, lambda i: (0, 0))],
        out_specs=[spec] * 4,
        out_shape=[jax.ShapeDtypeStruct((n, wd), F32)] * 2 + [jax.ShapeDtypeStruct((n, wd), BF16)] * 2,
        compiler_params=_cp(("arbitrary",)),
        name="mem_kv",
    )(mem2d, wb)


def _mem_attend_kernel(q_ref, mk_ref, mv_ref, o_ref, *, bb, kv_t):
    tt = q_ref.shape[1]
    lane = lax.broadcasted_iota(I32, (tt, LANE), 1)
    low = lane < HEAD_DIM
    for b in range(bb):
        for blk in range(MEM_HEADS // 2):
            sl = slice(blk * LANE, (blk + 1) * LANE)
            qb = q_ref[b, :, sl]
            mkb = (mk_ref[b, sl, :] if kv_t else mk_ref[b, :, sl]).astype(BF16)
            mvb = (mv_ref[b, sl, :] if kv_t else mv_ref[b, :, sl]).astype(BF16)
            res = []
            for half in range(2):
                qh = jnp.where(low if half == 0 else jnp.logical_not(low), qb, jnp.zeros_like(qb))
                s = _dot(qh, mkb) if kv_t else _dot_nt(qh, mkb)
                p = jnp.exp(s - jnp.max(s, axis=1, keepdims=True))
                p = (p / jnp.sum(p, axis=1, keepdims=True)).astype(BF16)
                res.append(_dot_nt(p, mvb) if kv_t else _dot(p, mvb))
            o_ref[b, :, sl] = jnp.where(low, res[0], res[1]).astype(o_ref.dtype)


def _mem_attend(q, mk, mv, bb, tt, kv_t, layer=None):
    bm, t, wd = q.shape
    if layer is None:
        kv = pl.BlockSpec((bb,) + mk.shape[1:], lambda b, i: (b, 0, 0))
    else:
        kv = pl.BlockSpec((None, bb) + mk.shape[2:], lambda b, i: (layer, b, 0, 0))
    return pl.pallas_call(
        functools.partial(_mem_attend_kernel, bb=bb, kv_t=kv_t),
        grid=(bm // bb, t // tt),
        in_specs=[pl.BlockSpec((bb, tt, wd), lambda b, i: (b, i, 0)), kv, kv],
        out_specs=pl.BlockSpec((bb, tt, wd), lambda b, i: (b, i, 0)),
        out_shape=jax.ShapeDtypeStruct((bm, t, wd), BF16),
        compiler_params=_cp(("parallel", "parallel")),
        name="mem_attend",
    )(q, mk, mv)


def _split_bf16(x):
    hi = x.astype(BF16)
    return hi, (x - hi.astype(F32)).astype(BF16)


def _outproj_router_kernel(x_ref, mix_ref, mem_ref, wo1_ref, wo2_ref, g_ref, b_ref, wrh_ref, wrl_ref, rb_ref,
                           x1_ref, gate_ref):
    y = ALPHA * x_ref[...] + _dot(mix_ref[...].astype(BF16), wo1_ref[...]) + _dot(mem_ref[...], wo2_ref[...])
    x1 = _ln(y, g_ref[...], b_ref[...])
    x1_ref[...] = x1
    xh, xl = _split_bf16(x1)
    logit = _dot_nt(wrh_ref[...], xh) + _dot_nt(wrh_ref[...], xl) + _dot_nt(wrl_ref[...], xh)
    s = 1.0 / (1.0 + jnp.exp(-logit))
    sb = s + rb_ref[...]
    epg = N_EXPERTS // N_GROUPS
    rows = [sb[e:e + 1] for e in range(N_EXPERTS)]
    gscore = []
    for g in range(N_GROUPS):
        v = rows[g * epg:(g + 1) * epg]
        best = None
        for a in range(epg):
            for b in range(a + 1, epg):
                pr = v[a] + v[b]
                best = pr if best is None else jnp.maximum(best, pr)
        gscore.append(best)
    gmax = functools.reduce(jnp.maximum, gscore)
    taken = jnp.zeros_like(gmax)
    gsel = []
    for g in range(N_GROUPS):
        pick = jnp.where(gscore[g] == gmax, 1.0, 0.0) * (1.0 - taken)
        taken = taken + pick
        gsel.append(pick)
    val = [jnp.where(gsel[e // epg] > 0.0, rows[e], NEG) for e in range(N_EXPERTS)]
    chosen = [jnp.zeros_like(gmax) for _ in range(N_EXPERTS)]
    for _ in range(2):
        vmax = functools.reduce(jnp.maximum, val)
        taken = jnp.zeros_like(gmax)
        for e in range(N_EXPERTS):
            pick = jnp.where(val[e] == vmax, 1.0, 0.0) * (1.0 - taken)
            taken = taken + pick
            chosen[e] = chosen[e] + pick
            val[e] = jnp.where(pick > 0.0, 2.0 * NEG, val[e])
    wsel = [jnp.where(chosen[e] > 0.0, s[e:e + 1], 0.0) for e in range(N_EXPERTS)]
    wsum = functools.reduce(lambda a, b: a + b, wsel)
    gate_ref[...] = jnp.concatenate([w / wsum for w in wsel], axis=0)


def _outproj_router(x, mix, mem, wo1, wo2, g, b, wrh, wrl, rb, tm):
    n = x.shape[0]
    full = lambda a: pl.BlockSpec(a.shape, lambda i: (0,) * a.ndim)
    rows = lambda w: pl.BlockSpec((tm, w), lambda i: (i, 0))
    return pl.pallas_call(
        _outproj_router_kernel,
        grid=(n // tm,),
        in_specs=[rows(D_MODEL), rows(mix.shape[1]), rows(mem.shape[1]), full(wo1), full(wo2), full(g), full(b),
                  full(wrh), full(wrl), full(rb)],
        out_specs=[rows(D_MODEL), pl.BlockSpec((N_EXPERTS, tm), lambda i: (0, i))],
        out_shape=[jax.ShapeDtypeStruct((n, D_MODEL), F32), jax.ShapeDtypeStruct((N_EXPERTS, n), F32)],
        compiler_params=_cp(("parallel",)),
        name="outproj_router",
    )(x, mix, mem, wo1, wo2, g, b, wrh, wrl, rb)


def _moe_kernel(x_ref, gate_ref, wgu_ref, wd_ref, g_ref, b_ref, o_ref, acc_ref, xb_ref):
    e = pl.program_id(1)

    @pl.when(e == 0)
    def _():
        acc_ref[...] = jnp.zeros(acc_ref.shape, F32)
        xb_ref[...] = x_ref[...].astype(BF16)

    gate = gate_ref[...]
    lane = lax.broadcasted_iota(I32, gate.shape, 1)
    gcol = jnp.sum(jnp.where(lane == e, gate, 0.0), axis=1, keepdims=True)
    hgu = _dot(xb_ref[...], wgu_ref[...])
    hg, hu = hgu[:, :D_EXPERT], hgu[:, D_EXPERT:]
    h = hg * (1.0 / (1.0 + jnp.exp(-hg))) * hu * gcol
    acc_ref[...] += _dot(h.astype(BF16), wd_ref[...])

    @pl.when(e == pl.num_programs(1) - 1)
    def _():
        o_ref[...] = _ln(ALPHA * x_ref[...] + acc_ref[...], g_ref[...], b_ref[...])


def _moe(x, gate, wgu, wd, g, b, tm):
    n = x.shape[0]
    full = lambda a: pl.BlockSpec(a.shape, lambda i, e: (0,) * a.ndim)
    return pl.pallas_call(
        _moe_kernel,
        grid=(n // tm, N_EXPERTS),
        in_specs=[pl.BlockSpec((tm, D_MODEL), lambda i, e: (i, 0)), pl.BlockSpec((tm, N_EXPERTS), lambda i, e: (i, 0)),
                  pl.BlockSpec((None, D_MODEL, 2 * D_EXPERT), lambda i, e: (e, 0, 0)),
                  pl.BlockSpec((None, D_EXPERT, D_MODEL), lambda i, e: (e, 0, 0)), full(g), full(b)],
        out_specs=pl.BlockSpec((tm, D_MODEL), lambda i, e: (i, 0)),
        out_shape=jax.ShapeDtypeStruct((n, D_MODEL), F32),
        scratch_shapes=[pltpu.VMEM((tm, D_MODEL), F32), pltpu.VMEM((tm, D_MODEL), BF16)],
        compiler_params=_cp(("parallel", "arbitrary")),
        name="moe",
    )(x, gate, wgu, wd, g, b)


def _tile(n, pref):
    t = min(n, pref)
    while n % t:
        t //= 2
    return t


def kernel(x_prompt, x_sample, cache_mla_ckv, cache_mla_krope, cache_dsa_k, cache_dsa_v, cache_dsa_kidx, cache_mem_k, cache_mem_v, page_table, mem_prompt, w_in_mla, mla_q_norm, mla_kv_norm, w_uq, w_uk, w_uv, w_in_dsa, idx_k_norm_g, idx_k_norm_b, w_mem_kv, w_out, ln1_g, ln1_b, ln2_g, ln2_b, w_router, router_bias, w_gate, w_up, w_down):
    B, S, _ = x_prompt.shape
    DB, T, _ = x_sample.shape
    n_mem = mem_prompt.shape[1]
    npg = page_table.shape[1]
    past = npg * PAGE
    n_pool = cache_mla_ckv.shape[1]
    NP, NS = B * S, DB * T
    TPAD = 16
    pp_dsa = pp_idx = 32 if npg % 32 == 0 else npg
    pp_mla = 32 if npg % 32 == 0 else npg
    ns_dsa = 4 if pp_dsa % 4 == 0 else 1
    ns_mla = 4 if pp_mla % 4 == 0 else 1
    tm_p, tm_s = _tile(NP, 512), _tile(NS, 512)
    t_att = _tile(S, 512)
    KVW = KV_HEADS * HEAD_DIM

    pos_p = jnp.arange(S, dtype=jnp.int32)
    pos_s = jnp.tile(past + jnp.arange(T, dtype=jnp.int32), DB)
    cos_mp, sin_mp = _rope_tables(pos_p, QK_ROPE, LANE, QK_NOPE, LANE)
    cos_ms, sin_ms = _rope_tables(pos_s, QK_ROPE, LANE, QK_NOPE, LANE)
    cos_mr, sin_mr = _rope_tables(pos_s, QK_ROPE, QK_ROPE, 0, N_HEADS * QK_ROPE)
    cos_dp, sin_dp = _rope_tables(pos_p, ROT_DIM, HEAD_DIM, 0, LANE)
    cos_ds, sin_ds = _rope_tables(pos_s, ROT_DIM, HEAD_DIM, 0, LANE)

    wrt = w_router.T
    wrh = wrt.astype(BF16)
    wrl = (wrt - wrh.astype(F32)).astype(BF16)
    rb = router_bias.reshape(N_EXPERTS, 1).astype(F32)
    perm = jnp.asarray(DSA_PERM)
    pool_kt = jnp.transpose(cache_dsa_k, (0, 1, 3, 4, 2)).reshape(-1, n_pool, KVW, PAGE)
    pool_vt = jnp.transpose(cache_dsa_v, (0, 1, 3, 4, 2)).reshape(-1, n_pool, KVW, PAGE)
    pool_kidxt = jnp.transpose(cache_dsa_kidx, (0, 1, 3, 2))
    pool_krt = jnp.transpose(cache_mla_krope, (0, 1, 3, 2))
    mem_kt = jnp.transpose(cache_mem_k, (0, 1, 3, 4, 2)).reshape(DEPTH, DB, MEM_HEADS * HEAD_DIM, n_mem)
    mem_vt = jnp.transpose(cache_mem_v, (0, 1, 3, 4, 2)).reshape(DEPTH, DB, MEM_HEADS * HEAD_DIM, n_mem)
    mem2d = mem_prompt.reshape(B * n_mem, D_MODEL)

    def pad_new(a):
        a = a.reshape(DB, T, a.shape[-1])
        return jnp.concatenate([a, jnp.zeros((DB, TPAD - T, a.shape[-1]), a.dtype)], axis=1)

    def post(x, mix, mem, i, wo_rows, tm):
        wo = w_out[i]
        wo1 = wo[:N_HEADS * HEAD_DIM]
        if wo_rows is not None:
            wo1 = wo1.reshape(N_HEADS, HEAD_DIM, D_MODEL)[wo_rows].reshape(N_HEADS * HEAD_DIM, D_MODEL)
        x1, gate_t = _outproj_router(x, mix, mem, wo1.astype(BF16), wo[N_HEADS * HEAD_DIM:].astype(BF16),
                                     ln1_g[i].reshape(1, -1), ln1_b[i].reshape(1, -1), wrh, wrl, rb, tm)
        wgu = jnp.concatenate([w_gate[i], w_up[i]], axis=-1).astype(BF16)
        return _moe(x1, gate_t.T, wgu, w_down[i].astype(BF16), ln2_g[i].reshape(1, -1), ln2_b[i].reshape(1, -1),
                    _tile(x.shape[0], 1024))

    xp = x_prompt.reshape(NP, D_MODEL)
    xs = x_sample.reshape(NS, D_MODEL)
    ckv_p, kr_p, ckv_s, kr_s = [], [], [], []
    k_p, v_p, ki_p, k_s, v_s, ki_s = [], [], [], [], [], []
    mk_list, mv_list = [], []
    for i in range(DEPTH):
        j = i // 2
        mkf, mvf, mkb, mvb = _mem_kv(mem2d, w_mem_kv[i])
        mk_list.append(mkf.reshape(B, n_mem, MEM_HEADS, HEAD_DIM))
        mv_list.append(mvf.reshape(B, n_mem, MEM_HEADS, HEAD_DIM))
        if i % 2 == 0:
            q, k, vt, ckv, krpad, qm_p = _mla_proj_prompt(xp, w_in_mla[j], mla_q_norm[j], mla_kv_norm[j], w_uq[j],
                                                          w_uk[j], w_uv[j], cos_mp, sin_mp, tm_p)
            mix_p = _mla_flash(q, k, vt, B, _tile(S, 1024), _tile(S, 1024))
            ckv_p.append(ckv.reshape(B, S, KV_LORA))
            kr_p.append(krpad[:, QK_NOPE:QK_NOPE + QK_ROPE].reshape(B, S, QK_ROPE))
            qlat, qr, ckv, krpad, qm_s = _mla_proj_sample(xs, w_in_mla[j], mla_q_norm[j], mla_kv_norm[j], w_uq[j],
                                                          w_uk[j], cos_ms, sin_ms, cos_mr, sin_mr, tm_s)
            kr = krpad[:, QK_NOPE:QK_NOPE + QK_ROPE]
            ckv_s.append(ckv.reshape(DB, T, KV_LORA))
            kr_s.append(kr.reshape(DB, T, QK_ROPE))
            qlat = qlat.reshape(DB, T, N_HEADS, KV_LORA).transpose(0, 2, 1, 3).reshape(DB, N_HEADS * T, KV_LORA)
            qr = qr.reshape(DB, T, N_HEADS, QK_ROPE).transpose(0, 2, 1, 3).reshape(DB, N_HEADS * T, QK_ROPE)
            o_lat = _mla_decode(page_table, qlat, qr, pad_new(ckv), pad_new(kr), cache_mla_ckv, pool_krt,
                                j, pp_mla, ns_mla, T)
            o_lat = o_lat.reshape(DB, N_HEADS, T, KV_LORA).transpose(1, 0, 2, 3).reshape(N_HEADS, NS, KV_LORA)
            o = _head_matmul(o_lat, jnp.transpose(w_uv[j], (1, 0, 2)).astype(BF16))
            mix_s = o.transpose(1, 0, 2).reshape(NS, N_HEADS * HEAD_DIM)
            wo_rows = None
        else:
            q, kb, kf, vf, vt, wt, qidx, kw, kib, qm_p = _dsa_proj(xp, w_in_dsa[j], idx_k_norm_g[j],
                                                                   idx_k_norm_b[j], cos_dp, sin_dp, tm_p)
            k_p.append(kf.reshape(B, S, KV_HEADS, HEAD_DIM))
            v_p.append(vf.reshape(B, S, KV_HEADS, HEAD_DIM))
            ki_p.append(kw[:, :IDX_DIM].reshape(B, S, IDX_DIM))
            sct = _indexer_prompt(qidx, kib, wt, B, t_att, t_att)
            thr, cut = _select_cols(sct, min(TOPK_MAX, S // 4), _tile(S, 512))
            mix_p = _dsa_attn(q, kb, vt, sct, thr, cut, B, t_att, t_att)

            q, kb, kf, vf, vt, wt, qidx, kw, kib, qm_s = _dsa_proj(xs, w_in_dsa[j], idx_k_norm_g[j],
                                                                   idx_k_norm_b[j], cos_ds, sin_ds, tm_s)
            k_s.append(kf.reshape(DB, T, KV_HEADS, HEAD_DIM))
            v_s.append(vf.reshape(DB, T, KV_HEADS, HEAD_DIM))
            kidx_new = kw[:, :IDX_DIM]
            ki_s.append(kidx_new.reshape(DB, T, IDX_DIM))
            qc = jnp.stack([qidx[h, :, (h % 2) * HEAD_DIM:(h % 2 + 1) * HEAD_DIM] for h in range(IDX_HEADS)])
            qc = qc.reshape(IDX_HEADS, DB, T, IDX_DIM).transpose(1, 0, 2, 3).reshape(DB, IDX_HEADS * T, IDX_DIM)
            w_s = kw[:, HEAD_DIM:HEAD_DIM + IDX_HEADS].reshape(DB, T, IDX_HEADS)
            sc_past, sc_new = _indexer_sample(page_table, qc, w_s, pad_new(kidx_new), pool_kidxt, j, pp_idx, T)
            topk_s = min(TOPK_MAX, (past + T) // 4)
            cw_s = 5 * LANE if (past + LANE) % (5 * LANE) == 0 else LANE
            sc_all = jnp.concatenate([sc_past, sc_new], axis=-1).reshape(NS, past + LANE)
            thr, cut = _select_rows(sc_all, topk_s, cw_s, _tile(NS, 128))
            qh = jnp.stack([q[h, :, ((h // 3) % 2) * HEAD_DIM:((h // 3) % 2 + 1) * HEAD_DIM] for h in range(N_HEADS)])
            qh = qh.reshape(KV_HEADS, 3, DB, T, HEAD_DIM).transpose(2, 0, 1, 3, 4)
            eye = jnp.eye(KV_HEADS, dtype=BF16)
            qbd = (qh[:, :, :, :, None, :] * eye[None, :, None, None, :, None]).reshape(DB, N_HEADS * T, KVW)
            o = _dsa_decode(page_table, qbd, sc_past, sc_new, thr.reshape(DB, T, LANE), cut.reshape(DB, T, LANE),
                            pad_new(kf).transpose(0, 2, 1), pad_new(vf), pool_kt, pool_vt, j, pp_dsa, ns_dsa, T)
            o = o.reshape(DB, KV_HEADS, 3, T, KV_HEADS, HEAD_DIM)
            o = jnp.stack([o[:, g, :, :, g, :] for g in range(KV_HEADS)], axis=1)
            o = o.reshape(DB, N_HEADS, T, HEAD_DIM)[:, perm]
            mix_s = o.transpose(0, 2, 1, 3).reshape(NS, N_HEADS * HEAD_DIM)
            wo_rows = perm
        memo_p = _mem_attend(qm_p.reshape(B, S, -1), mkb.reshape(B, n_mem, -1), mvb.reshape(B, n_mem, -1), 1,
                             _tile(S, 512), False)
        qm_s3 = qm_s.reshape(DB, T, -1)
        qm_s3 = jnp.concatenate([qm_s3, jnp.zeros((DB, TPAD - T, qm_s3.shape[-1]), BF16)], axis=1)
        memo_s = _mem_attend(qm_s3, mem_kt, mem_vt, _tile(DB, 8), TPAD, True, layer=i)[:, :T]
        xp = post(xp, mix_p, memo_p.reshape(NP, -1), i, wo_rows, tm_p)
        xs = post(xs, mix_s, memo_s.reshape(NS, -1), i, wo_rows, tm_s)
    return (xp.reshape(B, S, D_MODEL), xs.reshape(DB, T, D_MODEL),
            jnp.stack(ckv_p), jnp.stack(kr_p), jnp.stack(k_p), jnp.stack(v_p), jnp.stack(ki_p),
            jnp.stack(mk_list), jnp.stack(mv_list),
            jnp.stack(ckv_s), jnp.stack(kr_s), jnp.stack(k_s), jnp.stack(v_s), jnp.stack(ki_s))
```
